```python
import jax, jax.numpy as jnp
from jax import lax
import numpy as np

D_MODEL = 1024
BATCH = 1
SEQ = 16384
DEPTH = 1

HEAD_DIM = 64
NSA_HEADS = 8
NSA_KV_HEADS = 2
NSA_GROUP = NSA_HEADS // NSA_KV_HEADS
SB_HEADS = 8
MIX_WIDTH = (NSA_HEADS + SB_HEADS) * HEAD_DIM
ROPE_THETA = 500000.0
ROPE_DIM = HEAD_DIM // 4
CMP_LEN = 32
CMP_STRIDE = 16
CMP_HIDDEN = 256
SEL_LEN = 64
SEL_TOP = 16
SEL_BONUS = 1.0e4
WINDOW = 512
Q_BLOCK = 128
N_GROUPS = 4
EXPERTS_PER_GROUP = 4
N_EXPERTS = N_GROUPS * EXPERTS_PER_GROUP
TOP_K_IN_GROUP = 2
EXPERT_FF = 512
MOE_BLOCK = 128
EPS = 1e-6
NEG = -1e30

N_Q = NSA_HEADS * HEAD_DIM
N_KV = 3 * 2 * NSA_KV_HEADS * HEAD_DIM
N_GATE = NSA_HEADS * 3
N_SB = 3 * SB_HEADS * HEAD_DIM
IN_COLS = N_Q + N_KV + N_GATE + N_SB

kernel_name = "hymba_nsa_stickbreak_hiermoe"


def rms_norm(x, w):
    xf = x.astype(jnp.float32)
    y = xf * lax.rsqrt(jnp.mean(xf * xf, axis=-1, keepdims=True) + EPS)
    return (y * w.astype(jnp.float32)).astype(x.dtype)


def rope_partial(x, cos, sin):
    half = ROPE_DIM // 2
    xr = x[..., :ROPE_DIM].astype(jnp.float32)
    x1, x2 = xr[..., :half], xr[..., half:]
    rot = jnp.concatenate([x1 * cos - x2 * sin, x2 * cos + x1 * sin], axis=-1)
    return jnp.concatenate([rot.astype(x.dtype), x[..., ROPE_DIM:]], axis=-1)


def masked_softmax(s, m):
    s = jnp.where(m, s, NEG)
    mx = jnp.max(s, axis=-1, keepdims=True)
    e = jnp.where(m, jnp.exp(s - mx), 0.0)
    return e / jnp.maximum(jnp.sum(e, axis=-1, keepdims=True), 1e-30)


def compress(k, pe, w1, w2):
    B, T, G, dh = k.shape
    n_cmp = (T - CMP_LEN) // CMP_STRIDE + 1
    idx = jnp.arange(n_cmp)[:, None] * CMP_STRIDE + jnp.arange(CMP_LEN)[None, :]
    blocks = k[:, idx] + pe[:, None, :].astype(k.dtype)
    blocks = blocks.transpose(0, 3, 1, 2, 4).reshape(B, G, n_cmp, CMP_LEN * dh)
    return jax.nn.silu(blocks @ w1) @ w2


def gather_blocks(blocks, idx):
    return jax.vmap(jax.vmap(lambda bl, ix: bl[ix]))(blocks, idx)


def nsa_attention(q, k_cmp, v_cmp, k_sel, v_sel, k_win, v_win, gates, cmp_pe_k, cmp_pe_v,
                  cmp_k_w1, cmp_k_w2, cmp_v_w1, cmp_v_w2):
    B, T, H, dh = q.shape
    G, R = NSA_KV_HEADS, NSA_GROUP
    scale = dh ** -0.5
    kc = compress(k_cmp, cmp_pe_k, cmp_k_w1, cmp_k_w2)
    vc = compress(v_cmp, cmp_pe_v, cmp_v_w1, cmp_v_w2)
    n_cmp = kc.shape[2]
    cmp_start = jnp.arange(n_cmp) * CMP_STRIDE
    cmp_end = cmp_start + CMP_LEN - 1
    n_sel = T // SEL_LEN
    n_top = min(SEL_TOP, n_sel)
    sel_start = jnp.arange(n_sel) * SEL_LEN
    overlap = ((cmp_start[:, None] < sel_start[None, :] + SEL_LEN)
               & (cmp_start[:, None] + CMP_LEN > sel_start[None, :])).astype(jnp.float32)
    ks_blocks = k_sel.reshape(B, n_sel, SEL_LEN, G, dh).transpose(0, 3, 1, 2, 4)
    vs_blocks = v_sel.reshape(B, n_sel, SEL_LEN, G, dh).transpose(0, 3, 1, 2, 4)
    pad = ((0, 0), (WINDOW, 0), (0, 0), (0, 0))
    kw_pad = jnp.pad(k_win, pad)
    vw_pad = jnp.pad(v_win, pad)
    jsel = jnp.arange(n_sel)

    def block(b):
        q0 = b * Q_BLOCK
        t = q0 + jnp.arange(Q_BLOCK)
        qb = lax.dynamic_slice_in_dim(q, q0, Q_BLOCK, axis=1).reshape(B, Q_BLOCK, G, R, dh)
        s_c = jnp.einsum('bqgrd,bgnd->bgrqn', qb, kc).astype(jnp.float32) * scale
        p_c = masked_softmax(s_c, cmp_end[None, :] <= t[:, None])
        o_c = jnp.einsum('bgrqn,bgnd->bqgrd', p_c.astype(vc.dtype), vc)
        imp = jnp.einsum('bgrqn,nm->bgqm', p_c, overlap)
        bq = t // SEL_LEN
        forced = (jsel[None, :] == 0) | (jsel[None, :] == bq[:, None]) | (jsel[None, :] == bq[:, None] - 1)
        allowed = jsel[None, :] <= bq[:, None]
        score = jnp.where(allowed, imp + jnp.where(forced, SEL_BONUS, 0.0), NEG)
        _, sel_idx = lax.top_k(score, n_top)
        kg = gather_blocks(ks_blocks, sel_idx).reshape(B, G, Q_BLOCK, n_top * SEL_LEN, dh)
        vg = gather_blocks(vs_blocks, sel_idx).reshape(B, G, Q_BLOCK, n_top * SEL_LEN, dh)
        tok = (sel_idx[..., None] * SEL_LEN + jnp.arange(SEL_LEN)).reshape(B, G, Q_BLOCK, n_top * SEL_LEN)
        m_s = (tok <= t[None, None, :, None])[:, :, None]
        s_s = jnp.einsum('bqgrd,bgqkd->bgrqk', qb, kg).astype(jnp.float32) * scale
        p_s = masked_softmax(s_s, m_s)
        o_s = jnp.einsum('bgrqk,bgqkd->bqgrd', p_s.astype(vg.dtype), vg)
        kwb = lax.dynamic_slice_in_dim(kw_pad, q0, WINDOW + Q_BLOCK, axis=1)
        vwb = lax.dynamic_slice_in_dim(vw_pad, q0, WINDOW + Q_BLOCK, axis=1)
        kpos = q0 - WINDOW + jnp.arange(WINDOW + Q_BLOCK)
        m_w = (kpos[None, :] <= t[:, None]) & (t[:, None] - kpos[None, :] < WINDOW) & (kpos[None, :] >= 0)
        s_w = jnp.einsum('bqgrd,bkgd->bgrqk', qb, kwb).astype(jnp.float32) * scale
        p_w = masked_softmax(s_w, m_w)
        o_w = jnp.einsum('bgrqk,bkgd->bqgrd', p_w.astype(vwb.dtype), vwb)
        gb = lax.dynamic_slice_in_dim(gates, q0, Q_BLOCK, axis=1).reshape(B, Q_BLOCK, G, R, 3)
        o = gb[..., 0:1] * o_c + gb[..., 1:2] * o_s + gb[..., 2:3] * o_w
        return o.reshape(B, Q_BLOCK, H * dh)

    out = lax.map(block, jnp.arange(T // Q_BLOCK))
    return out.transpose(1, 0, 2, 3).reshape(B, T, H * dh)


def stick_breaking_attention(q, k, v):
    B, T, H, dh = q.shape
    scale = dh ** -0.5
    s_pos = jnp.arange(T)

    def block(b):
        q0 = b * Q_BLOCK
        t = q0 + jnp.arange(Q_BLOCK)
        qb = lax.dynamic_slice_in_dim(q, q0, Q_BLOCK, axis=1)
        z = jnp.einsum('bqhd,bshd->bhqs', qb, k).astype(jnp.float32) * scale
        mask = s_pos[None, :] < t[:, None]
        log_not = jnp.where(mask, jax.nn.log_sigmoid(-z), 0.0)
        after = lax.cumsum(log_not, axis=3, reverse=True) - log_not
        a = jnp.where(mask, jnp.exp(jax.nn.log_sigmoid(z) + after), 0.0)
        o = jnp.einsum('bhqs,bshd->bqhd', a.astype(v.dtype), v)
        return o.reshape(B, Q_BLOCK, H * dh)

    out = lax.map(block, jnp.arange(T // Q_BLOCK))
    return out.transpose(1, 0, 2, 3).reshape(B, T, H * dh)


def hybrid_mixer(h, positions, w_in, cmp_pe_k, cmp_pe_v, cmp_k_w1, cmp_k_w2, cmp_v_w1, cmp_v_w2,
                 nsa_out_norm_w, sb_out_norm_w, w_out):
    B, T, _ = h.shape
    proj = h @ w_in
    o1, o2, o3 = N_Q, N_Q + N_KV, N_Q + N_KV + N_GATE
    q_n = proj[..., :o1].reshape(B, T, NSA_HEADS, HEAD_DIM)
    kv = proj[..., o1:o2].reshape(B, T, 3, 2, NSA_KV_HEADS, HEAD_DIM)
    gates = jax.nn.sigmoid(proj[..., o2:o3].reshape(B, T, NSA_HEADS, 3))
    sb = proj[..., o3:].reshape(B, T, 3, SB_HEADS, HEAD_DIM)
    inv_freq = ROPE_THETA ** (-jnp.arange(0, ROPE_DIM, 2, dtype=jnp.float32) / ROPE_DIM)
    ang = positions.astype(jnp.float32)[..., None] * inv_freq
    cos, sin = jnp.cos(ang)[:, :, None, :], jnp.sin(ang)[:, :, None, :]
    q_n = rope_partial(q_n, cos, sin)
    k_sel = rope_partial(kv[:, :, 1, 0], cos, sin)
    k_win = rope_partial(kv[:, :, 2, 0], cos, sin)
    o_nsa = nsa_attention(q_n, kv[:, :, 0, 0], kv[:, :, 0, 1], k_sel, kv[:, :, 1, 1], k_win, kv[:, :, 2, 1],
                          gates, cmp_pe_k, cmp_pe_v, cmp_k_w1, cmp_k_w2, cmp_v_w1, cmp_v_w2)
    o_sb = stick_breaking_attention(sb[:, :, 0], sb[:, :, 1], sb[:, :, 2])
    o = jnp.concatenate([rms_norm(o_nsa, nsa_out_norm_w), rms_norm(o_sb, sb_out_norm_w)], axis=-1)
    return o @ w_out


def hierarchical_moe(h, rg_w, rg_b, re_w, re_b, w_gate, w_up, w_down):
    B, T, D = h.shape
    hf = h.reshape(B * T, D)
    N = B * T
    g_logits = (hf @ rg_w).astype(jnp.float32) + rg_b
    grp = jnp.argmax(g_logits, axis=-1)
    g_gate = jnp.take_along_axis(jax.nn.softmax(g_logits, axis=-1), grp[:, None], axis=1)[:, 0]
    e_logits = ((hf @ re_w).astype(jnp.float32) + re_b).reshape(N, N_GROUPS, EXPERTS_PER_GROUP)
    e_logits = jnp.take_along_axis(e_logits, grp[:, None, None], axis=1)[:, 0]
    top_val, top_loc = lax.top_k(e_logits, TOP_K_IN_GROUP)
    weight = g_gate[:, None] * jax.nn.softmax(top_val, axis=-1)
    expert = grp[:, None] * EXPERTS_PER_GROUP + top_loc
    flat_e = expert.reshape(-1)
    flat_t = jnp.repeat(jnp.arange(N, dtype=jnp.int32), TOP_K_IN_GROUP)
    flat_w = weight.reshape(-1)
    A = N * TOP_K_IN_GROUP
    order = jnp.argsort(flat_e)
    se = flat_e[order]
    counts = jnp.bincount(flat_e, length=N_EXPERTS)
    padded = ((counts + MOE_BLOCK - 1) // MOE_BLOCK) * MOE_BLOCK
    start = jnp.cumsum(counts) - counts
    pend = jnp.cumsum(padded)
    pstart = pend - padded
    dest = pstart[se] + jnp.arange(A) - start[se]
    R = (-(-A // MOE_BLOCK) + N_EXPERTS) * MOE_BLOCK
    nb = R // MOE_BLOCK
    row_tok = jnp.zeros((R,), jnp.int32).at[dest].set(flat_t[order])
    row_w = jnp.zeros((R,), jnp.float32).at[dest].set(flat_w[order])
    blk_exp = jnp.minimum(jnp.searchsorted(pend, jnp.arange(nb) * MOE_BLOCK, side='right'), N_EXPERTS - 1)

    def expert_block(args):
        toks, e = args
        xb = hf[toks]
        return (jax.nn.silu(xb @ w_gate[e]) * (xb @ w_up[e])) @ w_down[e]

    y = lax.map(expert_block, (row_tok.reshape(nb, MOE_BLOCK), blk_exp)).reshape(R, D)
    y = y * row_w[:, None].astype(y.dtype)
    out = jnp.zeros((N, D), h.dtype).at[row_tok].add(y)
    return out.reshape(B, T, D)


def setup_inputs(seed: int = 0) -> dict:
    key = jax.random.key(seed)
    ks = jax.random.split(key, 24)
    L, D = DEPTH, D_MODEL
    nrm = lambda k, shape, fan: jax.random.normal(k, shape, jnp.float32) * (fan ** -0.5)
    gain = lambda k, shape: 1.0 + 0.02 * jax.random.normal(k, shape, jnp.float32)
    return {
        "x": jax.random.normal(ks[0], (BATCH, SEQ, D), jnp.float32),
        "positions": jnp.broadcast_to(jnp.arange(SEQ, dtype=jnp.int32), (BATCH, SEQ)),
        "attn_norm_w": gain(ks[1], (L, D)),
        "w_in": nrm(ks[2], (L, D, IN_COLS), D),
        "cmp_pe_k": 0.02 * jax.random.normal(ks[3], (L, CMP_LEN, HEAD_DIM), jnp.float32),
        "cmp_pe_v": 0.02 * jax.random.normal(ks[4], (L, CMP_LEN, HEAD_DIM), jnp.float32),
        "cmp_k_w1": nrm(ks[5], (L, CMP_LEN * HEAD_DIM, CMP_HIDDEN), CMP_LEN * HEAD_DIM),
        "cmp_k_w2": nrm(ks[6], (L, CMP_HIDDEN, HEAD_DIM), CMP_HIDDEN),
        "cmp_v_w1": nrm(ks[7], (L, CMP_LEN * HEAD_DIM, CMP_HIDDEN), CMP_LEN * HEAD_DIM),
        "cmp_v_w2": nrm(ks[8], (L, CMP_HIDDEN, HEAD_DIM), CMP_HIDDEN),
        "nsa_out_norm_w": gain(ks[9], (L, NSA_HEADS * HEAD_DIM)),
        "sb_out_norm_w": gain(ks[10], (L, SB_HEADS * HEAD_DIM)),
        "w_out": nrm(ks[11], (L, MIX_WIDTH, D), MIX_WIDTH),
        "ffn_norm_w": gain(ks[12], (L, D)),
        "router_group_w": nrm(ks[13], (L, D, N_GROUPS), D),
        "router_group_b": 0.01 * jax.random.normal(ks[14], (L, N_GROUPS), jnp.float32),
        "router_expert_w": nrm(ks[15], (L, D, N_EXPERTS), D),
        "router_expert_b": 0.01 * jax.random.normal(ks[16], (L, N_EXPERTS), jnp.float32),
        "w_gate": nrm(ks[17], (L, N_EXPERTS, D, EXPERT_FF), D),
        "w_up": nrm(ks[18], (L, N_EXPERTS, D, EXPERT_FF), D),
        "w_down": nrm(ks[19], (L, N_EXPERTS, EXPERT_FF, D), EXPERT_FF),
        "final_norm_w": gain(ks[20], (D,)),
    }


def reference(x, positions, attn_norm_w, w_in, cmp_pe_k, cmp_pe_v, cmp_k_w1, cmp_k_w2, cmp_v_w1, cmp_v_w2,
              nsa_out_norm_w, sb_out_norm_w, w_out, ffn_norm_w, router_group_w, router_group_b,
              router_expert_w, router_expert_b, w_gate, w_up, w_down, final_norm_w):
    for i in range(DEPTH):
        h = rms_norm(x, attn_norm_w[i])
        x = x + hybrid_mixer(h, positions, w_in[i], cmp_pe_k[i], cmp_pe_v[i], cmp_k_w1[i], cmp_k_w2[i],
                             cmp_v_w1[i], cmp_v_w2[i], nsa_out_norm_w[i], sb_out_norm_w[i], w_out[i])
        h = rms_norm(x, ffn_norm_w[i])
        x = x + hierarchical_moe(h, router_group_w[i], router_group_b[i], router_expert_w[i],
                                 router_expert_b[i], w_gate[i], w_up[i], w_down[i])
    return rms_norm(x, final_norm_w)
```

```python
import functools

import jax
import jax.numpy as jnp
from jax import lax
from jax.experimental import pallas as pl
from jax.experimental.pallas import tpu as pltpu

HEAD_DIM = 64
NSA_HEADS = 8
NSA_KV_HEADS = 2
NSA_GROUP = NSA_HEADS // NSA_KV_HEADS
SB_HEADS = 8
ROPE_THETA = 500000.0
ROPE_DIM = HEAD_DIM // 4
ROPE_HALF = ROPE_DIM // 2
CMP_LEN = 32
CMP_STRIDE = 16
CMP_HIDDEN = 256
SEL_LEN = 64
SEL_TOP = 16
SEL_BONUS = 1.0e4
WINDOW = 512
Q_BLOCK = 128
N_GROUPS = 4
EXPERTS_PER_GROUP = 4
N_EXPERTS = N_GROUPS * EXPERTS_PER_GROUP
EXPERT_FF = 512
EPS = 1e-6
NEG = -1e30
LOWEST = -3.0e38

N_Q = NSA_HEADS * HEAD_DIM
N_KVG = NSA_KV_HEADS * HEAD_DIM
N_GATE = NSA_HEADS * 3
N_GATE_PAD = 32
N_SBH = SB_HEADS * HEAD_DIM
SCALE = HEAD_DIM ** -0.5

PROJ_ROWS = 512
SEL_TILE = 512
WIN_TILES = WINDOW // Q_BLOCK + 1
SB_TILE = 128
MOE_ROWS = 1024
ROUTER_LANES = 128
SB_SKIP_LOG = -104.0

VMEM_LIMIT = 56 * 1024 * 1024

BF16 = jnp.bfloat16
F32 = jnp.float32


def _rms(x, w):
    return x * lax.rsqrt(jnp.mean(x * x, axis=-1, keepdims=True) + EPS) * w


def _dot(a, b):
    return jnp.dot(a, b, preferred_element_type=F32)


def _dot_nt(a, b):
    return lax.dot_general(a, b, (((1,), (1,)), ((), ())), preferred_element_type=F32)


def _split3(x):
    hi = x.astype(BF16)
    r1 = x - hi.astype(F32)
    mid = r1.astype(BF16)
    lo = (r1 - mid.astype(F32)).astype(BF16)
    return hi, mid, lo


def _proj_kernel(x_ref, nw_ref, wn_ref, wt_ref, pos_ref, invf_ref,
                 cmpk_ref, cmpv_ref, sbq_ref, sbk_ref, sbv_ref,
                 qT_ref, ksel_ref, kwin_ref, vselT_ref, vwinT_ref, gT_ref):
    h = _rms(x_ref[...], nw_ref[...]).astype(BF16)
    p1 = _dot(h, wn_ref[...])
    cmpk_ref[...] = p1[:, 0:N_KVG]
    cmpv_ref[...] = p1[:, N_KVG:2 * N_KVG]
    o = 2 * N_KVG
    sbq_ref[...] = (p1[:, o:o + N_SBH] * SCALE).astype(BF16)
    sbk_ref[...] = p1[:, o + N_SBH:o + 2 * N_SBH].astype(BF16)
    sbv_ref[...] = p1[:, o + 2 * N_SBH:o + 3 * N_SBH].astype(BF16)

    p2 = _dot_nt(wt_ref[...], h)
    ang = invf_ref[...] * pos_ref[...].astype(F32)
    cos, sin = jnp.cos(ang), jnp.sin(ang)
    n_rope_heads = NSA_HEADS + 2 * NSA_KV_HEADS
    roped = []
    for hd in range(n_rope_heads):
        blk = p2[hd * HEAD_DIM:(hd + 1) * HEAD_DIM]
        x1, x2 = blk[0:ROPE_HALF], blk[ROPE_HALF:ROPE_DIM]
        roped.append(jnp.concatenate(
            [x1 * cos - x2 * sin, x2 * cos + x1 * sin, blk[ROPE_DIM:]], axis=0))
    qT_ref[...] = (jnp.concatenate(roped[:NSA_HEADS], axis=0) * SCALE).astype(BF16)
    kT = jnp.concatenate(roped[NSA_HEADS:], axis=0)
    kn = kT.T.astype(BF16)
    ksel_ref[...] = kn[:, 0:N_KVG]
    kwin_ref[...] = kn[:, N_KVG:2 * N_KVG]
    o = N_Q + 2 * N_KVG
    vselT_ref[0] = p2[o:o + N_KVG].astype(BF16)
    vw = p2[o + N_KVG:o + 2 * N_KVG].astype(BF16)
    for j in range(PROJ_ROWS // Q_BLOCK):
        vwinT_ref[j] = vw[:, j * Q_BLOCK:(j + 1) * Q_BLOCK]
    o = o + 2 * N_KVG
    gT_ref[...] = jax.nn.sigmoid(p2[o:o + N_GATE_PAD])


def _project(x, positions, attn_norm_w, w_in):
    T, D = x.shape
    R = PROJ_ROWS
    o1, o2, o3 = N_Q, N_Q + 6 * N_KVG, N_Q + 6 * N_KVG + N_GATE
    kv = lambda i: w_in[:, o1 + i * N_KVG:o1 + (i + 1) * N_KVG]
    w_nat = jnp.concatenate([kv(0), kv(1), w_in[:, o3:]], axis=1).astype(BF16)
    w_t = jnp.concatenate(
        [w_in[:, :o1], kv(2), kv(4), kv(3), kv(5), w_in[:, o2:o3],
         jnp.zeros((D, N_GATE_PAD - N_GATE), w_in.dtype)], axis=1).T.astype(BF16)
    inv_freq = ROPE_THETA ** (-jnp.arange(0, ROPE_DIM, 2, dtype=F32) / ROPE_DIM)
    n_nat, n_t = w_nat.shape[1], w_t.shape[0]
    full = lambda shape: pl.BlockSpec(shape, lambda i: (0,) * len(shape))
    rows = lambda n: pl.BlockSpec((R, n), lambda i: (i, 0))
    cols = lambda n: pl.BlockSpec((n, R), lambda i: (0, i))
    return pl.pallas_call(
        _proj_kernel,
        grid=(T // R,),
        in_specs=[rows(D), full((1, D)), full((D, n_nat)), full((n_t, D)), cols(1), full((ROPE_HALF, 1))],
        out_specs=[rows(N_KVG), rows(N_KVG), rows(N_SBH), rows(N_SBH), rows(N_SBH),
                   cols(N_Q), rows(N_KVG), rows(N_KVG),
                   pl.BlockSpec((1, N_KVG, R), lambda i: (i, 0, 0)),
                   pl.BlockSpec((R // Q_BLOCK, N_KVG, Q_BLOCK), lambda i: (i, 0, 0)),
                   cols(N_GATE_PAD)],
        out_shape=[jax.ShapeDtypeStruct((T, N_KVG), F32), jax.ShapeDtypeStruct((T, N_KVG), F32),
                   jax.ShapeDtypeStruct((T, N_SBH), BF16), jax.ShapeDtypeStruct((T, N_SBH), BF16),
                   jax.ShapeDtypeStruct((T, N_SBH), BF16),
                   jax.ShapeDtypeStruct((N_Q, T), BF16),
                   jax.ShapeDtypeStruct((T, N_KVG), BF16), jax.ShapeDtypeStruct((T, N_KVG), BF16),
                   jax.ShapeDtypeStruct((T // R, N_KVG, R), BF16),
                   jax.ShapeDtypeStruct((T // Q_BLOCK, N_KVG, Q_BLOCK), BF16),
                   jax.ShapeDtypeStruct((N_GATE_PAD, T), F32)],
        compiler_params=pltpu.CompilerParams(dimension_semantics=("arbitrary",),
                                             vmem_limit_bytes=VMEM_LIMIT),
    )(x, attn_norm_w.reshape(1, D), w_nat, w_t, positions.reshape(1, T), inv_freq.reshape(ROPE_HALF, 1))


def _compress_kernel(x_ref, pea_ref, peb_ref, wa_ref, wb_ref, w2_ref, nat_ref, tr_ref):
    x = x_ref[0]
    nc = x.shape[0]
    ha = _dot((x + pea_ref[0]).astype(BF16), wa_ref[0])
    hb = _dot((x + peb_ref[0]).astype(BF16), wb_ref[0])
    hid = ha + pltpu.roll(hb, nc - 1, 0)
    act = (hid * jax.nn.sigmoid(hid)).astype(BF16)
    out = _dot(act, w2_ref[0])
    nat_ref[0] = out.astype(BF16)
    tr_ref[0] = out.T.astype(BF16)


def _compress(cmpk, cmpv, pe_k, pe_v, k_w1, k_w2, v_w1, v_w2):
    T = cmpk.shape[0]
    nc = T // CMP_STRIDE
    half = CMP_LEN // 2
    G = NSA_KV_HEADS
    width = half * N_KVG
    x = jnp.stack([cmpk.reshape(nc, width), cmpv.reshape(nc, width)])
    eye = jnp.eye(G, dtype=F32)

    def pe_rows(pe):
        return jnp.broadcast_to(pe[:, None, :], (half, G, HEAD_DIM)).reshape(1, width)

    def w1_block(w1):
        w = w1.reshape(half, HEAD_DIM, CMP_HIDDEN)
        return jnp.einsum('ldj,gh->lgdhj', w, eye).reshape(width, G * CMP_HIDDEN).astype(BF16)

    def w2_block(w2):
        return jnp.einsum('jd,gh->gjhd', w2, eye).reshape(G * CMP_HIDDEN, N_KVG).astype(BF16)

    hw = half * HEAD_DIM
    pea = jnp.stack([pe_rows(pe_k[:half]), pe_rows(pe_v[:half])])
    peb = jnp.stack([pe_rows(pe_k[half:]), pe_rows(pe_v[half:])])
    wa = jnp.stack([w1_block(k_w1[:hw]), w1_block(v_w1[:hw])])
    wb = jnp.stack([w1_block(k_w1[hw:]), w1_block(v_w1[hw:])])
    w2 = jnp.stack([w2_block(k_w2), w2_block(v_w2)])
    blk = lambda a, b: pl.BlockSpec((1, a, b), lambda i: (i, 0, 0))
    nat, tr = pl.pallas_call(
        _compress_kernel,
        grid=(2,),
        in_specs=[blk(nc, width), blk(1, width), blk(1, width), blk(width, G * CMP_HIDDEN),
                  blk(width, G * CMP_HIDDEN), blk(G * CMP_HIDDEN, N_KVG)],
        out_specs=[blk(nc, N_KVG), blk(N_KVG, nc)],
        out_shape=[jax.ShapeDtypeStruct((2, nc, N_KVG), BF16), jax.ShapeDtypeStruct((2, N_KVG, nc), BF16)],
        compiler_params=pltpu.CompilerParams(dimension_semantics=("arbitrary",),
                                             vmem_limit_bytes=VMEM_LIMIT),
    )(x, pea, peb, wa, wb, w2)
    return nat[0], tr[1]


def _nsa_kernel(q_ref, kc_ref, vcT_ref, ovT_ref, ksel_ref, vselT_ref, *rest):
    kwin_refs = rest[0:WIN_TILES]
    vwin_refs = rest[WIN_TILES:2 * WIN_TILES]
    gT_ref, o_ref, bias_scr, m_scr, l_scr, acc_scr = rest[2 * WIN_TILES:]
    i = pl.program_id(0)
    g = pl.program_id(1)
    R, Q = NSA_GROUP, Q_BLOCK
    L = R * Q
    nsel = bias_scr.shape[0]

    qg = jnp.concatenate(
        [q_ref[pl.ds(pl.multiple_of(g * (R * HEAD_DIM) + r * HEAD_DIM, HEAD_DIM), HEAD_DIM), :] for r in range(R)],
        axis=1)
    row_grp = lax.broadcasted_iota(jnp.int32, (N_KVG, L), 0) >> 6
    qz = jnp.where(row_grp == g, jnp.concatenate([qg] * NSA_KV_HEADS, axis=0), jnp.zeros((), BF16))

    def own_rows(x):
        return jnp.where(g == 0, x[0:HEAD_DIM], x[HEAD_DIM:2 * HEAD_DIM])

    t_lane = i * Q + (lax.broadcasted_iota(jnp.int32, (1, L), 1) & (Q - 1))

    nc = kc_ref.shape[0]
    sc = _dot(kc_ref[...], qz)
    cmp_end = lax.broadcasted_iota(jnp.int32, (nc, 1), 0) * CMP_STRIDE + (CMP_LEN - 1)
    mc = cmp_end <= t_lane
    sc = jnp.where(mc, sc, NEG)
    ec = jnp.where(mc, jnp.exp(sc - jnp.max(sc, axis=0, keepdims=True)), 0.0)
    pc = ec * (1.0 / jnp.maximum(jnp.sum(ec, axis=0, keepdims=True), 1e-30))
    ocT = own_rows(_dot(vcT_ref[...], pc.astype(BF16)))

    psum = pc[:, 0:Q]
    for r in range(1, R):
        psum = psum + pc[:, r * Q:(r + 1) * Q]
    ov = ovT_ref[...]
    imp = sum(_dot(ov, part) for part in _split3(psum))
    m_idx = lax.broadcasted_iota(jnp.int32, (nsel, Q), 0).astype(F32)
    bq = ((i * Q + lax.broadcasted_iota(jnp.int32, (1, Q), 1)) >> 6).astype(F32)
    allowed = m_idx <= bq
    forced = (m_idx == 0.0) | (m_idx == bq) | (m_idx == bq - 1.0)
    score = jnp.where(allowed, imp + jnp.where(forced, SEL_BONUS, 0.0), NEG)
    picked = jnp.zeros((nsel, Q), F32)
    for _ in range(min(SEL_TOP, nsel)):
        best = jnp.max(score, axis=0, keepdims=True)
        first = jnp.min(jnp.where(score == best, m_idx, float(nsel)), axis=0, keepdims=True)
        hit = m_idx == first
        picked = jnp.where(hit, 1.0, picked)
        score = jnp.where(hit, LOWEST, score)
    bias = jnp.where(allowed, (picked - 1.0) * (-NEG), NEG)
    bias_scr[...] = jnp.concatenate([bias] * R, axis=1)

    m_scr[...] = jnp.full(m_scr.shape, NEG, F32)
    l_scr[...] = jnp.zeros(l_scr.shape, F32)
    acc_scr[...] = jnp.zeros(acc_scr.shape, F32)
    blocks_per_tile = SEL_TILE // SEL_LEN

    def sel_tile(kt, causal):
        s = _dot(ksel_ref[kt], qz)
        parts = []
        for b in range(blocks_per_tile):
            brow = bias_scr[pl.ds(kt * blocks_per_tile + b, 1), :]
            parts.append(s[b * SEL_LEN:(b + 1) * SEL_LEN] + brow)
        s = jnp.concatenate(parts, axis=0)
        if causal:
            kpos = kt * SEL_TILE + lax.broadcasted_iota(jnp.int32, (SEL_TILE, 1), 0)
            s = jnp.where(kpos <= t_lane, s, NEG)
        m_old = m_scr[...]
        m_new = jnp.maximum(m_old, jnp.max(s, axis=0, keepdims=True))
        alpha = jnp.exp(m_old - m_new)
        p = jnp.exp(s - m_new)
        l_scr[...] = alpha * l_scr[...] + jnp.sum(p, axis=0, keepdims=True)
        acc_scr[...] = alpha * acc_scr[...] + _dot(vselT_ref[kt], p.astype(BF16))
        m_scr[...] = m_new

    n_full = (i * Q) // SEL_TILE

    def full_tile(kt, carry):
        sel_tile(kt, False)
        return carry

    lax.fori_loop(0, n_full, full_tile, 0)
    sel_tile(n_full, True)
    osT = own_rows(acc_scr[...]) * (1.0 / l_scr[...])

    sw, masks = [], []
    for w in range(WIN_TILES):
        j = i - (WIN_TILES - 1) + w
        s = _dot(kwin_refs[w][0], qz)
        kpos = j * Q + lax.broadcasted_iota(jnp.int32, (Q, 1), 0)
        mw = (kpos <= t_lane) & (t_lane - kpos < WINDOW) & (kpos >= 0)
        sw.append(jnp.where(mw, s, NEG))
        masks.append(mw)
    mxw = functools.reduce(jnp.maximum, [jnp.max(s, axis=0, keepdims=True) for s in sw])
    ew = [jnp.where(mw, jnp.exp(s - mxw), 0.0) for s, mw in zip(sw, masks)]
    denw = jnp.maximum(sum(jnp.sum(e, axis=0, keepdims=True) for e in ew), 1e-30)
    owT = sum(_dot(vwin_refs[w][0], ew[w].astype(BF16)) for w in range(WIN_TILES))
    owT = own_rows(owT) * (1.0 / denw)

    def gate_row(j):
        return jnp.concatenate(
            [gT_ref[pl.ds((g * R + r) * 3 + j, 1), :] for r in range(R)], axis=1)

    oT = gate_row(0) * ocT + gate_row(1) * osT + gate_row(2) * owT
    o_hd = jnp.concatenate([oT[:, r * Q:(r + 1) * Q] for r in range(R)], axis=0)
    o_ref[...] = o_hd.T


def _nsa(qT, kc, vcT, ksel, vselT, kwin, vwinT, gT):
    T = qT.shape[1]
    nb = T // Q_BLOCK
    nc = kc.shape[0]
    nsel = T // SEL_LEN
    ntile = T // SEL_TILE
    n = jnp.arange(nc)[None, :] * CMP_STRIDE
    m = jnp.arange(nsel)[:, None] * SEL_LEN
    ovT = ((n < m + SEL_LEN) & (n + CMP_LEN > m)).astype(BF16)
    ksel3 = ksel.reshape(ntile, SEL_TILE, N_KVG)
    kwin3 = kwin.reshape(nb, Q_BLOCK, N_KVG)
    L = NSA_GROUP * Q_BLOCK
    const = lambda shape: pl.BlockSpec(shape, lambda i, g: (0,) * len(shape))
    kwin_specs = [pl.BlockSpec((1, Q_BLOCK, N_KVG),
                               functools.partial(lambda i, g, w: (jnp.maximum(i - (WIN_TILES - 1) + w, 0), 0, 0), w=w))
                  for w in range(WIN_TILES)]
    vwin_specs = [pl.BlockSpec((1, N_KVG, Q_BLOCK),
                               functools.partial(lambda i, g, w: (jnp.maximum(i - (WIN_TILES - 1) + w, 0), 0, 0), w=w))
                  for w in range(WIN_TILES)]
    return pl.pallas_call(
        _nsa_kernel,
        grid=(nb, NSA_KV_HEADS),
        in_specs=[pl.BlockSpec((N_Q, Q_BLOCK), lambda i, g: (0, i)),
                  const((nc, N_KVG)), const((N_KVG, nc)), const((nsel, nc)),
                  const((ntile, SEL_TILE, N_KVG)), const((ntile, N_KVG, SEL_TILE))]
                 + kwin_specs + vwin_specs
                 + [pl.BlockSpec((N_GATE_PAD, Q_BLOCK), lambda i, g: (0, i))],
        out_specs=pl.BlockSpec((Q_BLOCK, NSA_GROUP * HEAD_DIM), lambda i, g: (i, g)),
        out_shape=jax.ShapeDtypeStruct((T, N_Q), F32),
        scratch_shapes=[pltpu.VMEM((nsel, L), F32), pltpu.VMEM((1, L), F32), pltpu.VMEM((1, L), F32),
                        pltpu.VMEM((N_KVG, L), F32)],
        compiler_params=pltpu.CompilerParams(dimension_semantics=("arbitrary", "arbitrary"),
                                             vmem_limit_bytes=VMEM_LIMIT),
    )(qT, kc, vcT, ovT, ksel3, vselT, *([kwin3] * WIN_TILES), *([vwinT] * WIN_TILES), gT)


def _sb_kernel(q_ref, k_ref, v_ref, o_ref, acc_scr, c_scr):
    i = pl.program_id(1)
    Q = Q_BLOCK
    lane = lax.broadcasted_iota(jnp.int32, (Q, 2 * HEAD_DIM), 1)
    q2 = q_ref[...]
    zero = jnp.zeros((), BF16)
    qh = [jnp.where(lane < HEAD_DIM, q2, zero), jnp.where(lane >= HEAD_DIM, q2, zero)]
    row = lax.broadcasted_iota(jnp.int32, (Q, SB_TILE), 0)
    col = lax.broadcasted_iota(jnp.int32, (Q, SB_TILE), 1)
    later = (row > col).astype(BF16)
    acc_scr[...] = jnp.zeros(acc_scr.shape, F32)
    c_scr[...] = jnp.zeros(c_scr.shape, F32)

    def tile(kt, diagonal):
        k_t = k_ref[pl.ds(pl.multiple_of(kt * SB_TILE, SB_TILE), SB_TILE), :]
        v_t = v_ref[pl.ds(pl.multiple_of(kt * SB_TILE, SB_TILE), SB_TILE), :]
        worst = None
        for hd in range(2):
            z = _dot_nt(qh[hd], k_t)
            soft = jnp.log1p(jnp.exp(-jnp.abs(z)))
            log_not = -jnp.maximum(z, 0.0) - soft
            log_sig = jnp.minimum(z, 0.0) - soft
            if diagonal:
                log_not = jnp.where(col < row, log_not, 0.0)
            hi = log_not.astype(BF16)
            lo = (log_not - hi.astype(F32)).astype(BF16)
            c_old = c_scr[hd]
            after = _dot(hi, later) + _dot(lo, later) + c_old
            a = jnp.exp(log_sig + after)
            if diagonal:
                a = jnp.where(col < row, a, 0.0)
            acc_scr[hd] = acc_scr[hd] + _dot(a.astype(BF16), v_t)
            c_new = c_old + jnp.sum(log_not, axis=1, keepdims=True)
            c_scr[hd] = c_new
            top = jnp.max(c_new)
            worst = top if worst is None else jnp.maximum(worst, top)
        return worst

    worst0 = tile(i, True)

    def cond(carry):
        kt, worst = carry
        return (kt >= 0) & (worst >= SB_SKIP_LOG)

    def body(carry):
        kt, _ = carry
        return kt - 1, tile(kt, False)

    lax.while_loop(cond, body, (i - 1, worst0))
    o_ref[...] = jnp.where(lane < HEAD_DIM, acc_scr[0], acc_scr[1])


def _stick_breaking(sbq, sbk, sbv):
    T = sbq.shape[0]
    nb = T // Q_BLOCK
    W = 2 * HEAD_DIM
    return pl.pallas_call(
        _sb_kernel,
        grid=(SB_HEADS // 2, nb),
        in_specs=[pl.BlockSpec((Q_BLOCK, W), lambda p, i: (i, p)),
                  pl.BlockSpec((T, W), lambda p, i: (0, p)),
                  pl.BlockSpec((T, W), lambda p, i: (0, p))],
        out_specs=pl.BlockSpec((Q_BLOCK, W), lambda p, i: (i, p)),
        out_shape=jax.ShapeDtypeStruct((T, N_SBH), F32),
        scratch_shapes=[pltpu.VMEM((2, Q_BLOCK, W), F32), pltpu.VMEM((2, Q_BLOCK, 1), F32)],
        compiler_params=pltpu.CompilerParams(dimension_semantics=("arbitrary", "arbitrary"),
                                             vmem_limit_bytes=VMEM_LIMIT),
    )(sbq, sbk, sbv)


def _mix_kernel(x_ref, on_ref, os_ref, nwn_ref, nws_ref, wo_ref, fw_ref, rhi_ref, rlo_ref, rb_ref,
                x1_ref, h2_ref, lg_ref):
    n1 = _rms(on_ref[...], nwn_ref[...]).astype(BF16)
    n2 = _rms(os_ref[...], nws_ref[...]).astype(BF16)
    x1 = x_ref[...] + _dot(n1, wo_ref[0:N_Q]) + _dot(n2, wo_ref[N_Q:N_Q + N_SBH])
    x1_ref[...] = x1
    h2 = _rms(x1, fw_ref[...])
    hi = h2.astype(BF16)
    lo = (h2 - hi.astype(F32)).astype(BF16)
    h2_ref[...] = hi
    lg_ref[...] = _dot(hi, rhi_ref[...]) + _dot(hi, rlo_ref[...]) + _dot(lo, rhi_ref[...]) + rb_ref[...]


def _mix(x, o_nsa, o_sb, nsa_norm_w, sb_norm_w, w_out, ffn_norm_w, rg_w, rg_b, re_w, re_b):
    T, D = x.shape
    R = PROJ_ROWS
    pad = ROUTER_LANES - N_GROUPS - N_EXPERTS
    wr = jnp.concatenate([rg_w, re_w, jnp.zeros((D, pad), F32)], axis=1)
    wr_hi = wr.astype(BF16)
    wr_lo = (wr - wr_hi.astype(F32)).astype(BF16)
    rb = jnp.concatenate([rg_b, re_b, jnp.zeros((pad,), F32)]).reshape(1, ROUTER_LANES)
    full = lambda shape: pl.BlockSpec(shape, lambda i: (0,) * len(shape))
    rows = lambda n: pl.BlockSpec((R, n), lambda i: (i, 0))
    return pl.pallas_call(
        _mix_kernel,
        grid=(T // R,),
        in_specs=[rows(D), rows(N_Q), rows(N_SBH), full((1, N_Q)), full((1, N_SBH)), full((N_Q + N_SBH, D)),
                  full((1, D)), full((D, ROUTER_LANES)), full((D, ROUTER_LANES)), full((1, ROUTER_LANES))],
        out_specs=[rows(D), rows(D), rows(ROUTER_LANES)],
        out_shape=[jax.ShapeDtypeStruct((T, D), F32), jax.ShapeDtypeStruct((T, D), BF16),
                   jax.ShapeDtypeStruct((T, ROUTER_LANES), F32)],
        compiler_params=pltpu.CompilerParams(dimension_semantics=("arbitrary",),
                                             vmem_limit_bytes=VMEM_LIMIT),
    )(x, o_nsa, o_sb, nsa_norm_w.reshape(1, N_Q), sb_norm_w.reshape(1, N_SBH), w_out.astype(BF16),
      ffn_norm_w.reshape(1, D), wr_hi, wr_lo, rb)


def _routing_weights(lg):
    lane_i = lax.broadcasted_iota(jnp.int32, lg.shape, 1)
    lane = lane_i.astype(F32)
    first_max = lambda v, mx: jnp.min(jnp.where(v == mx, lane, float(ROUTER_LANES)), axis=1, keepdims=True)
    gl = jnp.where(lane_i < N_GROUPS, lg, -jnp.inf)
    gmax = jnp.max(gl, axis=1, keepdims=True)
    grp = first_max(gl, gmax)
    g_gate = 1.0 / jnp.sum(jnp.exp(gl - gmax), axis=1, keepdims=True)
    e_idx = lane_i - N_GROUPS
    e_grp = (e_idx >> 2).astype(F32)
    in_grp = (e_idx >= 0) & (e_idx < N_EXPERTS) & (e_grp == grp)
    el = jnp.where(in_grp, lg, -jnp.inf)
    top1 = jnp.max(el, axis=1, keepdims=True)
    i1 = first_max(el, top1)
    el2 = jnp.where(lane == i1, -jnp.inf, el)
    top2 = jnp.max(el2, axis=1, keepdims=True)
    i2 = first_max(el2, top2)
    e2 = jnp.exp(top2 - top1)
    w1 = 1.0 / (1.0 + e2)
    w2 = e2 / (1.0 + e2)
    return g_gate * (jnp.where(lane == i1, w1, 0.0) + jnp.where(lane == i2, w2, 0.0))


def _moe_kernel(h_ref, lg_ref, x1_ref, wg_ref, wu_ref, wd_ref, fw_ref, o_ref, acc_scr, cw_scr):
    e = pl.program_id(1)

    @pl.when(e == 0)
    def _():
        acc_scr[...] = jnp.zeros(acc_scr.shape, F32)
        cw_scr[...] = _routing_weights(lg_ref[...])

    lane = lax.broadcasted_iota(jnp.int32, cw_scr.shape, 1)
    cw = jnp.sum(jnp.where(lane == e + N_GROUPS, cw_scr[...], 0.0), axis=1, keepdims=True)
    h = h_ref[...]
    a = _dot(h, wg_ref[0])
    b = _dot(h, wu_ref[0])
    act = (a * jax.nn.sigmoid(a) * b).astype(BF16)
    y = _dot(act, wd_ref[0])
    acc_scr[...] += jnp.where(cw != 0.0, cw * y, 0.0)

    @pl.when(e == N_EXPERTS - 1)
    def _():
        o_ref[...] = _rms(x1_ref[...] + acc_scr[...], fw_ref[...])


def _moe(h2, logits, x1, w_gate, w_up, w_down, final_norm_w):
    T, D = x1.shape
    R = min(MOE_ROWS, T)
    rows = lambda n: pl.BlockSpec((R, n), lambda i, e: (i, 0))
    return pl.pallas_call(
        _moe_kernel,
        grid=(T // R, N_EXPERTS),
        in_specs=[rows(D), rows(ROUTER_LANES), rows(D),
                  pl.BlockSpec((1, D, EXPERT_FF), lambda i, e: (e, 0, 0)),
                  pl.BlockSpec((1, D, EXPERT_FF), lambda i, e: (e, 0, 0)),
                  pl.BlockSpec((1, EXPERT_FF, D), lambda i, e: (e, 0, 0)),
                  pl.BlockSpec((1, D), lambda i, e: (0, 0))],
        out_specs=rows(D),
        out_shape=jax.ShapeDtypeStruct((T, D), F32),
        scratch_shapes=[pltpu.VMEM((R, D), F32), pltpu.VMEM((R, ROUTER_LANES), F32)],
        compiler_params=pltpu.CompilerParams(dimension_semantics=("arbitrary", "arbitrary"),
                                             vmem_limit_bytes=VMEM_LIMIT),
    )(h2, logits, x1, w_gate.astype(BF16), w_up.astype(BF16), w_down.astype(BF16),
      final_norm_w.reshape(1, D))


def kernel(x, positions, attn_norm_w, w_in, cmp_pe_k, cmp_pe_v, cmp_k_w1, cmp_k_w2, cmp_v_w1, cmp_v_w2,
           nsa_out_norm_w, sb_out_norm_w, w_out, ffn_norm_w, router_group_w, router_group_b,
           router_expert_w, router_expert_b, w_gate, w_up, w_down, final_norm_w):
    B, T, D = x.shape
    assert B == 1 and T % SEL_TILE == 0 and T % PROJ_ROWS == 0 and T // SEL_LEN >= SEL_TOP
    assert attn_norm_w.shape[0] == 1, "the final norm is fused into the (single) layer's MoE kernel"
    xs = x.reshape(T, D)
    pos = positions.reshape(T)
    (cmpk, cmpv, sbq, sbk, sbv, qT, ksel, kwin, vselT, vwinT, gT) = _project(xs, pos, attn_norm_w[0], w_in[0])
    kc, vcT = _compress(cmpk, cmpv, cmp_pe_k[0], cmp_pe_v[0], cmp_k_w1[0], cmp_k_w2[0], cmp_v_w1[0], cmp_v_w2[0])
    o_nsa = _nsa(qT, kc, vcT, ksel, vselT, kwin, vwinT, gT)
    o_sb = _stick_breaking(sbq, sbk, sbv)
    x1, h2, logits = _mix(xs, o_nsa, o_sb, nsa_out_norm_w[0], sb_out_norm_w[0], w_out[0], ffn_norm_w[0],
                          router_group_w[0], router_group_b[0], router_expert_w[0], router_expert_b[0])
    out = _moe(h2, logits, x1, w_gate[0], w_up[0], w_down[0], final_norm_w)
    return out.reshape(B, T, D)
```

```python
import functools

import jax
import jax.numpy as jnp
from jax import lax
from jax.experimental import pallas as pl
from jax.experimental.pallas import tpu as pltpu

HEAD_DIM = 64
NSA_HEADS = 8
NSA_KV_HEADS = 2
NSA_GROUP = NSA_HEADS // NSA_KV_HEADS
SB_HEADS = 8
ROPE_THETA = 500000.0
ROPE_DIM = HEAD_DIM // 4
ROPE_HALF = ROPE_DIM // 2
CMP_LEN = 32
CMP_STRIDE = 16
CMP_HIDDEN = 256
SEL_LEN = 64
SEL_TOP = 16
SEL_BONUS = 1.0e4
WINDOW = 512
Q_BLOCK = 128
N_GROUPS = 4
EXPERTS_PER_GROUP = 4
N_EXPERTS = N_GROUPS * EXPERTS_PER_GROUP
EXPERT_FF = 512
EPS = 1e-6
NEG = -1e30
LOWEST = -3.0e38

N_Q = NSA_HEADS * HEAD_DIM
N_KVG = NSA_KV_HEADS * HEAD_DIM
N_GATE = NSA_HEADS * 3
N_GATE_PAD = 32
N_SBH = SB_HEADS * HEAD_DIM
SCALE = HEAD_DIM ** -0.5
LOG2E = 1.4426950408889634
SEL_BLOCKS_PER_TILE = 8
BIAS_ROWS = 16

PROJ_ROWS = 512
SEL_TILE = 512
WIN_TILES = WINDOW // Q_BLOCK + 1
SB_TILE = 128
MOE_ROWS = 1024
MOE_CAP = 256
ROUTER_LANES = 128
ROUTER_ROWS = 32
SB_SKIP_LOG = -104.0

VMEM_LIMIT = 56 * 1024 * 1024

BF16 = jnp.bfloat16
F32 = jnp.float32


def _rms(x, w):
    return x * lax.rsqrt(jnp.mean(x * x, axis=-1, keepdims=True) + EPS) * w


def _dot(a, b):
    return jnp.dot(a, b, preferred_element_type=F32)


def _dot_nt(a, b):
    return lax.dot_general(a, b, (((1,), (1,)), ((), ())), preferred_element_type=F32)


def _split3(x):
    hi = x.astype(BF16)
    r1 = x - hi.astype(F32)
    mid = r1.astype(BF16)
    lo = (r1 - mid.astype(F32)).astype(BF16)
    return hi, mid, lo


def _proj_kernel(x_ref, nw_ref, wn_ref, wt_ref, pos_ref, invf_ref,
                 cmpk_ref, cmpv_ref, sbq_ref, sbk_ref, sbv_ref,
                 qT_ref, ksel_ref, kwin_ref, vselT_ref, vwinT_ref, gT_ref):
    h = _rms(x_ref[...], nw_ref[...]).astype(BF16)
    p1 = _dot(h, wn_ref[...])
    cmpk_ref[...] = p1[:, 0:N_KVG]
    cmpv_ref[...] = p1[:, N_KVG:2 * N_KVG]
    o = 2 * N_KVG
    sbq_ref[...] = (p1[:, o:o + N_SBH] * SCALE).astype(BF16)
    sbk_ref[...] = p1[:, o + N_SBH:o + 2 * N_SBH].astype(BF16)
    sbv_ref[...] = p1[:, o + 2 * N_SBH:o + 3 * N_SBH].astype(BF16)

    p2 = _dot_nt(wt_ref[...], h)
    ang = invf_ref[...] * pos_ref[...].astype(F32)
    cos, sin = jnp.cos(ang), jnp.sin(ang)
    n_rope_heads = NSA_HEADS + 2 * NSA_KV_HEADS
    roped = []
    for hd in range(n_rope_heads):
        blk = p2[hd * HEAD_DIM:(hd + 1) * HEAD_DIM]
        x1, x2 = blk[0:ROPE_HALF], blk[ROPE_HALF:ROPE_DIM]
        roped.append(jnp.concatenate(
            [x1 * cos - x2 * sin, x2 * cos + x1 * sin, blk[ROPE_DIM:]], axis=0))
    qT_ref[...] = (jnp.concatenate(roped[:NSA_HEADS], axis=0) * (SCALE * LOG2E)).astype(BF16)
    kT = jnp.concatenate(roped[NSA_HEADS:], axis=0)
    kn = kT.T.astype(BF16)
    r_blk = lax.broadcasted_iota(jnp.int32, (PROJ_ROWS, HEAD_DIM), 0) >> 6
    c_idx = lax.broadcasted_iota(jnp.int32, (PROJ_ROWS, HEAD_DIM), 1)
    onehot = jnp.where(r_blk == c_idx, 1.0, 0.0).astype(BF16)
    for gk in range(NSA_KV_HEADS):
        ksel_ref[gk] = jnp.concatenate([kn[:, gk * HEAD_DIM:(gk + 1) * HEAD_DIM], onehot], axis=1)
    kwin_ref[...] = kn[:, N_KVG:2 * N_KVG]
    o = N_Q + 2 * N_KVG
    for gk in range(NSA_KV_HEADS):
        vselT_ref[0, gk] = p2[o + gk * HEAD_DIM:o + (gk + 1) * HEAD_DIM].astype(BF16)
    vw = p2[o + N_KVG:o + 2 * N_KVG].astype(BF16)
    for j in range(PROJ_ROWS // Q_BLOCK):
        vwinT_ref[j] = vw[:, j * Q_BLOCK:(j + 1) * Q_BLOCK]
    o = o + 2 * N_KVG
    gT_ref[...] = jax.nn.sigmoid(p2[o:o + N_GATE_PAD])


def _project(x, positions, attn_norm_w, w_in):
    T, D = x.shape
    R = PROJ_ROWS
    o1, o2, o3 = N_Q, N_Q + 6 * N_KVG, N_Q + 6 * N_KVG + N_GATE
    kv = lambda i: w_in[:, o1 + i * N_KVG:o1 + (i + 1) * N_KVG]
    w_nat = jnp.concatenate([kv(0), kv(1), w_in[:, o3:]], axis=1).astype(BF16)
    w_t = jnp.concatenate(
        [w_in[:, :o1], kv(2), kv(4), kv(3), kv(5), w_in[:, o2:o3],
         jnp.zeros((D, N_GATE_PAD - N_GATE), w_in.dtype)], axis=1).T.astype(BF16)
    inv_freq = ROPE_THETA ** (-jnp.arange(0, ROPE_DIM, 2, dtype=F32) / ROPE_DIM)
    n_nat, n_t = w_nat.shape[1], w_t.shape[0]
    full = lambda shape: pl.BlockSpec(shape, lambda i: (0,) * len(shape))
    rows = lambda n: pl.BlockSpec((R, n), lambda i: (i, 0))
    cols = lambda n: pl.BlockSpec((n, R), lambda i: (0, i))
    return pl.pallas_call(
        _proj_kernel,
        grid=(T // R,),
        in_specs=[rows(D), full((1, D)), full((D, n_nat)), full((n_t, D)), cols(1), full((ROPE_HALF, 1))],
        out_specs=[rows(N_KVG), rows(N_KVG), rows(N_SBH), rows(N_SBH), rows(N_SBH),
                   cols(N_Q), pl.BlockSpec((NSA_KV_HEADS, R, N_KVG), lambda i: (0, i, 0)), rows(N_KVG),
                   pl.BlockSpec((1, NSA_KV_HEADS, HEAD_DIM, R), lambda i: (i, 0, 0, 0)),
                   pl.BlockSpec((R // Q_BLOCK, N_KVG, Q_BLOCK), lambda i: (i, 0, 0)),
                   cols(N_GATE_PAD)],
        out_shape=[jax.ShapeDtypeStruct((T, N_KVG), F32), jax.ShapeDtypeStruct((T, N_KVG), F32),
                   jax.ShapeDtypeStruct((T, N_SBH), BF16), jax.ShapeDtypeStruct((T, N_SBH), BF16),
                   jax.ShapeDtypeStruct((T, N_SBH), BF16),
                   jax.ShapeDtypeStruct((N_Q, T), BF16),
                   jax.ShapeDtypeStruct((NSA_KV_HEADS, T, N_KVG), BF16), jax.ShapeDtypeStruct((T, N_KVG), BF16),
                   jax.ShapeDtypeStruct((T // R, NSA_KV_HEADS, HEAD_DIM, R), BF16),
                   jax.ShapeDtypeStruct((T // Q_BLOCK, N_KVG, Q_BLOCK), BF16),
                   jax.ShapeDtypeStruct((N_GATE_PAD, T), F32)],
        compiler_params=pltpu.CompilerParams(dimension_semantics=("arbitrary",),
                                             vmem_limit_bytes=VMEM_LIMIT),
    )(x, attn_norm_w.reshape(1, D), w_nat, w_t, positions.reshape(1, T), inv_freq.reshape(ROPE_HALF, 1))


def _compress_kernel(x_ref, pea_ref, peb_ref, wa_ref, wb_ref, w2_ref, nat_ref, tr_ref):
    x = x_ref[0]
    nc = x.shape[0]
    ha = _dot((x + pea_ref[0]).astype(BF16), wa_ref[0])
    hb = _dot((x + peb_ref[0]).astype(BF16), wb_ref[0])
    hid = ha + pltpu.roll(hb, nc - 1, 0)
    act = (hid * jax.nn.sigmoid(hid)).astype(BF16)
    out = _dot(act, w2_ref[0])
    nat_ref[0] = out.astype(BF16)
    tr_ref[0] = out.T.astype(BF16)


def _compress(cmpk, cmpv, pe_k, pe_v, k_w1, k_w2, v_w1, v_w2):
    T = cmpk.shape[0]
    nc = T // CMP_STRIDE
    half = CMP_LEN // 2
    G = NSA_KV_HEADS
    width = half * N_KVG
    x = jnp.stack([cmpk.reshape(nc, width), cmpv.reshape(nc, width)])
    eye = jnp.eye(G, dtype=F32)

    def pe_rows(pe):
        return jnp.broadcast_to(pe[:, None, :], (half, G, HEAD_DIM)).reshape(1, width)

    def w1_block(w1):
        w = w1.reshape(half, HEAD_DIM, CMP_HIDDEN)
        return jnp.einsum('ldj,gh->lgdhj', w, eye).reshape(width, G * CMP_HIDDEN).astype(BF16)

    def w2_block(w2):
        return jnp.einsum('jd,gh->gjhd', w2, eye).reshape(G * CMP_HIDDEN, N_KVG).astype(BF16)

    hw = half * HEAD_DIM
    pea = jnp.stack([pe_rows(pe_k[:half]), pe_rows(pe_v[:half])])
    peb = jnp.stack([pe_rows(pe_k[half:]), pe_rows(pe_v[half:])])
    wa = jnp.stack([w1_block(k_w1[:hw]), w1_block(v_w1[:hw])])
    wb = jnp.stack([w1_block(k_w1[hw:]), w1_block(v_w1[hw:])])
    w2 = jnp.stack([w2_block(k_w2), w2_block(v_w2)])
    blk = lambda a, b: pl.BlockSpec((1, a, b), lambda i: (i, 0, 0))
    nat, tr = pl.pallas_call(
        _compress_kernel,
        grid=(2,),
        in_specs=[blk(nc, width), blk(1, width), blk(1, width), blk(width, G * CMP_HIDDEN),
                  blk(width, G * CMP_HIDDEN), blk(G * CMP_HIDDEN, N_KVG)],
        out_specs=[blk(nc, N_KVG), blk(N_KVG, nc)],
        out_shape=[jax.ShapeDtypeStruct((2, nc, N_KVG), BF16), jax.ShapeDtypeStruct((2, N_KVG, nc), BF16)],
        compiler_params=pltpu.CompilerParams(dimension_semantics=("arbitrary",),
                                             vmem_limit_bytes=VMEM_LIMIT),
    )(x, pea, peb, wa, wb, w2)
    return nat[0], tr[1]


def _nsa_kernel(q_ref, kc_ref, vcT_ref, ovT_ref, ksel_ref, vselT_ref, *rest):
    kwin_refs = rest[0:WIN_TILES]
    vwin_refs = rest[WIN_TILES:2 * WIN_TILES]
    gT_ref, o_ref, bias_scr, s0_scr, s1_scr, m_scr, l_scr, acc_scr, oc_scr = rest[2 * WIN_TILES:]
    i = pl.program_id(0)
    g = pl.program_id(1)
    R, Q = NSA_GROUP, Q_BLOCK
    L = R * Q
    nsel = bias_scr.shape[0]

    qg = jnp.concatenate(
        [q_ref[pl.ds(pl.multiple_of(g * (R * HEAD_DIM) + r * HEAD_DIM, HEAD_DIM), HEAD_DIM), :] for r in range(R)],
        axis=1)
    row_grp = lax.broadcasted_iota(jnp.int32, (N_KVG, L), 0) >> 6
    qz = jnp.where(row_grp == g, jnp.concatenate([qg] * NSA_KV_HEADS, axis=0), jnp.zeros((), BF16))

    def own_rows(x):
        return jnp.where(g == 0, x[0:HEAD_DIM], x[HEAD_DIM:2 * HEAD_DIM])

    t_lane = i * Q + (lax.broadcasted_iota(jnp.int32, (1, L), 1) & (Q - 1))

    nc = kc_ref.shape[0]
    q_pos = i * Q + lax.broadcasted_iota(jnp.int32, (1, Q), 1)
    bq = (q_pos >> 6).astype(F32)

    def compress_and_select(n_eff, first_class):
        m_eff = n_eff // (SEL_LEN // CMP_STRIDE)
        sc = _dot(kc_ref[0:n_eff, :], qz)
        cmp_end = lax.broadcasted_iota(jnp.int32, (n_eff, 1), 0) * CMP_STRIDE + (CMP_LEN - 1)
        sc = jnp.where(cmp_end <= t_lane, sc, NEG)
        mxc = jnp.max(sc, axis=0, keepdims=True)
        mxc = jnp.where(mxc < 0.5 * NEG, 0.0, mxc)
        ec = jnp.exp2(sc - mxc)
        pc = ec * (1.0 / jnp.maximum(jnp.sum(ec, axis=0, keepdims=True), 1e-30))
        oc_scr[...] = own_rows(_dot(vcT_ref[:, 0:n_eff], pc.astype(BF16)))
        psum = pc[:, 0:Q]
        for r in range(1, R):
            psum = psum + pc[:, r * Q:(r + 1) * Q]
        ov = ovT_ref[0:m_eff, 0:n_eff]
        imp = sum(_dot(ov, part) for part in _split3(psum))
        m_idx = lax.broadcasted_iota(jnp.int32, (m_eff, Q), 0).astype(F32)
        allowed = m_idx <= bq
        forced = (m_idx == 0.0) | (m_idx == bq) | (m_idx == bq - 1.0)
        if first_class:
            score = jnp.where(allowed, imp + jnp.where(forced, SEL_BONUS, 0.0), NEG)
            picked = jnp.zeros((m_eff, Q), F32)
            n_pick = min(SEL_TOP, m_eff)
        else:
            score = jnp.where(allowed & jnp.logical_not(forced), imp, LOWEST)
            picked = jnp.where(forced, 1.0, 0.0)
            n_pick = SEL_TOP - 3
        for _ in range(n_pick):
            best = jnp.max(score, axis=0, keepdims=True)
            first = jnp.min(jnp.where(score == best, m_idx, float(m_eff)), axis=0, keepdims=True)
            hit = m_idx == first
            picked = jnp.where(hit, 1.0, picked)
            score = jnp.where(hit, LOWEST, score)
        bias = jnp.where(allowed, (picked - 1.0) * (-NEG), NEG)
        bias_scr[0:m_eff, :] = jnp.concatenate([bias] * R, axis=1)
        if m_eff < nsel:
            bias_scr[m_eff:nsel, :] = jnp.full((nsel - m_eff, L), NEG, F32)

    sizes = sorted({max(Q, nc >> shift) for shift in range(4)})
    lo = 0
    for n_eff in sizes:
        hi = n_eff // (Q // CMP_STRIDE)
        pl.when((i >= lo) & (i < hi))(functools.partial(compress_and_select, n_eff, lo == 0))
        lo = hi
    ocT = oc_scr[...]

    m_scr[...] = jnp.full(m_scr.shape, NEG, F32)
    l_scr[...] = jnp.zeros(l_scr.shape, F32)
    acc_scr[...] = jnp.zeros(acc_scr.shape, F32)
    rhs_pad = jnp.zeros((N_KVG - HEAD_DIM - BIAS_ROWS, L), BF16)
    bias_pad = jnp.zeros((BIAS_ROWS - SEL_BLOCKS_PER_TILE, L), F32)

    def scores(kt, dst):
        brows = bias_scr[pl.ds(pl.multiple_of(kt * SEL_BLOCKS_PER_TILE, SEL_BLOCKS_PER_TILE), SEL_BLOCKS_PER_TILE), :]
        rhs = jnp.concatenate([qg, jnp.concatenate([brows, bias_pad], axis=0).astype(BF16), rhs_pad], axis=0)
        s = _dot(ksel_ref[g, kt], rhs)
        dst[...] = s
        return jnp.max(s, axis=0, keepdims=True)

    def absorb(kt, src, m_tile):
        m_old = m_scr[...]
        m_new = jnp.maximum(m_old, m_tile)
        alpha = jnp.exp2(m_old - m_new)
        p = jnp.exp2(src[...] - m_new)
        l_scr[...] = alpha * l_scr[...] + jnp.sum(p, axis=0, keepdims=True)
        acc_scr[...] = alpha * acc_scr[...] + _dot(vselT_ref[kt, g], p.astype(BF16))
        m_scr[...] = m_new

    def step(kt, src, dst, m_tile):
        m_next = scores(kt + 1, dst)
        absorb(kt, src, m_tile)
        return m_next

    k_row = lax.broadcasted_iota(jnp.int32, (Q, 1), 0)
    q_lane = lax.broadcasted_iota(jnp.int32, (1, L), 1) & (Q - 1)
    n_full = (i * Q) // SEL_TILE

    def last_tile(src):
        r0 = pl.multiple_of(i * Q - n_full * SEL_TILE, Q)
        src[pl.ds(r0, Q), :] = jnp.where(k_row <= q_lane, src[pl.ds(r0, Q), :], NEG)
        absorb(n_full, src, jnp.max(src[...], axis=0, keepdims=True))

    def two_steps(j, m_tile):
        return step(2 * j + 1, s1_scr, s0_scr, step(2 * j, s0_scr, s1_scr, m_tile))

    m_pending = lax.fori_loop(0, n_full // 2, two_steps, scores(0, s0_scr))

    @pl.when((n_full & 1) == 0)
    def _():
        last_tile(s0_scr)

    @pl.when((n_full & 1) == 1)
    def _():
        step(n_full - 1, s0_scr, s1_scr, m_pending)
        last_tile(s1_scr)

    osT = acc_scr[...] * (1.0 / l_scr[...])

    sw = []
    for w in range(WIN_TILES):
        j = i - (WIN_TILES - 1) + w
        s = _dot(kwin_refs[w][0], qz)
        if w == 0:
            s = jnp.where(k_row > q_lane, s, NEG)
        if w == WIN_TILES - 1:
            s = jnp.where(k_row <= q_lane, s, NEG)
        else:
            s = jnp.where(j >= 0, s, NEG)
        sw.append(s)
    mxw = functools.reduce(jnp.maximum, [jnp.max(s, axis=0, keepdims=True) for s in sw])
    ew = [jnp.exp2(s - mxw) for s in sw]
    denw = jnp.maximum(sum(jnp.sum(e, axis=0, keepdims=True) for e in ew), 1e-30)
    owT = sum(_dot(vwin_refs[w][0], ew[w].astype(BF16)) for w in range(WIN_TILES))
    owT = own_rows(owT) * (1.0 / denw)

    def gate_row(j):
        return jnp.concatenate(
            [gT_ref[pl.ds((g * R + r) * 3 + j, 1), :] for r in range(R)], axis=1)

    oT = gate_row(0) * ocT + gate_row(1) * osT + gate_row(2) * owT
    o_hd = jnp.concatenate([oT[:, r * Q:(r + 1) * Q] for r in range(R)], axis=0)
    o_ref[...] = o_hd.T


def _nsa(qT, kc, vcT, ksel, vselT, kwin, vwinT, gT):
    T = qT.shape[1]
    nb = T // Q_BLOCK
    nc = kc.shape[0]
    nsel = T // SEL_LEN
    ntile = T // SEL_TILE
    n = jnp.arange(nc)[None, :] * CMP_STRIDE
    m = jnp.arange(nsel)[:, None] * SEL_LEN
    ovT = ((n < m + SEL_LEN) & (n + CMP_LEN > m)).astype(BF16)
    ksel4 = ksel.reshape(NSA_KV_HEADS, ntile, SEL_TILE, N_KVG)
    kwin3 = kwin.reshape(nb, Q_BLOCK, N_KVG)
    L = NSA_GROUP * Q_BLOCK
    const = lambda shape: pl.BlockSpec(shape, lambda i, g: (0,) * len(shape))
    kwin_specs = [pl.BlockSpec((1, Q_BLOCK, N_KVG),
                               functools.partial(lambda i, g, w: (jnp.maximum(i - (WIN_TILES - 1) + w, 0), 0, 0), w=w))
                  for w in range(WIN_TILES)]
    vwin_specs = [pl.BlockSpec((1, N_KVG, Q_BLOCK),
                               functools.partial(lambda i, g, w: (jnp.maximum(i - (WIN_TILES - 1) + w, 0), 0, 0), w=w))
                  for w in range(WIN_TILES)]
    return pl.pallas_call(
        _nsa_kernel,
        grid=(nb, NSA_KV_HEADS),
        in_specs=[pl.BlockSpec((N_Q, Q_BLOCK), lambda i, g: (0, i)),
                  const((nc, N_KVG)), const((N_KVG, nc)), const((nsel, nc)),
                  const((NSA_KV_HEADS, ntile, SEL_TILE, N_KVG)), const((ntile, NSA_KV_HEADS, HEAD_DIM, SEL_TILE))]
                 + kwin_specs + vwin_specs
                 + [pl.BlockSpec((N_GATE_PAD, Q_BLOCK), lambda i, g: (0, i))],
        out_specs=pl.BlockSpec((Q_BLOCK, NSA_GROUP * HEAD_DIM), lambda i, g: (i, g)),
        out_shape=jax.ShapeDtypeStruct((T, N_Q), F32),
        scratch_shapes=[pltpu.VMEM((nsel, L), F32), pltpu.VMEM((SEL_TILE, L), F32), pltpu.VMEM((SEL_TILE, L), F32),
                        pltpu.VMEM((1, L), F32), pltpu.VMEM((1, L), F32), pltpu.VMEM((HEAD_DIM, L), F32),
                        pltpu.VMEM((HEAD_DIM, L), F32)],
        compiler_params=pltpu.CompilerParams(dimension_semantics=("arbitrary", "arbitrary"),
                                             vmem_limit_bytes=VMEM_LIMIT),
    )(qT, kc, vcT, ovT, ksel4, vselT, *([kwin3] * WIN_TILES), *([vwinT] * WIN_TILES), gT)


def _sb_kernel(q_ref, k_ref, v_ref, o_ref, acc_scr, c_scr):
    i = pl.program_id(0)
    Q, W = Q_BLOCK, 2 * HEAD_DIM
    lane = lax.broadcasted_iota(jnp.int32, (Q, W), 1)
    zero = jnp.zeros((), BF16)
    qh = []
    for hd in range(SB_HEADS):
        q2 = q_ref[:, (hd // 2) * W:(hd // 2 + 1) * W]
        qh.append(jnp.where((lane < HEAD_DIM) if hd % 2 == 0 else (lane >= HEAD_DIM), q2, zero))
    row = lax.broadcasted_iota(jnp.int32, (Q, SB_TILE), 0)
    col = lax.broadcasted_iota(jnp.int32, (Q, SB_TILE), 1)
    later = (row > col).astype(BF16)
    acc_scr[...] = jnp.zeros(acc_scr.shape, F32)
    c_scr[...] = jnp.zeros(c_scr.shape, F32)

    def tile(kt, diagonal):
        rows = pl.ds(pl.multiple_of(kt * SB_TILE, SB_TILE), SB_TILE)
        log_nots, log_sigs = [], []
        for hd in range(SB_HEADS):
            z = _dot_nt(qh[hd], k_ref[rows, (hd // 2) * W:(hd // 2 + 1) * W])
            soft = jnp.log1p(jnp.exp(-jnp.abs(z)))
            log_not = -jnp.maximum(z, 0.0) - soft
            if diagonal:
                log_not = jnp.where(col < row, log_not, 0.0)
            log_nots.append(log_not)
            log_sigs.append(jnp.minimum(z, 0.0) - soft)
        his = [x.astype(BF16) for x in log_nots]
        los = [(x - h.astype(F32)).astype(BF16) for x, h in zip(log_nots, his)]
        tails = _dot(jnp.concatenate(his + los, axis=0), later)
        worst = None
        for hd in range(SB_HEADS):
            c_old = c_scr[hd]
            after = tails[hd * Q:(hd + 1) * Q] + tails[(SB_HEADS + hd) * Q:(SB_HEADS + hd + 1) * Q] + c_old
            a = jnp.exp(log_sigs[hd] + after)
            if diagonal:
                a = jnp.where(col < row, a, 0.0)
            acc_scr[hd] = acc_scr[hd] + _dot(a.astype(BF16), v_ref[rows, (hd // 2) * W:(hd // 2 + 1) * W])
            c_new = c_old + jnp.sum(log_nots[hd], axis=1, keepdims=True)
            c_scr[hd] = c_new
            worst = c_new if worst is None else jnp.maximum(worst, c_new)
        return jnp.max(worst)

    worst0 = tile(i, True)

    def cond(carry):
        kt, worst = carry
        return (kt >= 0) & (worst >= SB_SKIP_LOG)

    def body(carry):
        kt, _ = carry
        return kt - 1, tile(kt, False)

    lax.while_loop(cond, body, (i - 1, worst0))
    for pr in range(SB_HEADS // 2):
        o_ref[:, pr * W:(pr + 1) * W] = jnp.where(lane < HEAD_DIM, acc_scr[2 * pr], acc_scr[2 * pr + 1])


def _stick_breaking(sbq, sbk, sbv):
    T = sbq.shape[0]
    nb = T // Q_BLOCK
    W = 2 * HEAD_DIM
    return pl.pallas_call(
        _sb_kernel,
        grid=(nb,),
        in_specs=[pl.BlockSpec((Q_BLOCK, N_SBH), lambda i: (i, 0)),
                  pl.BlockSpec((T, N_SBH), lambda i: (0, 0)),
                  pl.BlockSpec((T, N_SBH), lambda i: (0, 0))],
        out_specs=pl.BlockSpec((Q_BLOCK, N_SBH), lambda i: (i, 0)),
        out_shape=jax.ShapeDtypeStruct((T, N_SBH), F32),
        scratch_shapes=[pltpu.VMEM((SB_HEADS, Q_BLOCK, W), F32), pltpu.VMEM((SB_HEADS, Q_BLOCK, 1), F32)],
        compiler_params=pltpu.CompilerParams(dimension_semantics=("arbitrary",),
                                             vmem_limit_bytes=VMEM_LIMIT),
    )(sbq, sbk, sbv)


def _mix_kernel(x_ref, on_ref, os_ref, nwn_ref, nws_ref, wo_ref, fw_ref, rhi_ref, rlo_ref, rb_ref,
                x1_ref, h2_ref, lg_ref, lgT_ref):
    n1 = _rms(on_ref[...], nwn_ref[...]).astype(BF16)
    n2 = _rms(os_ref[...], nws_ref[...]).astype(BF16)
    x1 = x_ref[...] + _dot(n1, wo_ref[0:N_Q]) + _dot(n2, wo_ref[N_Q:N_Q + N_SBH])
    x1_ref[...] = x1
    h2 = _rms(x1, fw_ref[...])
    hi = h2.astype(BF16)
    lo = (h2 - hi.astype(F32)).astype(BF16)
    h2_ref[...] = hi
    lg = _dot(hi, rhi_ref[...]) + _dot(hi, rlo_ref[...]) + _dot(lo, rhi_ref[...]) + rb_ref[...]
    lg_ref[...] = lg
    lgT_ref[...] = lg.T[0:ROUTER_ROWS]


def _mix(x, o_nsa, o_sb, nsa_norm_w, sb_norm_w, w_out, ffn_norm_w, rg_w, rg_b, re_w, re_b):
    T, D = x.shape
    R = PROJ_ROWS
    pad = ROUTER_LANES - N_GROUPS - N_EXPERTS
    wr = jnp.concatenate([rg_w, re_w, jnp.zeros((D, pad), F32)], axis=1)
    wr_hi = wr.astype(BF16)
    wr_lo = (wr - wr_hi.astype(F32)).astype(BF16)
    rb = jnp.concatenate([rg_b, re_b, jnp.zeros((pad,), F32)]).reshape(1, ROUTER_LANES)
    full = lambda shape: pl.BlockSpec(shape, lambda i: (0,) * len(shape))
    rows = lambda n: pl.BlockSpec((R, n), lambda i: (i, 0))
    return pl.pallas_call(
        _mix_kernel,
        grid=(T // R,),
        in_specs=[rows(D), rows(N_Q), rows(N_SBH), full((1, N_Q)), full((1, N_SBH)), full((N_Q + N_SBH, D)),
                  full((1, D)), full((D, ROUTER_LANES)), full((D, ROUTER_LANES)), full((1, ROUTER_LANES))],
        out_specs=[rows(D), rows(D), rows(ROUTER_LANES), pl.BlockSpec((ROUTER_ROWS, R), lambda i: (0, i))],
        out_shape=[jax.ShapeDtypeStruct((T, D), F32), jax.ShapeDtypeStruct((T, D), BF16),
                   jax.ShapeDtypeStruct((T, ROUTER_LANES), F32), jax.ShapeDtypeStruct((ROUTER_ROWS, T), F32)],
        compiler_params=pltpu.CompilerParams(dimension_semantics=("arbitrary",),
                                             vmem_limit_bytes=VMEM_LIMIT),
    )(x, o_nsa, o_sb, nsa_norm_w.reshape(1, N_Q), sb_norm_w.reshape(1, N_SBH), w_out.astype(BF16),
      ffn_norm_w.reshape(1, D), wr_hi, wr_lo, rb)


def _routing(lg, axis):
    pos_i = lax.broadcasted_iota(jnp.int32, lg.shape, axis)
    pos = pos_i.astype(F32)
    first_max = lambda v, mx: jnp.min(jnp.where(v == mx, pos, float(ROUTER_LANES)), axis=axis, keepdims=True)
    gl = jnp.where(pos_i < N_GROUPS, lg, -jnp.inf)
    gmax = jnp.max(gl, axis=axis, keepdims=True)
    grp = first_max(gl, gmax)
    g_gate = 1.0 / jnp.sum(jnp.exp(gl - gmax), axis=axis, keepdims=True)
    e_idx = pos_i - N_GROUPS
    e_grp = (e_idx >> 2).astype(F32)
    in_grp = (e_idx >= 0) & (e_idx < N_EXPERTS) & (e_grp == grp)
    el = jnp.where(in_grp, lg, -jnp.inf)
    top1 = jnp.max(el, axis=axis, keepdims=True)
    i1 = first_max(el, top1)
    el2 = jnp.where(pos == i1, -jnp.inf, el)
    top2 = jnp.max(el2, axis=axis, keepdims=True)
    i2 = first_max(el2, top2)
    e2 = jnp.exp(top2 - top1)
    w1 = 1.0 / (1.0 + e2)
    w2 = e2 / (1.0 + e2)
    weight = g_gate * (jnp.where(pos == i1, w1, 0.0) + jnp.where(pos == i2, w2, 0.0))
    routed = jnp.where(pos == i1, 1.0, 0.0) + jnp.where(pos == i2, 1.0, 0.0)
    return weight, routed


def _moe_kernel(h_ref, lg_ref, lgT_ref, x1_ref, before_ref, beforeT_ref, wg_ref, wu_ref, wd_ref, fw_ref, o_ref,
                acc_scr, rank_scr, cw_scr, rankT_scr):
    e = pl.program_id(1)
    rows = h_ref.shape[0]

    @pl.when(e == 0)
    def _():
        acc_scr[...] = jnp.zeros(acc_scr.shape, F32)
        weight, routed = _routing(lg_ref[...], 1)
        rank = _dot(beforeT_ref[...], routed.astype(BF16))
        rank_scr[...] = jnp.where(routed > 0.0, rank, -1.0)
        cw_scr[...] = weight
        _, routed_t = _routing(lgT_ref[...], 0)
        rank_t = _dot(routed_t.astype(BF16), before_ref[...])
        rankT_scr[...] = jnp.where(routed_t > 0.0, rank_t, -1.0)

    lane = lax.broadcasted_iota(jnp.int32, (rows, ROUTER_LANES), 1)
    mine = lane == e + N_GROUPS
    rank_col = jnp.sum(jnp.where(mine, rank_scr[...], 0.0), axis=1, keepdims=True)
    w_col = jnp.sum(jnp.where(mine, cw_scr[...], 0.0), axis=1, keepdims=True)
    rank_row = rankT_scr[pl.ds(e + N_GROUPS, 1), :]
    n_routed = jnp.max(rank_row).astype(jnp.int32) + 1
    slot_col = lax.broadcasted_iota(jnp.int32, (MOE_CAP, 1), 0)
    slot_row = lax.broadcasted_iota(jnp.int32, (1, MOE_CAP), 1)

    def chunk(ch, carry):
        base = ch * MOE_CAP
        gather = jnp.where(rank_row == (slot_col + base).astype(F32), 1.0, 0.0).astype(BF16)
        scatter = jnp.where(rank_col == (slot_row + base).astype(F32), 1.0, 0.0).astype(BF16)
        xg = _dot(gather, h_ref[...]).astype(BF16)
        a = _dot(xg, wg_ref[0])
        b = _dot(xg, wu_ref[0])
        act = (a * jax.nn.sigmoid(a) * b).astype(BF16)
        y = _dot(act, wd_ref[0])
        acc_scr[...] += _dot(scatter, y.astype(BF16)) * w_col
        return carry

    lax.fori_loop(0, (n_routed + (MOE_CAP - 1)) // MOE_CAP, chunk, 0)

    @pl.when(e == N_EXPERTS - 1)
    def _():
        o_ref[...] = _rms(x1_ref[...] + acc_scr[...], fw_ref[...])


def _moe(h2, logits, logitsT, x1, w_gate, w_up, w_down, final_norm_w):
    T, D = x1.shape
    R = min(MOE_ROWS, T)
    before = jnp.triu(jnp.ones((R, R), BF16), k=1)
    rows = lambda n: pl.BlockSpec((R, n), lambda i, e: (i, 0))
    const = lambda a, b: pl.BlockSpec((a, b), lambda i, e: (0, 0))
    per_expert = lambda a, b: pl.BlockSpec((1, a, b), lambda i, e: (e, 0, 0))
    return pl.pallas_call(
        _moe_kernel,
        grid=(T // R, N_EXPERTS),
        in_specs=[rows(D), rows(ROUTER_LANES), pl.BlockSpec((ROUTER_ROWS, R), lambda i, e: (0, i)), rows(D),
                  const(R, R), const(R, R),
                  per_expert(D, EXPERT_FF), per_expert(D, EXPERT_FF), per_expert(EXPERT_FF, D), const(1, D)],
        out_specs=rows(D),
        out_shape=jax.ShapeDtypeStruct((T, D), F32),
        scratch_shapes=[pltpu.VMEM((R, D), F32), pltpu.VMEM((R, ROUTER_LANES), F32),
                        pltpu.VMEM((R, ROUTER_LANES), F32), pltpu.VMEM((ROUTER_ROWS, R), F32)],
        compiler_params=pltpu.CompilerParams(dimension_semantics=("arbitrary", "arbitrary"),
                                             vmem_limit_bytes=VMEM_LIMIT),
    )(h2, logits, logitsT, x1, before, before.T, w_gate.astype(BF16), w_up.astype(BF16), w_down.astype(BF16),
      final_norm_w.reshape(1, D))


def kernel(x, positions, attn_norm_w, w_in, cmp_pe_k, cmp_pe_v, cmp_k_w1, cmp_k_w2, cmp_v_w1, cmp_v_w2,
           nsa_out_norm_w, sb_out_norm_w, w_out, ffn_norm_w, router_group_w, router_group_b,
           router_expert_w, router_expert_b, w_gate, w_up, w_down, final_norm_w):
    B, T, D = x.shape
    assert B == 1 and T % SEL_TILE == 0 and T % PROJ_ROWS == 0 and T // SEL_LEN >= SEL_TOP
    assert attn_norm_w.shape[0] == 1, "the final norm is fused into the (single) layer's MoE kernel"
    xs = x.reshape(T, D)
    pos = positions.reshape(T)
    (cmpk, cmpv, sbq, sbk, sbv, qT, ksel, kwin, vselT, vwinT, gT) = _project(xs, pos, attn_norm_w[0], w_in[0])
    kc, vcT = _compress(cmpk, cmpv, cmp_pe_k[0], cmp_pe_v[0], cmp_k_w1[0], cmp_k_w2[0], cmp_v_w1[0], cmp_v_w2[0])
    o_nsa = _nsa(qT, kc, vcT, ksel, vselT, kwin, vwinT, gT)
    o_sb = _stick_breaking(sbq, sbk, sbv)
    x1, h2, logits, logitsT = _mix(xs, o_nsa, o_sb, nsa_out_norm_w[0], sb_out_norm_w[0], w_out[0], ffn_norm_w[0],
                                   router_group_w[0], router_group_b[0], router_expert_w[0], router_expert_b[0])
    out = _moe(h2, logits, logitsT, x1, w_gate[0], w_up[0], w_down[0], final_norm_w)
    return out.reshape(B, T, D)
```

```python
import functools

import jax
import jax.numpy as jnp
from jax import lax
from jax.experimental import pallas as pl
from jax.experimental.pallas import tpu as pltpu

HEAD_DIM = 64
NSA_HEADS = 8
NSA_KV_HEADS = 2
NSA_GROUP = NSA_HEADS // NSA_KV_HEADS
SB_HEADS = 8
ROPE_THETA = 500000.0
ROPE_DIM = HEAD_DIM // 4
ROPE_HALF = ROPE_DIM // 2
CMP_LEN = 32
CMP_STRIDE = 16
CMP_HIDDEN = 256
SEL_LEN = 64
SEL_TOP = 16
SEL_BONUS = 1.0e4
WINDOW = 512
Q_BLOCK = 128
N_GROUPS = 4
EXPERTS_PER_GROUP = 4
N_EXPERTS = N_GROUPS * EXPERTS_PER_GROUP
EXPERT_FF = 512
EPS = 1e-6
NEG = -1e30
LOWEST = -3.0e38

N_Q = NSA_HEADS * HEAD_DIM
N_KVG = NSA_KV_HEADS * HEAD_DIM
N_GATE = NSA_HEADS * 3
N_GATE_PAD = 32
N_SBH = SB_HEADS * HEAD_DIM
SCALE = HEAD_DIM ** -0.5
LOG2E = 1.4426950408889634
SEL_BLOCKS_PER_TILE = 8
BIAS_ROWS = 16

PROJ_ROWS = 512
SEL_TILE = 512
WIN_TILES = WINDOW // Q_BLOCK + 1
SB_TILE = 128
MOE_ROWS = 1024
MOE_CAP = 256
ROUTER_LANES = 128
ROUTER_ROWS = 32
SB_SKIP_LOG = -104.0

VMEM_LIMIT = 56 * 1024 * 1024

BF16 = jnp.bfloat16
F32 = jnp.float32


def _rms(x, w):
    return x * lax.rsqrt(jnp.mean(x * x, axis=-1, keepdims=True) + EPS) * w


def _dot(a, b):
    return jnp.dot(a, b, preferred_element_type=F32)


def _dot_nt(a, b):
    return lax.dot_general(a, b, (((1,), (1,)), ((), ())), preferred_element_type=F32)


def _split3(x):
    hi = x.astype(BF16)
    r1 = x - hi.astype(F32)
    mid = r1.astype(BF16)
    lo = (r1 - mid.astype(F32)).astype(BF16)
    return hi, mid, lo


def _proj_kernel(x_ref, nw_ref, wn_ref, wt_ref, pos_ref, invf_ref,
                 cmpk_ref, cmpv_ref, sbq_ref, sbk_ref, sbv_ref,
                 qT_ref, ksel_ref, kwin_ref, vselT_ref, vwinT_ref, gT_ref):
    h = _rms(x_ref[...], nw_ref[...]).astype(BF16)
    p1 = _dot(h, wn_ref[...])
    cmpk_ref[...] = p1[:, 0:N_KVG]
    cmpv_ref[...] = p1[:, N_KVG:2 * N_KVG]
    o = 2 * N_KVG
    sbq_ref[...] = (p1[:, o:o + N_SBH] * SCALE).astype(BF16)
    sbk_ref[...] = p1[:, o + N_SBH:o + 2 * N_SBH].astype(BF16)
    sbv_ref[...] = p1[:, o + 2 * N_SBH:o + 3 * N_SBH].astype(BF16)

    p2 = _dot_nt(wt_ref[...], h)
    ang = invf_ref[...] * pos_ref[...].astype(F32)
    cos, sin = jnp.cos(ang), jnp.sin(ang)
    n_rope_heads = NSA_HEADS + 2 * NSA_KV_HEADS
    roped = []
    for hd in range(n_rope_heads):
        blk = p2[hd * HEAD_DIM:(hd + 1) * HEAD_DIM]
        x1, x2 = blk[0:ROPE_HALF], blk[ROPE_HALF:ROPE_DIM]
        roped.append(jnp.concatenate(
            [x1 * cos - x2 * sin, x2 * cos + x1 * sin, blk[ROPE_DIM:]], axis=0))
    qT_ref[...] = (jnp.concatenate(roped[:NSA_HEADS], axis=0) * (SCALE * LOG2E)).astype(BF16)
    kT = jnp.concatenate(roped[NSA_HEADS:], axis=0)
    kn = kT.T.astype(BF16)
    r_blk = lax.broadcasted_iota(jnp.int32, (PROJ_ROWS, HEAD_DIM), 0) >> 6
    c_idx = lax.broadcasted_iota(jnp.int32, (PROJ_ROWS, HEAD_DIM), 1)
    onehot = jnp.where(r_blk == c_idx, 1.0, 0.0).astype(BF16)
    for gk in range(NSA_KV_HEADS):
        ksel_ref[gk] = jnp.concatenate([kn[:, gk * HEAD_DIM:(gk + 1) * HEAD_DIM], onehot], axis=1)
    kwin_ref[...] = kn[:, N_KVG:2 * N_KVG]
    o = N_Q + 2 * N_KVG
    for gk in range(NSA_KV_HEADS):
        vselT_ref[0, gk] = p2[o + gk * HEAD_DIM:o + (gk + 1) * HEAD_DIM].astype(BF16)
    vw = p2[o + N_KVG:o + 2 * N_KVG].astype(BF16)
    for j in range(PROJ_ROWS // Q_BLOCK):
        vwinT_ref[j] = vw[:, j * Q_BLOCK:(j + 1) * Q_BLOCK]
    o = o + 2 * N_KVG
    gT_ref[...] = jax.nn.sigmoid(p2[o:o + N_GATE_PAD])


def _project(x, positions, attn_norm_w, w_in):
    T, D = x.shape
    R = PROJ_ROWS
    o1, o2, o3 = N_Q, N_Q + 6 * N_KVG, N_Q + 6 * N_KVG + N_GATE
    kv = lambda i: w_in[:, o1 + i * N_KVG:o1 + (i + 1) * N_KVG]
    w_nat = jnp.concatenate([kv(0), kv(1), w_in[:, o3:]], axis=1).astype(BF16)
    w_t = jnp.concatenate(
        [w_in[:, :o1], kv(2), kv(4), kv(3), kv(5), w_in[:, o2:o3],
         jnp.zeros((D, N_GATE_PAD - N_GATE), w_in.dtype)], axis=1).T.astype(BF16)
    inv_freq = ROPE_THETA ** (-jnp.arange(0, ROPE_DIM, 2, dtype=F32) / ROPE_DIM)
    n_nat, n_t = w_nat.shape[1], w_t.shape[0]
    full = lambda shape: pl.BlockSpec(shape, lambda i: (0,) * len(shape))
    rows = lambda n: pl.BlockSpec((R, n), lambda i: (i, 0))
    cols = lambda n: pl.BlockSpec((n, R), lambda i: (0, i))
    return pl.pallas_call(
        _proj_kernel,
        grid=(T // R,),
        in_specs=[rows(D), full((1, D)), full((D, n_nat)), full((n_t, D)), cols(1), full((ROPE_HALF, 1))],
        out_specs=[rows(N_KVG), rows(N_KVG), rows(N_SBH), rows(N_SBH), rows(N_SBH),
                   cols(N_Q), pl.BlockSpec((NSA_KV_HEADS, R, N_KVG), lambda i: (0, i, 0)), rows(N_KVG),
                   pl.BlockSpec((1, NSA_KV_HEADS, HEAD_DIM, R), lambda i: (i, 0, 0, 0)),
                   pl.BlockSpec((R // Q_BLOCK, N_KVG, Q_BLOCK), lambda i: (i, 0, 0)),
                   cols(N_GATE_PAD)],
        out_shape=[jax.ShapeDtypeStruct((T, N_KVG), F32), jax.ShapeDtypeStruct((T, N_KVG), F32),
                   jax.ShapeDtypeStruct((T, N_SBH), BF16), jax.ShapeDtypeStruct((T, N_SBH), BF16),
                   jax.ShapeDtypeStruct((T, N_SBH), BF16),
                   jax.ShapeDtypeStruct((N_Q, T), BF16),
                   jax.ShapeDtypeStruct((NSA_KV_HEADS, T, N_KVG), BF16), jax.ShapeDtypeStruct((T, N_KVG), BF16),
                   jax.ShapeDtypeStruct((T // R, NSA_KV_HEADS, HEAD_DIM, R), BF16),
                   jax.ShapeDtypeStruct((T // Q_BLOCK, N_KVG, Q_BLOCK), BF16),
                   jax.ShapeDtypeStruct((N_GATE_PAD, T), F32)],
        compiler_params=pltpu.CompilerParams(dimension_semantics=("arbitrary",),
                                             vmem_limit_bytes=VMEM_LIMIT),
    )(x, attn_norm_w.reshape(1, D), w_nat, w_t, positions.reshape(1, T), inv_freq.reshape(ROPE_HALF, 1))


def _compress_kernel(x_ref, pea_ref, peb_ref, wa_ref, wb_ref, w2_ref, nat_ref, tr_ref):
    x = x_ref[0]
    nc = x.shape[0]
    ha = _dot((x + pea_ref[0]).astype(BF16), wa_ref[0])
    hb = _dot((x + peb_ref[0]).astype(BF16), wb_ref[0])
    hid = ha + pltpu.roll(hb, nc - 1, 0)
    act = (hid * jax.nn.sigmoid(hid)).astype(BF16)
    out = _dot(act, w2_ref[0])
    nat_ref[0] = out.astype(BF16)
    tr_ref[0] = out.T.astype(BF16)


def _compress(cmpk, cmpv, pe_k, pe_v, k_w1, k_w2, v_w1, v_w2):
    T = cmpk.shape[0]
    nc = T // CMP_STRIDE
    half = CMP_LEN // 2
    G = NSA_KV_HEADS
    width = half * N_KVG
    x = jnp.stack([cmpk.reshape(nc, width), cmpv.reshape(nc, width)])
    eye = jnp.eye(G, dtype=F32)

    def pe_rows(pe):
        return jnp.broadcast_to(pe[:, None, :], (half, G, HEAD_DIM)).reshape(1, width)

    def w1_block(w1):
        w = w1.reshape(half, HEAD_DIM, CMP_HIDDEN)
        return jnp.einsum('ldj,gh->lgdhj', w, eye).reshape(width, G * CMP_HIDDEN).astype(BF16)

    def w2_block(w2):
        return jnp.einsum('jd,gh->gjhd', w2, eye).reshape(G * CMP_HIDDEN, N_KVG).astype(BF16)

    hw = half * HEAD_DIM
    pea = jnp.stack([pe_rows(pe_k[:half]), pe_rows(pe_v[:half])])
    peb = jnp.stack([pe_rows(pe_k[half:]), pe_rows(pe_v[half:])])
    wa = jnp.stack([w1_block(k_w1[:hw]), w1_block(v_w1[:hw])])
    wb = jnp.stack([w1_block(k_w1[hw:]), w1_block(v_w1[hw:])])
    w2 = jnp.stack([w2_block(k_w2), w2_block(v_w2)])
    blk = lambda a, b: pl.BlockSpec((1, a, b), lambda i: (i, 0, 0))
    nat, tr = pl.pallas_call(
        _compress_kernel,
        grid=(2,),
        in_specs=[blk(nc, width), blk(1, width), blk(1, width), blk(width, G * CMP_HIDDEN),
                  blk(width, G * CMP_HIDDEN), blk(G * CMP_HIDDEN, N_KVG)],
        out_specs=[blk(nc, N_KVG), blk(N_KVG, nc)],
        out_shape=[jax.ShapeDtypeStruct((2, nc, N_KVG), BF16), jax.ShapeDtypeStruct((2, N_KVG, nc), BF16)],
        compiler_params=pltpu.CompilerParams(dimension_semantics=("arbitrary",),
                                             vmem_limit_bytes=VMEM_LIMIT),
    )(x, pea, peb, wa, wb, w2)
    return nat[0], tr[1]


def _nsa_kernel(q_ref, kc_ref, vcT_ref, ovT_ref, ksel_ref, vselT_ref, *rest):
    kwin_refs = rest[0:WIN_TILES]
    vwin_refs = rest[WIN_TILES:2 * WIN_TILES]
    (gT_ref, o_ref, bias_scr, s0_scr, s1_scr, t0_scr, t1_scr, m_scr, l_scr, acc_scr,
     oc_scr) = rest[2 * WIN_TILES:]
    i = pl.program_id(0)
    g = pl.program_id(1)
    R, Q = NSA_GROUP, Q_BLOCK
    L = R * Q
    nsel = bias_scr.shape[0]

    qg = jnp.concatenate(
        [q_ref[pl.ds(pl.multiple_of(g * (R * HEAD_DIM) + r * HEAD_DIM, HEAD_DIM), HEAD_DIM), :] for r in range(R)],
        axis=1)
    row_grp = lax.broadcasted_iota(jnp.int32, (N_KVG, L), 0) >> 6
    qz = jnp.where(row_grp == g, jnp.concatenate([qg] * NSA_KV_HEADS, axis=0), jnp.zeros((), BF16))

    def own_rows(x):
        return jnp.where(g == 0, x[0:HEAD_DIM], x[HEAD_DIM:2 * HEAD_DIM])

    t_lane = i * Q + (lax.broadcasted_iota(jnp.int32, (1, L), 1) & (Q - 1))

    nc = kc_ref.shape[0]
    q_pos = i * Q + lax.broadcasted_iota(jnp.int32, (1, Q), 1)
    bq = (q_pos >> 6).astype(F32)

    def compress_and_select(n_eff, first_class):
        m_eff = n_eff // (SEL_LEN // CMP_STRIDE)
        sc = _dot(kc_ref[0:n_eff, :], qz)
        cmp_end = lax.broadcasted_iota(jnp.int32, (n_eff, 1), 0) * CMP_STRIDE + (CMP_LEN - 1)
        sc = jnp.where(cmp_end <= t_lane, sc, NEG)
        mxc = jnp.max(sc, axis=0, keepdims=True)
        mxc = jnp.where(mxc < 0.5 * NEG, 0.0, mxc)
        ec = jnp.exp2(sc - mxc)
        pc = ec * (1.0 / jnp.maximum(jnp.sum(ec, axis=0, keepdims=True), 1e-30))
        oc_scr[...] = own_rows(_dot(vcT_ref[:, 0:n_eff], pc.astype(BF16)))
        psum = pc[:, 0:Q]
        for r in range(1, R):
            psum = psum + pc[:, r * Q:(r + 1) * Q]
        ov = ovT_ref[0:m_eff, 0:n_eff]
        imp = sum(_dot(ov, part) for part in _split3(psum))
        m_idx = lax.broadcasted_iota(jnp.int32, (m_eff, Q), 0).astype(F32)
        allowed = m_idx <= bq
        forced = (m_idx == 0.0) | (m_idx == bq) | (m_idx == bq - 1.0)
        if first_class:
            score = jnp.where(allowed, imp + jnp.where(forced, SEL_BONUS, 0.0), NEG)
            n_pick = min(SEL_TOP, m_eff)
        else:
            score = jnp.where(allowed & jnp.logical_not(forced), imp, NEG)
            n_pick = SEL_TOP - 3
        for _ in range(n_pick):
            best = jnp.max(score, axis=0, keepdims=True)
            first = jnp.min(jnp.where(score == best, m_idx, float(m_eff)), axis=0, keepdims=True)
            score = jnp.where(m_idx == first, LOWEST, score)
        picked = score < 0.5 * LOWEST
        if not first_class:
            picked = picked | forced
        bias = jnp.where(allowed & picked, 0.0, NEG)
        bias_scr[0:m_eff, :] = jnp.concatenate([bias] * R, axis=1)
        if m_eff < nsel:
            bias_scr[m_eff:nsel, :] = jnp.full((nsel - m_eff, L), NEG, F32)

    sizes = sorted({max(Q, nc >> shift) for shift in range(4)})
    lo = 0
    for n_eff in sizes:
        hi = n_eff // (Q // CMP_STRIDE)
        pl.when((i >= lo) & (i < hi))(functools.partial(compress_and_select, n_eff, lo == 0))
        lo = hi
    ocT = oc_scr[...]

    m_scr[...] = jnp.full(m_scr.shape, NEG, F32)
    l_scr[...] = jnp.zeros(l_scr.shape, F32)
    acc_scr[...] = jnp.zeros(acc_scr.shape, F32)
    rhs_pad = jnp.zeros((N_KVG - HEAD_DIM - BIAS_ROWS, L), BF16)
    bias_pad = jnp.zeros((BIAS_ROWS - SEL_BLOCKS_PER_TILE, L), F32)

    buf0, buf1 = (s0_scr, t0_scr), (s1_scr, t1_scr)

    def scores(kt, dst):
        brows = bias_scr[pl.ds(pl.multiple_of(kt * SEL_BLOCKS_PER_TILE, SEL_BLOCKS_PER_TILE), SEL_BLOCKS_PER_TILE), :]
        rhs = jnp.concatenate([qg, jnp.concatenate([brows, bias_pad], axis=0).astype(BF16), rhs_pad], axis=0)
        s = _dot(ksel_ref[g, kt], rhs)
        dst[0][...] = s
        dst[1][...] = jnp.max(s, axis=0, keepdims=True)

    def absorb(kt, src):
        m_old = m_scr[...]
        m_new = jnp.maximum(m_old, src[1][...])
        alpha = jnp.exp2(m_old - m_new)
        p = jnp.exp2(src[0][...] - m_new)
        l_scr[...] = alpha * l_scr[...] + jnp.sum(p, axis=0, keepdims=True)
        acc_scr[...] = alpha * acc_scr[...] + _dot(vselT_ref[kt, g], p.astype(BF16))
        m_scr[...] = m_new

    def step(kt, src, dst):
        scores(kt + 1, dst)
        absorb(kt, src)

    def two_steps(kt):
        step(kt, buf0, buf1)
        step(kt + 1, buf1, buf0)

    k_row = lax.broadcasted_iota(jnp.int32, (Q, 1), 0)
    q_lane = lax.broadcasted_iota(jnp.int32, (1, L), 1) & (Q - 1)
    n_full = (i * Q) // SEL_TILE

    def last_tile(src):
        r0 = pl.multiple_of(i * Q - n_full * SEL_TILE, Q)
        src[0][pl.ds(r0, Q), :] = jnp.where(k_row <= q_lane, src[0][pl.ds(r0, Q), :], NEG)
        src[1][...] = jnp.max(src[0][...], axis=0, keepdims=True)
        absorb(n_full, src)

    scores(0, buf0)

    def four_steps(j, carry):
        two_steps(4 * j)
        two_steps(4 * j + 2)
        return carry

    lax.fori_loop(0, n_full // 4, four_steps, 0)

    @pl.when((n_full & 2) != 0)
    def _():
        two_steps(n_full & -4)

    @pl.when((n_full & 1) == 0)
    def _():
        last_tile(buf0)

    @pl.when((n_full & 1) == 1)
    def _():
        step(n_full - 1, buf0, buf1)
        last_tile(buf1)

    osT = acc_scr[...] * (1.0 / l_scr[...])

    sw = []
    for w in range(WIN_TILES):
        j = i - (WIN_TILES - 1) + w
        s = _dot(kwin_refs[w][0], qz)
        if w == 0:
            s = jnp.where(k_row > q_lane, s, NEG)
        if w == WIN_TILES - 1:
            s = jnp.where(k_row <= q_lane, s, NEG)
        else:
            s = jnp.where(j >= 0, s, NEG)
        sw.append(s)
    mxw = functools.reduce(jnp.maximum, [jnp.max(s, axis=0, keepdims=True) for s in sw])
    ew = [jnp.exp2(s - mxw) for s in sw]
    denw = jnp.maximum(sum(jnp.sum(e, axis=0, keepdims=True) for e in ew), 1e-30)
    owT = sum(_dot(vwin_refs[w][0], ew[w].astype(BF16)) for w in range(WIN_TILES))
    owT = own_rows(owT) * (1.0 / denw)

    def gate_row(j):
        return jnp.concatenate(
            [gT_ref[pl.ds((g * R + r) * 3 + j, 1), :] for r in range(R)], axis=1)

    oT = gate_row(0) * ocT + gate_row(1) * osT + gate_row(2) * owT
    o_hd = jnp.concatenate([oT[:, r * Q:(r + 1) * Q] for r in range(R)], axis=0)
    o_ref[...] = o_hd.T


def _nsa(qT, kc, vcT, ksel, vselT, kwin, vwinT, gT):
    T = qT.shape[1]
    nb = T // Q_BLOCK
    nc = kc.shape[0]
    nsel = T // SEL_LEN
    ntile = T // SEL_TILE
    n = jnp.arange(nc)[None, :] * CMP_STRIDE
    m = jnp.arange(nsel)[:, None] * SEL_LEN
    ovT = ((n < m + SEL_LEN) & (n + CMP_LEN > m)).astype(BF16)
    ksel4 = ksel.reshape(NSA_KV_HEADS, ntile, SEL_TILE, N_KVG)
    kwin3 = kwin.reshape(nb, Q_BLOCK, N_KVG)
    L = NSA_GROUP * Q_BLOCK
    const = lambda shape: pl.BlockSpec(shape, lambda i, g: (0,) * len(shape))
    kwin_specs = [pl.BlockSpec((1, Q_BLOCK, N_KVG),
                               functools.partial(lambda i, g, w: (jnp.maximum(i - (WIN_TILES - 1) + w, 0), 0, 0), w=w))
                  for w in range(WIN_TILES)]
    vwin_specs = [pl.BlockSpec((1, N_KVG, Q_BLOCK),
                               functools.partial(lambda i, g, w: (jnp.maximum(i - (WIN_TILES - 1) + w, 0), 0, 0), w=w))
                  for w in range(WIN_TILES)]
    return pl.pallas_call(
        _nsa_kernel,
        grid=(nb, NSA_KV_HEADS),
        in_specs=[pl.BlockSpec((N_Q, Q_BLOCK), lambda i, g: (0, i)),
                  const((nc, N_KVG)), const((N_KVG, nc)), const((nsel, nc)),
                  const((NSA_KV_HEADS, ntile, SEL_TILE, N_KVG)), const((ntile, NSA_KV_HEADS, HEAD_DIM, SEL_TILE))]
                 + kwin_specs + vwin_specs
                 + [pl.BlockSpec((N_GATE_PAD, Q_BLOCK), lambda i, g: (0, i))],
        out_specs=pl.BlockSpec((Q_BLOCK, NSA_GROUP * HEAD_DIM), lambda i, g: (i, g)),
        out_shape=jax.ShapeDtypeStruct((T, N_Q), F32),
        scratch_shapes=[pltpu.VMEM((nsel, L), F32), pltpu.VMEM((SEL_TILE, L), F32), pltpu.VMEM((SEL_TILE, L), F32),
                        pltpu.VMEM((1, L), F32), pltpu.VMEM((1, L), F32),
                        pltpu.VMEM((1, L), F32), pltpu.VMEM((1, L), F32), pltpu.VMEM((HEAD_DIM, L), F32),
                        pltpu.VMEM((HEAD_DIM, L), F32)],
        compiler_params=pltpu.CompilerParams(dimension_semantics=("arbitrary", "arbitrary"),
                                             vmem_limit_bytes=VMEM_LIMIT),
    )(qT, kc, vcT, ovT, ksel4, vselT, *([kwin3] * WIN_TILES), *([vwinT] * WIN_TILES), gT)


def _sb_kernel(q_ref, k_ref, v_ref, o_ref, acc_scr, c_scr):
    i = pl.program_id(0)
    Q, W = Q_BLOCK, 2 * HEAD_DIM
    lane = lax.broadcasted_iota(jnp.int32, (Q, W), 1)
    zero = jnp.zeros((), BF16)
    qh = []
    for hd in range(SB_HEADS):
        q2 = q_ref[:, (hd // 2) * W:(hd // 2 + 1) * W]
        qh.append(jnp.where((lane < HEAD_DIM) if hd % 2 == 0 else (lane >= HEAD_DIM), q2, zero))
    row = lax.broadcasted_iota(jnp.int32, (Q, SB_TILE), 0)
    col = lax.broadcasted_iota(jnp.int32, (Q, SB_TILE), 1)
    sums_mat = jnp.concatenate([(row >= col).astype(BF16), jnp.ones((Q, SB_TILE), BF16)], axis=1)
    acc_scr[...] = jnp.zeros(acc_scr.shape, F32)
    c_scr[...] = jnp.zeros(c_scr.shape, F32)

    def tile(kt, diagonal):
        rows = pl.ds(pl.multiple_of(kt * SB_TILE, SB_TILE), SB_TILE)
        zs, ds = [], []
        for hd in range(SB_HEADS):
            z = _dot_nt(qh[hd], k_ref[rows, (hd // 2) * W:(hd // 2 + 1) * W])
            d = jnp.maximum(z, 0.0) + jnp.log(1.0 + jnp.exp(-jnp.abs(z)))
            if diagonal:
                d = jnp.where(col < row, d, 0.0)
            zs.append(z)
            ds.append(d)
        his = [x.astype(BF16) for x in ds]
        los = [(x - h.astype(F32)).astype(BF16) for x, h in zip(ds, his)]
        sums = _dot(jnp.concatenate(his + los, axis=0), sums_mat)
        least = None
        for hd in range(SB_HEADS):
            c_old = c_scr[hd]
            part = sums[hd * Q:(hd + 1) * Q] + sums[(SB_HEADS + hd) * Q:(SB_HEADS + hd + 1) * Q]
            a = jnp.exp(zs[hd] - part[:, 0:SB_TILE] - c_old)
            if diagonal:
                a = jnp.where(col < row, a, 0.0)
            acc_scr[hd] = acc_scr[hd] + _dot(a.astype(BF16), v_ref[rows, (hd // 2) * W:(hd // 2 + 1) * W])
            c_new = c_old + part[:, SB_TILE:2 * SB_TILE]
            c_scr[hd] = c_new
            least = c_new if least is None else jnp.minimum(least, c_new)
        return -jnp.min(least)

    worst0 = tile(i, True)

    def cond(carry):
        kt, worst = carry
        return (kt >= 0) & (worst >= SB_SKIP_LOG)

    def body(carry):
        kt, _ = carry
        return kt - 1, tile(kt, False)

    lax.while_loop(cond, body, (i - 1, worst0))
    for pr in range(SB_HEADS // 2):
        o_ref[:, pr * W:(pr + 1) * W] = jnp.where(lane < HEAD_DIM, acc_scr[2 * pr], acc_scr[2 * pr + 1])


def _stick_breaking(sbq, sbk, sbv):
    T = sbq.shape[0]
    nb = T // Q_BLOCK
    W = 2 * HEAD_DIM
    return pl.pallas_call(
        _sb_kernel,
        grid=(nb,),
        in_specs=[pl.BlockSpec((Q_BLOCK, N_SBH), lambda i: (i, 0)),
                  pl.BlockSpec((T, N_SBH), lambda i: (0, 0)),
                  pl.BlockSpec((T, N_SBH), lambda i: (0, 0))],
        out_specs=pl.BlockSpec((Q_BLOCK, N_SBH), lambda i: (i, 0)),
        out_shape=jax.ShapeDtypeStruct((T, N_SBH), F32),
        scratch_shapes=[pltpu.VMEM((SB_HEADS, Q_BLOCK, W), F32), pltpu.VMEM((SB_HEADS, Q_BLOCK, SB_TILE), F32)],
        compiler_params=pltpu.CompilerParams(dimension_semantics=("arbitrary",),
                                             vmem_limit_bytes=VMEM_LIMIT),
    )(sbq, sbk, sbv)


def _mix_kernel(x_ref, on_ref, os_ref, nwn_ref, nws_ref, wo_ref, fw_ref, rhi_ref, rlo_ref, rb_ref,
                x1_ref, h2_ref, lg_ref, lgT_ref):
    n1 = _rms(on_ref[...], nwn_ref[...]).astype(BF16)
    n2 = _rms(os_ref[...], nws_ref[...]).astype(BF16)
    x1 = x_ref[...] + _dot(n1, wo_ref[0:N_Q]) + _dot(n2, wo_ref[N_Q:N_Q + N_SBH])
    x1_ref[...] = x1
    h2 = _rms(x1, fw_ref[...])
    hi = h2.astype(BF16)
    lo = (h2 - hi.astype(F32)).astype(BF16)
    h2_ref[...] = hi
    lg = _dot(hi, rhi_ref[...]) + _dot(hi, rlo_ref[...]) + _dot(lo, rhi_ref[...]) + rb_ref[...]
    lg_ref[...] = lg
    lgT_ref[...] = lg.T[0:ROUTER_ROWS]


def _mix(x, o_nsa, o_sb, nsa_norm_w, sb_norm_w, w_out, ffn_norm_w, rg_w, rg_b, re_w, re_b):
    T, D = x.shape
    R = PROJ_ROWS
    pad = ROUTER_LANES - N_GROUPS - N_EXPERTS
    wr = jnp.concatenate([rg_w, re_w, jnp.zeros((D, pad), F32)], axis=1)
    wr_hi = wr.astype(BF16)
    wr_lo = (wr - wr_hi.astype(F32)).astype(BF16)
    rb = jnp.concatenate([rg_b, re_b, jnp.zeros((pad,), F32)]).reshape(1, ROUTER_LANES)
    full = lambda shape: pl.BlockSpec(shape, lambda i: (0,) * len(shape))
    rows = lambda n: pl.BlockSpec((R, n), lambda i: (i, 0))
    return pl.pallas_call(
        _mix_kernel,
        grid=(T // R,),
        in_specs=[rows(D), rows(N_Q), rows(N_SBH), full((1, N_Q)), full((1, N_SBH)), full((N_Q + N_SBH, D)),
                  full((1, D)), full((D, ROUTER_LANES)), full((D, ROUTER_LANES)), full((1, ROUTER_LANES))],
        out_specs=[rows(D), rows(D), rows(ROUTER_LANES), pl.BlockSpec((ROUTER_ROWS, R), lambda i: (0, i))],
        out_shape=[jax.ShapeDtypeStruct((T, D), F32), jax.ShapeDtypeStruct((T, D), BF16),
                   jax.ShapeDtypeStruct((T, ROUTER_LANES), F32), jax.ShapeDtypeStruct((ROUTER_ROWS, T), F32)],
        compiler_params=pltpu.CompilerParams(dimension_semantics=("arbitrary",),
                                             vmem_limit_bytes=VMEM_LIMIT),
    )(x, o_nsa, o_sb, nsa_norm_w.reshape(1, N_Q), sb_norm_w.reshape(1, N_SBH), w_out.astype(BF16),
      ffn_norm_w.reshape(1, D), wr_hi, wr_lo, rb)


def _routing(lg, axis):
    pos_i = lax.broadcasted_iota(jnp.int32, lg.shape, axis)
    pos = pos_i.astype(F32)
    first_max = lambda v, mx: jnp.min(jnp.where(v == mx, pos, float(ROUTER_LANES)), axis=axis, keepdims=True)
    gl = jnp.where(pos_i < N_GROUPS, lg, -jnp.inf)
    gmax = jnp.max(gl, axis=axis, keepdims=True)
    grp = first_max(gl, gmax)
    g_gate = 1.0 / jnp.sum(jnp.exp(gl - gmax), axis=axis, keepdims=True)
    e_idx = pos_i - N_GROUPS
    e_grp = (e_idx >> 2).astype(F32)
    in_grp = (e_idx >= 0) & (e_idx < N_EXPERTS) & (e_grp == grp)
    el = jnp.where(in_grp, lg, -jnp.inf)
    top1 = jnp.max(el, axis=axis, keepdims=True)
    i1 = first_max(el, top1)
    el2 = jnp.where(pos == i1, -jnp.inf, el)
    top2 = jnp.max(el2, axis=axis, keepdims=True)
    i2 = first_max(el2, top2)
    e2 = jnp.exp(top2 - top1)
    w1 = 1.0 / (1.0 + e2)
    w2 = e2 / (1.0 + e2)
    weight = g_gate * (jnp.where(pos == i1, w1, 0.0) + jnp.where(pos == i2, w2, 0.0))
    routed = jnp.where(pos == i1, 1.0, 0.0) + jnp.where(pos == i2, 1.0, 0.0)
    return weight, routed


def _moe_kernel(h_ref, lg_ref, lgT_ref, x1_ref, before_ref, beforeT_ref, wg_ref, wu_ref, wd_ref, fw_ref, o_ref,
                acc_scr, rank_scr, cw_scr, rankT_scr):
    e = pl.program_id(1)
    rows = h_ref.shape[0]

    @pl.when(e == 0)
    def _():
        acc_scr[...] = jnp.zeros(acc_scr.shape, F32)
        weight, routed = _routing(lg_ref[...], 1)
        rank = _dot(beforeT_ref[...], routed.astype(BF16))
        rank_scr[...] = jnp.where(routed > 0.0, rank, -1.0)
        cw_scr[...] = weight
        _, routed_t = _routing(lgT_ref[...], 0)
        rank_t = _dot(routed_t.astype(BF16), before_ref[...])
        rankT_scr[...] = jnp.where(routed_t > 0.0, rank_t, -1.0)

    lane = lax.broadcasted_iota(jnp.int32, (rows, ROUTER_LANES), 1)
    mine = lane == e + N_GROUPS
    rank_col = jnp.sum(jnp.where(mine, rank_scr[...], 0.0), axis=1, keepdims=True)
    w_col = jnp.sum(jnp.where(mine, cw_scr[...], 0.0), axis=1, keepdims=True)
    rank_row = rankT_scr[pl.ds(e + N_GROUPS, 1), :]
    n_routed = jnp.max(rank_row).astype(jnp.int32) + 1
    slot_col = lax.broadcasted_iota(jnp.int32, (MOE_CAP, 1), 0)
    slot_row = lax.broadcasted_iota(jnp.int32, (1, MOE_CAP), 1)

    def chunk(ch, carry):
        base = ch * MOE_CAP
        gather = jnp.where(rank_row == (slot_col + base).astype(F32), 1.0, 0.0).astype(BF16)
        scatter = jnp.where(rank_col == (slot_row + base).astype(F32), 1.0, 0.0).astype(BF16)
        xg = _dot(gather, h_ref[...]).astype(BF16)
        a = _dot(xg, wg_ref[0])
        b = _dot(xg, wu_ref[0])
        act = (a * jax.nn.sigmoid(a) * b).astype(BF16)
        y = _dot(act, wd_ref[0])
        acc_scr[...] += _dot(scatter, y.astype(BF16)) * w_col
        return carry

    lax.fori_loop(0, (n_routed + (MOE_CAP - 1)) // MOE_CAP, chunk, 0)

    @pl.when(e == N_EXPERTS - 1)
    def _():
        o_ref[...] = _rms(x1_ref[...] + acc_scr[...], fw_ref[...])


def _moe(h2, logits, logitsT, x1, w_gate, w_up, w_down, final_norm_w):
    T, D = x1.shape
    R = min(MOE_ROWS, T)
    before = jnp.triu(jnp.ones((R, R), BF16), k=1)
    rows = lambda n: pl.BlockSpec((R, n), lambda i, e: (i, 0))
    const = lambda a, b: pl.BlockSpec((a, b), lambda i, e: (0, 0))
    per_expert = lambda a, b: pl.BlockSpec((1, a, b), lambda i, e: (e, 0, 0))
    return pl.pallas_call(
        _moe_kernel,
        grid=(T // R, N_EXPERTS),
        in_specs=[rows(D), rows(ROUTER_LANES), pl.BlockSpec((ROUTER_ROWS, R), lambda i, e: (0, i)), rows(D),
                  const(R, R), const(R, R),
                  per_expert(D, EXPERT_FF), per_expert(D, EXPERT_FF), per_expert(EXPERT_FF, D), const(1, D)],
        out_specs=rows(D),
        out_shape=jax.ShapeDtypeStruct((T, D), F32),
        scratch_shapes=[pltpu.VMEM((R, D), F32), pltpu.VMEM((R, ROUTER_LANES), F32),
                        pltpu.VMEM((R, ROUTER_LANES), F32), pltpu.VMEM((ROUTER_ROWS, R), F32)],
        compiler_params=pltpu.CompilerParams(dimension_semantics=("arbitrary", "arbitrary"),
                                             vmem_limit_bytes=VMEM_LIMIT),
    )(h2, logits, logitsT, x1, before, before.T, w_gate.astype(BF16), w_up.astype(BF16), w_down.astype(BF16),
      final_norm_w.reshape(1, D))


def kernel(x, positions, attn_norm_w, w_in, cmp_pe_k, cmp_pe_v, cmp_k_w1, cmp_k_w2, cmp_v_w1, cmp_v_w2,
           nsa_out_norm_w, sb_out_norm_w, w_out, ffn_norm_w, router_group_w, router_group_b,
           router_expert_w, router_expert_b, w_gate, w_up, w_down, final_norm_w):
    B, T, D = x.shape
    assert B == 1 and T % SEL_TILE == 0 and T % PROJ_ROWS == 0 and T // SEL_LEN >= SEL_TOP
    assert attn_norm_w.shape[0] == 1, "the final norm is fused into the (single) layer's MoE kernel"
    xs = x.reshape(T, D)
    pos = positions.reshape(T)
    (cmpk, cmpv, sbq, sbk, sbv, qT, ksel, kwin, vselT, vwinT, gT) = _project(xs, pos, attn_norm_w[0], w_in[0])
    kc, vcT = _compress(cmpk, cmpv, cmp_pe_k[0], cmp_pe_v[0], cmp_k_w1[0], cmp_k_w2[0], cmp_v_w1[0], cmp_v_w2[0])
    o_nsa = _nsa(qT, kc, vcT, ksel, vselT, kwin, vwinT, gT)
    o_sb = _stick_breaking(sbq, sbk, sbv)
    x1, h2, logits, logitsT = _mix(xs, o_nsa, o_sb, nsa_out_norm_w[0], sb_out_norm_w[0], w_out[0], ffn_norm_w[0],
                                   router_group_w[0], router_group_b[0], router_expert_w[0], router_expert_b[0])
    out = _moe(h2, logits, logitsT, x1, w_gate[0], w_up[0], w_down[0], final_norm_w)
    return out.reshape(B, T, D)
```

```python
import functools

import jax
import jax.numpy as jnp
from jax import lax
from jax.experimental import pallas as pl
from jax.experimental.pallas import tpu as pltpu

HEAD_DIM = 64
NSA_HEADS = 8
NSA_KV_HEADS = 2
NSA_GROUP = NSA_HEADS // NSA_KV_HEADS
SB_HEADS = 8
ROPE_THETA = 500000.0
ROPE_DIM = HEAD_DIM // 4
ROPE_HALF = ROPE_DIM // 2
CMP_LEN = 32
CMP_STRIDE = 16
CMP_HIDDEN = 256
SEL_LEN = 64
SEL_TOP = 16
SEL_BONUS = 1.0e4
WINDOW = 512
Q_BLOCK = 128
N_GROUPS = 4
EXPERTS_PER_GROUP = 4
N_EXPERTS = N_GROUPS * EXPERTS_PER_GROUP
EXPERT_FF = 512
EPS = 1e-6
NEG = -1e30
LOWEST = -3.0e38

N_Q = NSA_HEADS * HEAD_DIM
N_KVG = NSA_KV_HEADS * HEAD_DIM
N_GATE = NSA_HEADS * 3
N_GATE_PAD = 32
N_SBH = SB_HEADS * HEAD_DIM
SCALE = HEAD_DIM ** -0.5
LOG2E = 1.4426950408889634
SEL_BLOCKS_PER_TILE = 8
BIAS_ROWS = 16
CMP_CLASSES = 8
SEL_V_ROWS = HEAD_DIM + 16

PROJ_ROWS = 512
SEL_TILE = 512
WIN_TILES = WINDOW // Q_BLOCK + 1
SB_TILE = 128
MOE_ROWS = 1024
MOE_CAP = 256
ROUTER_LANES = 128
ROUTER_ROWS = 32
SB_SKIP_LOG = -104.0

VMEM_LIMIT = 56 * 1024 * 1024

BF16 = jnp.bfloat16
F32 = jnp.float32


def _rms(x, w):
    return x * lax.rsqrt(jnp.mean(x * x, axis=-1, keepdims=True) + EPS) * w


def _dot(a, b):
    return jnp.dot(a, b, preferred_element_type=F32)


def _dot_nt(a, b):
    return lax.dot_general(a, b, (((1,), (1,)), ((), ())), preferred_element_type=F32)


def _split3(x):
    hi = x.astype(BF16)
    r1 = x - hi.astype(F32)
    mid = r1.astype(BF16)
    lo = (r1 - mid.astype(F32)).astype(BF16)
    return hi, mid, lo


def _proj_kernel(x_ref, nw_ref, wn_ref, wt_ref, pos_ref, invf_ref,
                 cmpk_ref, cmpv_ref, sbqT_ref, sbk_ref, sbvT_ref,
                 qT_ref, ksel_ref, kwin_ref, vselT_ref, vwinT_ref, gT_ref):
    h = _rms(x_ref[...], nw_ref[...]).astype(BF16)
    p1 = _dot(h, wn_ref[...])
    cmpk_ref[...] = p1[:, 0:N_KVG]
    cmpv_ref[...] = p1[:, N_KVG:2 * N_KVG]
    sbk_ref[...] = p1[:, 2 * N_KVG:2 * N_KVG + N_SBH].astype(BF16)

    p2 = _dot_nt(wt_ref[...], h)
    o = N_Q + 4 * N_KVG + N_GATE_PAD
    sbqT_ref[...] = (p2[o:o + N_SBH] * SCALE).astype(BF16)
    sbv = p2[o + N_SBH:o + 2 * N_SBH].astype(BF16)
    for j in range(PROJ_ROWS // Q_BLOCK):
        sbvT_ref[j] = sbv[:, j * Q_BLOCK:(j + 1) * Q_BLOCK]
    ang = invf_ref[...] * pos_ref[...].astype(F32)
    cos, sin = jnp.cos(ang), jnp.sin(ang)
    n_rope_heads = NSA_HEADS + 2 * NSA_KV_HEADS
    roped = []
    for hd in range(n_rope_heads):
        blk = p2[hd * HEAD_DIM:(hd + 1) * HEAD_DIM]
        x1, x2 = blk[0:ROPE_HALF], blk[ROPE_HALF:ROPE_DIM]
        roped.append(jnp.concatenate(
            [x1 * cos - x2 * sin, x2 * cos + x1 * sin, blk[ROPE_DIM:]], axis=0))
    qT_ref[...] = (jnp.concatenate(roped[:NSA_HEADS], axis=0) * (SCALE * LOG2E)).astype(BF16)
    kT = jnp.concatenate(roped[NSA_HEADS:], axis=0)
    kn = kT.T.astype(BF16)
    r_blk = lax.broadcasted_iota(jnp.int32, (PROJ_ROWS, HEAD_DIM), 0) >> 6
    c_idx = lax.broadcasted_iota(jnp.int32, (PROJ_ROWS, HEAD_DIM), 1)
    onehot = jnp.where(r_blk == c_idx, 1.0, 0.0).astype(BF16)
    for gk in range(NSA_KV_HEADS):
        ksel_ref[gk] = jnp.concatenate([kn[:, gk * HEAD_DIM:(gk + 1) * HEAD_DIM], onehot], axis=1)
    kwin_ref[...] = kn[:, N_KVG:2 * N_KVG]
    o = N_Q + 2 * N_KVG
    for gk in range(NSA_KV_HEADS):
        vselT_ref[0, gk] = jnp.concatenate(
            [p2[o + gk * HEAD_DIM:o + (gk + 1) * HEAD_DIM],
             jnp.where(lax.broadcasted_iota(jnp.int32, (SEL_V_ROWS - HEAD_DIM, PROJ_ROWS), 0) == 0, 1.0, 0.0)],
            axis=0).astype(BF16)
    vw = p2[o + N_KVG:o + 2 * N_KVG].astype(BF16)
    for j in range(PROJ_ROWS // Q_BLOCK):
        vwinT_ref[j] = vw[:, j * Q_BLOCK:(j + 1) * Q_BLOCK]
    o = o + 2 * N_KVG
    gT_ref[...] = jax.nn.sigmoid(p2[o:o + N_GATE_PAD])


def _project(x, positions, attn_norm_w, w_in):
    T, D = x.shape
    R = PROJ_ROWS
    o1, o2, o3 = N_Q, N_Q + 6 * N_KVG, N_Q + 6 * N_KVG + N_GATE
    kv = lambda i: w_in[:, o1 + i * N_KVG:o1 + (i + 1) * N_KVG]
    sb = lambda i: w_in[:, o3 + i * N_SBH:o3 + (i + 1) * N_SBH]
    w_nat = jnp.concatenate([kv(0), kv(1), sb(1)], axis=1).astype(BF16)
    w_t = jnp.concatenate(
        [w_in[:, :o1], kv(2), kv(4), kv(3), kv(5), w_in[:, o2:o3],
         jnp.zeros((D, N_GATE_PAD - N_GATE), w_in.dtype), sb(0), sb(2)], axis=1).T.astype(BF16)
    inv_freq = ROPE_THETA ** (-jnp.arange(0, ROPE_DIM, 2, dtype=F32) / ROPE_DIM)
    n_nat, n_t = w_nat.shape[1], w_t.shape[0]
    full = lambda shape: pl.BlockSpec(shape, lambda i: (0,) * len(shape))
    rows = lambda n: pl.BlockSpec((R, n), lambda i: (i, 0))
    cols = lambda n: pl.BlockSpec((n, R), lambda i: (0, i))
    return pl.pallas_call(
        _proj_kernel,
        grid=(T // R,),
        in_specs=[rows(D), full((1, D)), full((D, n_nat)), full((n_t, D)), cols(1), full((ROPE_HALF, 1))],
        out_specs=[rows(N_KVG), rows(N_KVG), cols(N_SBH), rows(N_SBH),
                   pl.BlockSpec((R // Q_BLOCK, N_SBH, Q_BLOCK), lambda i: (i, 0, 0)),
                   cols(N_Q), pl.BlockSpec((NSA_KV_HEADS, R, N_KVG), lambda i: (0, i, 0)), rows(N_KVG),
                   pl.BlockSpec((1, NSA_KV_HEADS, SEL_V_ROWS, R), lambda i: (i, 0, 0, 0)),
                   pl.BlockSpec((R // Q_BLOCK, N_KVG, Q_BLOCK), lambda i: (i, 0, 0)),
                   cols(N_GATE_PAD)],
        out_shape=[jax.ShapeDtypeStruct((T, N_KVG), F32), jax.ShapeDtypeStruct((T, N_KVG), F32),
                   jax.ShapeDtypeStruct((N_SBH, T), BF16), jax.ShapeDtypeStruct((T, N_SBH), BF16),
                   jax.ShapeDtypeStruct((T // Q_BLOCK, N_SBH, Q_BLOCK), BF16),
                   jax.ShapeDtypeStruct((N_Q, T), BF16),
                   jax.ShapeDtypeStruct((NSA_KV_HEADS, T, N_KVG), BF16), jax.ShapeDtypeStruct((T, N_KVG), BF16),
                   jax.ShapeDtypeStruct((T // R, NSA_KV_HEADS, SEL_V_ROWS, R), BF16),
                   jax.ShapeDtypeStruct((T // Q_BLOCK, N_KVG, Q_BLOCK), BF16),
                   jax.ShapeDtypeStruct((N_GATE_PAD, T), F32)],
        compiler_params=pltpu.CompilerParams(dimension_semantics=("arbitrary",),
                                             vmem_limit_bytes=VMEM_LIMIT),
    )(x, attn_norm_w.reshape(1, D), w_nat, w_t, positions.reshape(1, T), inv_freq.reshape(ROPE_HALF, 1))


def _compress_kernel(x_ref, pea_ref, peb_ref, wa_ref, wb_ref, w2_ref, nat_ref, tr_ref):
    x = x_ref[0]
    nc = x.shape[0]
    ha = _dot((x + pea_ref[0]).astype(BF16), wa_ref[0])
    hb = _dot((x + peb_ref[0]).astype(BF16), wb_ref[0])
    hid = ha + pltpu.roll(hb, nc - 1, 0)
    act = (hid * jax.nn.sigmoid(hid)).astype(BF16)
    out = _dot(act, w2_ref[0])
    nat_ref[0] = out.astype(BF16)
    tr_ref[0] = out.T.astype(BF16)


def _compress(cmpk, cmpv, pe_k, pe_v, k_w1, k_w2, v_w1, v_w2):
    T = cmpk.shape[0]
    nc = T // CMP_STRIDE
    half = CMP_LEN // 2
    G = NSA_KV_HEADS
    width = half * N_KVG
    x = jnp.stack([cmpk.reshape(nc, width), cmpv.reshape(nc, width)])
    eye = jnp.eye(G, dtype=F32)

    def pe_rows(pe):
        return jnp.broadcast_to(pe[:, None, :], (half, G, HEAD_DIM)).reshape(1, width)

    def w1_block(w1):
        w = w1.reshape(half, HEAD_DIM, CMP_HIDDEN)
        return jnp.einsum('ldj,gh->lgdhj', w, eye).reshape(width, G * CMP_HIDDEN).astype(BF16)

    def w2_block(w2):
        return jnp.einsum('jd,gh->gjhd', w2, eye).reshape(G * CMP_HIDDEN, N_KVG).astype(BF16)

    hw = half * HEAD_DIM
    pea = jnp.stack([pe_rows(pe_k[:half]), pe_rows(pe_v[:half])])
    peb = jnp.stack([pe_rows(pe_k[half:]), pe_rows(pe_v[half:])])
    wa = jnp.stack([w1_block(k_w1[:hw]), w1_block(v_w1[:hw])])
    wb = jnp.stack([w1_block(k_w1[hw:]), w1_block(v_w1[hw:])])
    w2 = jnp.stack([w2_block(k_w2), w2_block(v_w2)])
    blk = lambda a, b: pl.BlockSpec((1, a, b), lambda i: (i, 0, 0))
    nat, tr = pl.pallas_call(
        _compress_kernel,
        grid=(2,),
        in_specs=[blk(nc, width), blk(1, width), blk(1, width), blk(width, G * CMP_HIDDEN),
                  blk(width, G * CMP_HIDDEN), blk(G * CMP_HIDDEN, N_KVG)],
        out_specs=[blk(nc, N_KVG), blk(N_KVG, nc)],
        out_shape=[jax.ShapeDtypeStruct((2, nc, N_KVG), BF16), jax.ShapeDtypeStruct((2, N_KVG, nc), BF16)],
        compiler_params=pltpu.CompilerParams(dimension_semantics=("arbitrary",),
                                             vmem_limit_bytes=VMEM_LIMIT),
    )(x, pea, peb, wa, wb, w2)
    return nat[0], tr[1]


def _nsa_kernel(q_ref, kc_ref, vcT_ref, ovT_ref, ksel_ref, vselT_ref, *rest):
    kwin_refs = rest[0:WIN_TILES]
    vwin_refs = rest[WIN_TILES:2 * WIN_TILES]
    gT_ref, o_ref = rest[2 * WIN_TILES:2 * WIN_TILES + 2]
    scratch = rest[2 * WIN_TILES + 2:]
    G = NSA_KV_HEADS
    per = len(scratch) // G
    bias_scr, s0_scr, s1_scr, t0_scr, t1_scr, m_scr, acc_scr, oc_scr = (
        [scratch[g * per + n] for g in range(G)] for n in range(per))
    groups = range(G)
    i = pl.program_id(0)
    R, Q = NSA_GROUP, Q_BLOCK
    L = R * Q
    nsel = bias_scr[0].shape[0]

    zero = jnp.zeros((HEAD_DIM, L), BF16)
    qg, qz = [], []
    for g in groups:
        q = jnp.concatenate([q_ref[(g * R + r) * HEAD_DIM:(g * R + r + 1) * HEAD_DIM, :] for r in range(R)], axis=1)
        qg.append(q)
        qz.append(jnp.concatenate([q if gg == g else zero for gg in groups], axis=0))

    def own_rows(x, g):
        return x[g * HEAD_DIM:(g + 1) * HEAD_DIM]

    t_lane = i * Q + (lax.broadcasted_iota(jnp.int32, (1, L), 1) & (Q - 1))

    nc = kc_ref.shape[0]
    q_pos = i * Q + lax.broadcasted_iota(jnp.int32, (1, Q), 1)
    bq = (q_pos >> 6).astype(F32)

    def compress_and_select(n_eff, first_class):
        m_eff = n_eff // (SEL_LEN // CMP_STRIDE)
        cmp_end = lax.broadcasted_iota(jnp.int32, (n_eff, 1), 0) * CMP_STRIDE + (CMP_LEN - 1)
        visible = cmp_end <= t_lane
        sc = [_dot(kc_ref[0:n_eff, :], qz[g]) for g in groups]
        pc = []
        for g in groups:
            s = jnp.where(visible, sc[g], NEG)
            mxc = jnp.max(s, axis=0, keepdims=True)
            mxc = jnp.where(mxc < 0.5 * NEG, 0.0, mxc)
            ec = jnp.exp2(s - mxc)
            pc.append(ec * (1.0 / jnp.maximum(jnp.sum(ec, axis=0, keepdims=True), 1e-30)))
        for g in groups:
            oc_scr[g][...] = own_rows(_dot(vcT_ref[:, 0:n_eff], pc[g].astype(BF16)), g)
        ov = ovT_ref[0:m_eff, 0:n_eff]
        imp = []
        for g in groups:
            psum = pc[g][:, 0:Q]
            for r in range(1, R):
                psum = psum + pc[g][:, r * Q:(r + 1) * Q]
            imp.append(sum(_dot(ov, part) for part in _split3(psum)))
        m_idx = lax.broadcasted_iota(jnp.int32, (m_eff, Q), 0).astype(F32)
        allowed = m_idx <= bq
        forced = (m_idx == 0.0) | (m_idx == bq) | (m_idx == bq - 1.0)
        if first_class:
            score = [jnp.where(allowed, imp[g] + jnp.where(forced, SEL_BONUS, 0.0), NEG) for g in groups]
            n_pick = min(SEL_TOP, m_eff)
        else:
            free = allowed & jnp.logical_not(forced)
            score = [jnp.where(free, imp[g], NEG) for g in groups]
            n_pick = SEL_TOP - 3
        for _ in range(n_pick):
            for g in groups:
                best = jnp.max(score[g], axis=0, keepdims=True)
                first = jnp.min(jnp.where(score[g] == best, m_idx, float(m_eff)), axis=0, keepdims=True)
                score[g] = jnp.where(m_idx == first, LOWEST, score[g])
        for g in groups:
            picked = score[g] < 0.5 * LOWEST
            if not first_class:
                picked = picked | forced
            bias = jnp.where(allowed & picked, 0.0, NEG)
            bias_scr[g][0:m_eff, :] = jnp.concatenate([bias] * R, axis=1)
            if m_eff < nsel:
                bias_scr[g][m_eff:nsel, :] = jnp.full((nsel - m_eff, L), NEG, F32)

    sizes = sorted({max(Q, (nc * k // CMP_CLASSES) // Q * Q) for k in range(1, CMP_CLASSES + 1)})
    lo = 0
    for n_eff in sizes:
        hi = n_eff // (Q // CMP_STRIDE)
        pl.when((i >= lo) & (i < hi))(functools.partial(compress_and_select, n_eff, lo == 0))
        lo = hi

    for g in groups:
        m_scr[g][...] = jnp.full(m_scr[g].shape, NEG, F32)
        acc_scr[g][...] = jnp.zeros(acc_scr[g].shape, F32)
    rhs_pad = jnp.zeros((N_KVG - HEAD_DIM - BIAS_ROWS, L), BF16)
    bias_pad = jnp.zeros((BIAS_ROWS - SEL_BLOCKS_PER_TILE, L), F32)
    buf0, buf1 = (s0_scr, t0_scr), (s1_scr, t1_scr)

    def scores(kt, dst):
        for g in groups:
            brows = bias_scr[g][pl.ds(pl.multiple_of(kt * SEL_BLOCKS_PER_TILE, SEL_BLOCKS_PER_TILE),
                                      SEL_BLOCKS_PER_TILE), :]
            rhs = jnp.concatenate([qg[g], jnp.concatenate([brows, bias_pad], axis=0).astype(BF16), rhs_pad], axis=0)
            s = _dot(ksel_ref[g, kt], rhs)
            dst[0][g][...] = s
            dst[1][g][...] = jnp.max(s, axis=0, keepdims=True)

    def absorb(kt, src):
        for g in groups:
            m_old = m_scr[g][...]
            m_new = jnp.maximum(m_old, src[1][g][...])
            alpha = jnp.exp2(m_old - m_new)
            p = jnp.exp2(src[0][g][...] - m_new)
            acc_scr[g][...] = alpha * acc_scr[g][...] + _dot(vselT_ref[kt, g], p.astype(BF16))
            m_scr[g][...] = m_new

    def step(kt, src, dst):
        scores(kt + 1, dst)
        absorb(kt, src)

    def two_steps(kt):
        step(kt, buf0, buf1)
        step(kt + 1, buf1, buf0)

    k_row = lax.broadcasted_iota(jnp.int32, (Q, 1), 0)
    q_lane = lax.broadcasted_iota(jnp.int32, (1, L), 1) & (Q - 1)
    n_full = (i * Q) // SEL_TILE

    def last_tile(src):
        r0 = pl.multiple_of(i * Q - n_full * SEL_TILE, Q)
        for g in groups:
            src[0][g][pl.ds(r0, Q), :] = jnp.where(k_row <= q_lane, src[0][g][pl.ds(r0, Q), :], NEG)
            src[1][g][...] = jnp.max(src[0][g][...], axis=0, keepdims=True)
        absorb(n_full, src)

    scores(0, buf0)

    def four_steps(j, carry):
        two_steps(4 * j)
        two_steps(4 * j + 2)
        return carry

    lax.fori_loop(0, n_full // 4, four_steps, 0)

    @pl.when((n_full & 2) != 0)
    def _():
        two_steps(n_full & -4)

    @pl.when((n_full & 1) == 0)
    def _():
        last_tile(buf0)

    @pl.when((n_full & 1) == 1)
    def _():
        step(n_full - 1, buf0, buf1)
        last_tile(buf1)

    sw = [[] for _ in groups]
    for w in range(WIN_TILES):
        j = i - (WIN_TILES - 1) + w
        for g in groups:
            s = _dot(kwin_refs[w][0], qz[g])
            if w == 0:
                s = jnp.where(k_row > q_lane, s, NEG)
            if w == WIN_TILES - 1:
                s = jnp.where(k_row <= q_lane, s, NEG)
            else:
                s = jnp.where(j >= 0, s, NEG)
            sw[g].append(s)
    ew, denw = [], []
    for g in groups:
        mxw = functools.reduce(jnp.maximum, [jnp.max(s, axis=0, keepdims=True) for s in sw[g]])
        ew.append([jnp.exp2(s - mxw) for s in sw[g]])
        denw.append(jnp.maximum(sum(jnp.sum(e, axis=0, keepdims=True) for e in ew[g]), 1e-30))
    owT = []
    for g in groups:
        prod = sum(_dot(vwin_refs[w][0], ew[g][w].astype(BF16)) for w in range(WIN_TILES))
        owT.append(own_rows(prod, g) * (1.0 / denw[g]))

    def gate_row(g, j):
        return jnp.concatenate([gT_ref[(g * R + r) * 3 + j:(g * R + r) * 3 + j + 1, :] for r in range(R)], axis=1)

    o_rows = []
    for g in groups:
        osT = acc_scr[g][0:HEAD_DIM, :] * (1.0 / acc_scr[g][HEAD_DIM:HEAD_DIM + 1, :])
        oT = gate_row(g, 0) * oc_scr[g][...] + gate_row(g, 1) * osT + gate_row(g, 2) * owT[g]
        o_rows += [oT[:, r * Q:(r + 1) * Q] for r in range(R)]
    o_ref[...] = jnp.concatenate(o_rows, axis=0).T


def _nsa(qT, kc, vcT, ksel, vselT, kwin, vwinT, gT):
    T = qT.shape[1]
    nb = T // Q_BLOCK
    nc = kc.shape[0]
    nsel = T // SEL_LEN
    ntile = T // SEL_TILE
    n = jnp.arange(nc)[None, :] * CMP_STRIDE
    m = jnp.arange(nsel)[:, None] * SEL_LEN
    ovT = ((n < m + SEL_LEN) & (n + CMP_LEN > m)).astype(BF16)
    ksel4 = ksel.reshape(NSA_KV_HEADS, ntile, SEL_TILE, N_KVG)
    kwin3 = kwin.reshape(nb, Q_BLOCK, N_KVG)
    L = NSA_GROUP * Q_BLOCK
    const = lambda shape: pl.BlockSpec(shape, lambda i: (0,) * len(shape))
    win_tile = lambda i, w: (jnp.maximum(i - (WIN_TILES - 1) + w, 0), 0, 0)
    kwin_specs = [pl.BlockSpec((1, Q_BLOCK, N_KVG), functools.partial(win_tile, w=w)) for w in range(WIN_TILES)]
    vwin_specs = [pl.BlockSpec((1, N_KVG, Q_BLOCK), functools.partial(win_tile, w=w)) for w in range(WIN_TILES)]
    group_scratch = [pltpu.VMEM((nsel, L), F32), pltpu.VMEM((SEL_TILE, L), F32), pltpu.VMEM((SEL_TILE, L), F32),
                     pltpu.VMEM((1, L), F32), pltpu.VMEM((1, L), F32),
                     pltpu.VMEM((1, L), F32), pltpu.VMEM((SEL_V_ROWS, L), F32), pltpu.VMEM((HEAD_DIM, L), F32)]
    return pl.pallas_call(
        _nsa_kernel,
        grid=(nb,),
        in_specs=[pl.BlockSpec((N_Q, Q_BLOCK), lambda i: (0, i)),
                  const((nc, N_KVG)), const((N_KVG, nc)), const((nsel, nc)),
                  const((NSA_KV_HEADS, ntile, SEL_TILE, N_KVG)), const((ntile, NSA_KV_HEADS, SEL_V_ROWS, SEL_TILE))]
                 + kwin_specs + vwin_specs
                 + [pl.BlockSpec((N_GATE_PAD, Q_BLOCK), lambda i: (0, i))],
        out_specs=pl.BlockSpec((Q_BLOCK, N_Q), lambda i: (i, 0)),
        out_shape=jax.ShapeDtypeStruct((T, N_Q), F32),
        scratch_shapes=group_scratch * NSA_KV_HEADS,
        compiler_params=pltpu.CompilerParams(dimension_semantics=("arbitrary",),
                                             vmem_limit_bytes=VMEM_LIMIT),
    )(qT, kc, vcT, ovT, ksel4, vselT, *([kwin3] * WIN_TILES), *([vwinT] * WIN_TILES), gT)


def _sb_kernel(qT_ref, k_ref, vT_ref, o_ref, *scr):
    i = pl.program_id(0)
    Q, W = Q_BLOCK, 2 * HEAD_DIM
    pairs = SB_HEADS // 2
    blk_row = lax.broadcasted_iota(jnp.int32, (W, 2 * Q), 0) < HEAD_DIM
    blk_lane = lax.broadcasted_iota(jnp.int32, (W, 2 * Q), 1) < Q
    zero = jnp.zeros((), BF16)
    q_pairs = []
    for pr in range(pairs):
        x = qT_ref[pr * W:(pr + 1) * W, :]
        q_pairs.append(jnp.where(blk_row == blk_lane, jnp.concatenate([x, x], axis=1), zero))
    k_row = lax.broadcasted_iota(jnp.int32, (SB_TILE, 1), 0)
    q_lane = lax.broadcasted_iota(jnp.int32, (1, SB_HEADS * Q), 1) & (Q - 1)
    rr = lax.broadcasted_iota(jnp.int32, (SB_TILE, SB_TILE), 0)
    cc = lax.broadcasted_iota(jnp.int32, (SB_TILE, SB_TILE), 1)
    from_here = (cc >= rr).astype(BF16)
    acc_scrs, c_scrs = scr[0:pairs], scr[pairs:pairs + 2]
    for buf in scr:
        buf[...] = jnp.zeros(buf.shape, F32)

    def tile(kt, diagonal):
        rows = pl.ds(pl.multiple_of(kt * SB_TILE, SB_TILE), SB_TILE)
        halves = ((0, 1), (2, 3))
        valid = k_row < q_lane[:, 0:4 * Q]
        z = [jnp.concatenate([_dot(k_ref[rows, pr * W:(pr + 1) * W], q_pairs[pr]) for pr in prs], axis=1)
             for prs in halves]
        d, suffix, c_old, a = [None, None], [None, None], [None, None], [None, None]
        for hf in range(2):
            d[hf] = jnp.maximum(z[hf], 0.0) + jnp.log(1.0 + jnp.exp(-jnp.abs(z[hf])))
            if diagonal:
                d[hf] = jnp.where(valid, d[hf], 0.0)
            hi = d[hf].astype(BF16)
            lo = (d[hf] - hi.astype(F32)).astype(BF16)
            suffix[hf] = _dot(from_here, hi) + _dot(from_here, lo)
        least = None
        for hf in range(2):
            c_old = c_scrs[hf][...]
            a = jnp.exp(z[hf] - suffix[hf] - c_old)
            if diagonal:
                a = jnp.where(valid, a, 0.0)
            a = a.astype(BF16)
            for n, pr in enumerate(halves[hf]):
                acc_scrs[pr][...] = acc_scrs[pr][...] + _dot(vT_ref[kt, pr * W:(pr + 1) * W, :],
                                                             a[:, n * 2 * Q:(n + 1) * 2 * Q])
            c_new = c_old + jnp.sum(d[hf], axis=0, keepdims=True)
            c_scrs[hf][...] = c_new
            least = c_new if least is None else jnp.minimum(least, c_new)
        return -jnp.min(least)

    worst0 = tile(i, True)

    def cond(carry):
        kt, worst = carry
        return (kt >= 0) & (worst >= SB_SKIP_LOG)

    def body(carry):
        kt, _ = carry
        return kt - 1, tile(kt, False)

    lax.while_loop(cond, body, (i - 1, worst0))
    for pr in range(pairs):
        acc = acc_scrs[pr][...]
        o_ref[:, pr * W:(pr + 1) * W] = jnp.where(blk_row[:, 0:Q], acc[:, 0:Q], acc[:, Q:2 * Q]).T


def _stick_breaking(sbqT, sbk, sbvT):
    T = sbk.shape[0]
    nb = T // Q_BLOCK
    W = 2 * HEAD_DIM
    return pl.pallas_call(
        _sb_kernel,
        grid=(nb,),
        in_specs=[pl.BlockSpec((N_SBH, Q_BLOCK), lambda i: (0, i)),
                  pl.BlockSpec((T, N_SBH), lambda i: (0, 0)),
                  pl.BlockSpec((nb, N_SBH, Q_BLOCK), lambda i: (0, 0, 0))],
        out_specs=pl.BlockSpec((Q_BLOCK, N_SBH), lambda i: (i, 0)),
        out_shape=jax.ShapeDtypeStruct((T, N_SBH), F32),
        scratch_shapes=[pltpu.VMEM((W, 2 * Q_BLOCK), F32)] * (SB_HEADS // 2) + [pltpu.VMEM((1, 4 * Q_BLOCK), F32)] * 2,
        compiler_params=pltpu.CompilerParams(dimension_semantics=("arbitrary",),
                                             vmem_limit_bytes=VMEM_LIMIT),
    )(sbqT, sbk, sbvT)


def _mix_kernel(x_ref, on_ref, os_ref, nwn_ref, nws_ref, wo_ref, fw_ref, rhi_ref, rlo_ref, rb_ref,
                x1_ref, h2_ref, lg_ref, lgT_ref):
    n1 = _rms(on_ref[...], nwn_ref[...]).astype(BF16)
    n2 = _rms(os_ref[...], nws_ref[...]).astype(BF16)
    x1 = x_ref[...] + _dot(n1, wo_ref[0:N_Q]) + _dot(n2, wo_ref[N_Q:N_Q + N_SBH])
    x1_ref[...] = x1
    h2 = _rms(x1, fw_ref[...])
    hi = h2.astype(BF16)
    lo = (h2 - hi.astype(F32)).astype(BF16)
    h2_ref[...] = hi
    lg = _dot(hi, rhi_ref[...]) + _dot(hi, rlo_ref[...]) + _dot(lo, rhi_ref[...]) + rb_ref[...]
    lg_ref[...] = lg
    lgT_ref[...] = lg.T[0:ROUTER_ROWS]


def _mix(x, o_nsa, o_sb, nsa_norm_w, sb_norm_w, w_out, ffn_norm_w, rg_w, rg_b, re_w, re_b):
    T, D = x.shape
    R = PROJ_ROWS
    pad = ROUTER_LANES - N_GROUPS - N_EXPERTS
    wr = jnp.concatenate([rg_w, re_w, jnp.zeros((D, pad), F32)], axis=1)
    wr_hi = wr.astype(BF16)
    wr_lo = (wr - wr_hi.astype(F32)).astype(BF16)
    rb = jnp.concatenate([rg_b, re_b, jnp.zeros((pad,), F32)]).reshape(1, ROUTER_LANES)
    full = lambda shape: pl.BlockSpec(shape, lambda i: (0,) * len(shape))
    rows = lambda n: pl.BlockSpec((R, n), lambda i: (i, 0))
    return pl.pallas_call(
        _mix_kernel,
        grid=(T // R,),
        in_specs=[rows(D), rows(N_Q), rows(N_SBH), full((1, N_Q)), full((1, N_SBH)), full((N_Q + N_SBH, D)),
                  full((1, D)), full((D, ROUTER_LANES)), full((D, ROUTER_LANES)), full((1, ROUTER_LANES))],
        out_specs=[rows(D), rows(D), rows(ROUTER_LANES), pl.BlockSpec((ROUTER_ROWS, R), lambda i: (0, i))],
        out_shape=[jax.ShapeDtypeStruct((T, D), F32), jax.ShapeDtypeStruct((T, D), BF16),
                   jax.ShapeDtypeStruct((T, ROUTER_LANES), F32), jax.ShapeDtypeStruct((ROUTER_ROWS, T), F32)],
        compiler_params=pltpu.CompilerParams(dimension_semantics=("arbitrary",),
                                             vmem_limit_bytes=VMEM_LIMIT),
    )(x, o_nsa, o_sb, nsa_norm_w.reshape(1, N_Q), sb_norm_w.reshape(1, N_SBH), w_out.astype(BF16),
      ffn_norm_w.reshape(1, D), wr_hi, wr_lo, rb)


def _routing(lg, axis):
    pos_i = lax.broadcasted_iota(jnp.int32, lg.shape, axis)
    pos = pos_i.astype(F32)
    first_max = lambda v, mx: jnp.min(jnp.where(v == mx, pos, float(ROUTER_LANES)), axis=axis, keepdims=True)
    gl = jnp.where(pos_i < N_GROUPS, lg, -jnp.inf)
    gmax = jnp.max(gl, axis=axis, keepdims=True)
    grp = first_max(gl, gmax)
    g_gate = 1.0 / jnp.sum(jnp.exp(gl - gmax), axis=axis, keepdims=True)
    e_idx = pos_i - N_GROUPS
    e_grp = (e_idx >> 2).astype(F32)
    in_grp = (e_idx >= 0) & (e_idx < N_EXPERTS) & (e_grp == grp)
    el = jnp.where(in_grp, lg, -jnp.inf)
    top1 = jnp.max(el, axis=axis, keepdims=True)
    i1 = first_max(el, top1)
    el2 = jnp.where(pos == i1, -jnp.inf, el)
    top2 = jnp.max(el2, axis=axis, keepdims=True)
    i2 = first_max(el2, top2)
    e2 = jnp.exp(top2 - top1)
    w1 = 1.0 / (1.0 + e2)
    w2 = e2 / (1.0 + e2)
    weight = g_gate * (jnp.where(pos == i1, w1, 0.0) + jnp.where(pos == i2, w2, 0.0))
    routed = jnp.where(pos == i1, 1.0, 0.0) + jnp.where(pos == i2, 1.0, 0.0)
    return weight, routed


def _moe_kernel(h_ref, lg_ref, lgT_ref, x1_ref, before_ref, beforeT_ref, wg_ref, wu_ref, wd_ref, fw_ref, o_ref,
                acc_scr, rank_scr, cw_scr, rankT_scr):
    e = pl.program_id(1)
    rows = h_ref.shape[0]

    @pl.when(e == 0)
    def _():
        acc_scr[...] = jnp.zeros(acc_scr.shape, F32)
        weight, routed = _routing(lg_ref[...], 1)
        rank = _dot(beforeT_ref[...], routed.astype(BF16))
        rank_scr[...] = jnp.where(routed > 0.0, rank, -1.0)
        cw_scr[...] = weight
        _, routed_t = _routing(lgT_ref[...], 0)
        rank_t = _dot(routed_t.astype(BF16), before_ref[...])
        rankT_scr[...] = jnp.where(routed_t > 0.0, rank_t, -1.0)

    lane = lax.broadcasted_iota(jnp.int32, (rows, ROUTER_LANES), 1)
    mine = lane == e + N_GROUPS
    rank_col = jnp.sum(jnp.where(mine, rank_scr[...], 0.0), axis=1, keepdims=True)
    w_col = jnp.sum(jnp.where(mine, cw_scr[...], 0.0), axis=1, keepdims=True)
    rank_row = rankT_scr[pl.ds(e + N_GROUPS, 1), :]
    n_routed = jnp.max(rank_row).astype(jnp.int32) + 1
    slot_col = lax.broadcasted_iota(jnp.int32, (MOE_CAP, 1), 0)
    slot_row = lax.broadcasted_iota(jnp.int32, (1, MOE_CAP), 1)

    def chunk(ch, carry):
        base = ch * MOE_CAP
        gather = jnp.where(rank_row == (slot_col + base).astype(F32), 1.0, 0.0).astype(BF16)
        scatter = jnp.where(rank_col == (slot_row + base).astype(F32), 1.0, 0.0).astype(BF16)
        xg = _dot(gather, h_ref[...]).astype(BF16)
        a = _dot(xg, wg_ref[0])
        b = _dot(xg, wu_ref[0])
        act = (a * jax.nn.sigmoid(a) * b).astype(BF16)
        y = _dot(act, wd_ref[0])
        acc_scr[...] += _dot(scatter, y.astype(BF16)) * w_col
        return carry

    lax.fori_loop(0, (n_routed + (MOE_CAP - 1)) // MOE_CAP, chunk, 0)

    @pl.when(e == N_EXPERTS - 1)
    def _():
        o_ref[...] = _rms(x1_ref[...] + acc_scr[...], fw_ref[...])


def _moe(h2, logits, logitsT, x1, w_gate, w_up, w_down, final_norm_w):
    T, D = x1.shape
    R = min(MOE_ROWS, T)
    before = jnp.triu(jnp.ones((R, R), BF16), k=1)
    rows = lambda n: pl.BlockSpec((R, n), lambda i, e: (i, 0))
    const = lambda a, b: pl.BlockSpec((a, b), lambda i, e: (0, 0))
    per_expert = lambda a, b: pl.BlockSpec((1, a, b), lambda i, e: (e, 0, 0))
    return pl.pallas_call(
        _moe_kernel,
        grid=(T // R, N_EXPERTS),
        in_specs=[rows(D), rows(ROUTER_LANES), pl.BlockSpec((ROUTER_ROWS, R), lambda i, e: (0, i)), rows(D),
                  const(R, R), const(R, R),
                  per_expert(D, EXPERT_FF), per_expert(D, EXPERT_FF), per_expert(EXPERT_FF, D), const(1, D)],
        out_specs=rows(D),
        out_shape=jax.ShapeDtypeStruct((T, D), F32),
        scratch_shapes=[pltpu.VMEM((R, D), F32), pltpu.VMEM((R, ROUTER_LANES), F32),
                        pltpu.VMEM((R, ROUTER_LANES), F32), pltpu.VMEM((ROUTER_ROWS, R), F32)],
        compiler_params=pltpu.CompilerParams(dimension_semantics=("arbitrary", "arbitrary"),
                                             vmem_limit_bytes=VMEM_LIMIT),
    )(h2, logits, logitsT, x1, before, before.T, w_gate.astype(BF16), w_up.astype(BF16), w_down.astype(BF16),
      final_norm_w.reshape(1, D))


def kernel(x, positions, attn_norm_w, w_in, cmp_pe_k, cmp_pe_v, cmp_k_w1, cmp_k_w2, cmp_v_w1, cmp_v_w2,
           nsa_out_norm_w, sb_out_norm_w, w_out, ffn_norm_w, router_group_w, router_group_b,
           router_expert_w, router_expert_b, w_gate, w_up, w_down, final_norm_w):
    B, T, D = x.shape
    assert B == 1 and T % SEL_TILE == 0 and T % PROJ_ROWS == 0 and T // SEL_LEN >= SEL_TOP
    assert attn_norm_w.shape[0] == 1, "the final norm is fused into the (single) layer's MoE kernel"
    xs = x.reshape(T, D)
    pos = positions.reshape(T)
    (cmpk, cmpv, sbq, sbk, sbv, qT, ksel, kwin, vselT, vwinT, gT) = _project(xs, pos, attn_norm_w[0], w_in[0])
    kc, vcT = _compress(cmpk, cmpv, cmp_pe_k[0], cmp_pe_v[0], cmp_k_w1[0], cmp_k_w2[0], cmp_v_w1[0], cmp_v_w2[0])
    o_nsa = _nsa(qT, kc, vcT, ksel, vselT, kwin, vwinT, gT)
    o_sb = _stick_breaking(sbq, sbk, sbv)
    x1, h2, logits, logitsT = _mix(xs, o_nsa, o_sb, nsa_out_norm_w[0], sb_out_norm_w[0], w_out[0], ffn_norm_w[0],
                                   router_group_w[0], router_group_b[0], router_expert_w[0], router_expert_b[0])
    out = _moe(h2, logits, logitsT, x1, w_gate[0], w_up[0], w_down[0], final_norm_w)
    return out.reshape(B, T, D)
```

```python
import functools

import jax
import jax.numpy as jnp
from jax import lax
from jax.experimental import pallas as pl
from jax.experimental.pallas import tpu as pltpu

HEAD_DIM = 64
NSA_HEADS = 8
NSA_KV_HEADS = 2
NSA_GROUP = NSA_HEADS // NSA_KV_HEADS
SB_HEADS = 8
ROPE_THETA = 500000.0
ROPE_DIM = HEAD_DIM // 4
ROPE_HALF = ROPE_DIM // 2
CMP_LEN = 32
CMP_STRIDE = 16
CMP_HIDDEN = 256
SEL_LEN = 64
SEL_TOP = 16
SEL_BONUS = 1.0e4
WINDOW = 512
Q_BLOCK = 128
N_GROUPS = 4
EXPERTS_PER_GROUP = 4
N_EXPERTS = N_GROUPS * EXPERTS_PER_GROUP
EXPERT_FF = 512
EPS = 1e-6
NEG = -1e30
LOWEST = -3.0e38

N_Q = NSA_HEADS * HEAD_DIM
N_KVG = NSA_KV_HEADS * HEAD_DIM
N_GATE = NSA_HEADS * 3
N_GATE_PAD = 32
N_SBH = SB_HEADS * HEAD_DIM
SCALE = HEAD_DIM ** -0.5
LOG2E = 1.4426950408889634
SEL_BLOCKS_PER_TILE = 8
BIAS_ROWS = 16
CMP_CLASSES = 8
SEL_V_ROWS = HEAD_DIM + 16

PROJ_ROWS = 512
SEL_TILE = 512
WIN_TILES = WINDOW // Q_BLOCK + 1
SB_TILE = 128
MOE_ROWS = 1024
MOE_CAP = 256
ROUTER_LANES = 128
ROUTER_ROWS = 32
SB_FIRST_TILES = 3
SB_SKIP_LOG = -104.0

VMEM_LIMIT = 56 * 1024 * 1024

BF16 = jnp.bfloat16
F32 = jnp.float32


def _rms(x, w):
    return x * lax.rsqrt(jnp.mean(x * x, axis=-1, keepdims=True) + EPS) * w


def _dot(a, b):
    return jnp.dot(a, b, preferred_element_type=F32)


def _dot_nt(a, b):
    return lax.dot_general(a, b, (((1,), (1,)), ((), ())), preferred_element_type=F32)


def _split3(x):
    hi = x.astype(BF16)
    r1 = x - hi.astype(F32)
    mid = r1.astype(BF16)
    lo = (r1 - mid.astype(F32)).astype(BF16)
    return hi, mid, lo


def _proj_kernel(x_ref, nw_ref, wn_ref, wt_ref, pos_ref, invf_ref,
                 cmpk_ref, cmpv_ref, sbqT_ref, sbk_ref, sbvT_ref,
                 qT_ref, ksel_ref, kwin_ref, vselT_ref, vwinT_ref, gT_ref):
    h = _rms(x_ref[...], nw_ref[...]).astype(BF16)
    p1 = _dot(h, wn_ref[...])
    cmpk_ref[...] = p1[:, 0:N_KVG]
    cmpv_ref[...] = p1[:, N_KVG:2 * N_KVG]
    sbk_ref[...] = p1[:, 2 * N_KVG:2 * N_KVG + N_SBH].astype(BF16)

    p2 = _dot_nt(wt_ref[...], h)
    o = N_Q + 4 * N_KVG + N_GATE_PAD
    sbqT_ref[...] = (p2[o:o + N_SBH] * SCALE).astype(BF16)
    sbv = p2[o + N_SBH:o + 2 * N_SBH].astype(BF16)
    for j in range(PROJ_ROWS // Q_BLOCK):
        sbvT_ref[j] = sbv[:, j * Q_BLOCK:(j + 1) * Q_BLOCK]
    ang = invf_ref[...] * pos_ref[...].astype(F32)
    cos, sin = jnp.cos(ang), jnp.sin(ang)
    n_rope_heads = NSA_HEADS + 2 * NSA_KV_HEADS
    roped = []
    for hd in range(n_rope_heads):
        blk = p2[hd * HEAD_DIM:(hd + 1) * HEAD_DIM]
        x1, x2 = blk[0:ROPE_HALF], blk[ROPE_HALF:ROPE_DIM]
        roped.append(jnp.concatenate(
            [x1 * cos - x2 * sin, x2 * cos + x1 * sin, blk[ROPE_DIM:]], axis=0))
    qT_ref[...] = (jnp.concatenate(roped[:NSA_HEADS], axis=0) * (SCALE * LOG2E)).astype(BF16)
    kT = jnp.concatenate(roped[NSA_HEADS:], axis=0)
    kn = kT.T.astype(BF16)
    r_blk = lax.broadcasted_iota(jnp.int32, (PROJ_ROWS, HEAD_DIM), 0) >> 6
    c_idx = lax.broadcasted_iota(jnp.int32, (PROJ_ROWS, HEAD_DIM), 1)
    onehot = jnp.where(r_blk == c_idx, 1.0, 0.0).astype(BF16)
    for gk in range(NSA_KV_HEADS):
        ksel_ref[gk] = jnp.concatenate([kn[:, gk * HEAD_DIM:(gk + 1) * HEAD_DIM], onehot], axis=1)
    kwin_ref[...] = kn[:, N_KVG:2 * N_KVG]
    o = N_Q + 2 * N_KVG
    for gk in range(NSA_KV_HEADS):
        vselT_ref[0, gk] = jnp.concatenate(
            [p2[o + gk * HEAD_DIM:o + (gk + 1) * HEAD_DIM],
             jnp.where(lax.broadcasted_iota(jnp.int32, (SEL_V_ROWS - HEAD_DIM, PROJ_ROWS), 0) == 0, 1.0, 0.0)],
            axis=0).astype(BF16)
    vw = p2[o + N_KVG:o + 2 * N_KVG].astype(BF16)
    for j in range(PROJ_ROWS // Q_BLOCK):
        vwinT_ref[j] = vw[:, j * Q_BLOCK:(j + 1) * Q_BLOCK]
    o = o + 2 * N_KVG
    gT_ref[...] = jax.nn.sigmoid(p2[o:o + N_GATE_PAD])


def _project(x, positions, attn_norm_w, w_in):
    T, D = x.shape
    R = PROJ_ROWS
    o1, o2, o3 = N_Q, N_Q + 6 * N_KVG, N_Q + 6 * N_KVG + N_GATE
    kv = lambda i: w_in[:, o1 + i * N_KVG:o1 + (i + 1) * N_KVG]
    sb = lambda i: w_in[:, o3 + i * N_SBH:o3 + (i + 1) * N_SBH]
    w_nat = jnp.concatenate([kv(0), kv(1), sb(1)], axis=1).astype(BF16)
    w_t = jnp.concatenate(
        [w_in[:, :o1], kv(2), kv(4), kv(3), kv(5), w_in[:, o2:o3],
         jnp.zeros((D, N_GATE_PAD - N_GATE), w_in.dtype), sb(0), sb(2)], axis=1).T.astype(BF16)
    inv_freq = ROPE_THETA ** (-jnp.arange(0, ROPE_DIM, 2, dtype=F32) / ROPE_DIM)
    n_nat, n_t = w_nat.shape[1], w_t.shape[0]
    full = lambda shape: pl.BlockSpec(shape, lambda i: (0,) * len(shape))
    rows = lambda n: pl.BlockSpec((R, n), lambda i: (i, 0))
    cols = lambda n: pl.BlockSpec((n, R), lambda i: (0, i))
    return pl.pallas_call(
        _proj_kernel,
        grid=(T // R,),
        in_specs=[rows(D), full((1, D)), full((D, n_nat)), full((n_t, D)), cols(1), full((ROPE_HALF, 1))],
        out_specs=[rows(N_KVG), rows(N_KVG), cols(N_SBH), rows(N_SBH),
                   pl.BlockSpec((R // Q_BLOCK, N_SBH, Q_BLOCK), lambda i: (i, 0, 0)),
                   cols(N_Q), pl.BlockSpec((NSA_KV_HEADS, R, N_KVG), lambda i: (0, i, 0)), rows(N_KVG),
                   pl.BlockSpec((1, NSA_KV_HEADS, SEL_V_ROWS, R), lambda i: (i, 0, 0, 0)),
                   pl.BlockSpec((R // Q_BLOCK, N_KVG, Q_BLOCK), lambda i: (i, 0, 0)),
                   cols(N_GATE_PAD)],
        out_shape=[jax.ShapeDtypeStruct((T, N_KVG), F32), jax.ShapeDtypeStruct((T, N_KVG), F32),
                   jax.ShapeDtypeStruct((N_SBH, T), BF16), jax.ShapeDtypeStruct((T, N_SBH), BF16),
                   jax.ShapeDtypeStruct((T // Q_BLOCK, N_SBH, Q_BLOCK), BF16),
                   jax.ShapeDtypeStruct((N_Q, T), BF16),
                   jax.ShapeDtypeStruct((NSA_KV_HEADS, T, N_KVG), BF16), jax.ShapeDtypeStruct((T, N_KVG), BF16),
                   jax.ShapeDtypeStruct((T // R, NSA_KV_HEADS, SEL_V_ROWS, R), BF16),
                   jax.ShapeDtypeStruct((T // Q_BLOCK, N_KVG, Q_BLOCK), BF16),
                   jax.ShapeDtypeStruct((N_GATE_PAD, T), F32)],
        compiler_params=pltpu.CompilerParams(dimension_semantics=("arbitrary",),
                                             vmem_limit_bytes=VMEM_LIMIT),
    )(x, attn_norm_w.reshape(1, D), w_nat, w_t, positions.reshape(1, T), inv_freq.reshape(ROPE_HALF, 1))


def _compress_kernel(x_ref, pea_ref, peb_ref, wa_ref, wb_ref, w2_ref, nat_ref, tr_ref):
    x = x_ref[0]
    nc = x.shape[0]
    ha = _dot((x + pea_ref[0]).astype(BF16), wa_ref[0])
    hb = _dot((x + peb_ref[0]).astype(BF16), wb_ref[0])
    hid = ha + pltpu.roll(hb, nc - 1, 0)
    act = (hid * jax.nn.sigmoid(hid)).astype(BF16)
    out = _dot(act, w2_ref[0])
    nat_ref[0] = out.astype(BF16)
    tr_ref[0] = out.T.astype(BF16)


def _compress(cmpk, cmpv, pe_k, pe_v, k_w1, k_w2, v_w1, v_w2):
    T = cmpk.shape[0]
    nc = T // CMP_STRIDE
    half = CMP_LEN // 2
    G = NSA_KV_HEADS
    width = half * N_KVG
    x = jnp.stack([cmpk.reshape(nc, width), cmpv.reshape(nc, width)])
    eye = jnp.eye(G, dtype=F32)

    def pe_rows(pe):
        return jnp.broadcast_to(pe[:, None, :], (half, G, HEAD_DIM)).reshape(1, width)

    def w1_block(w1):
        w = w1.reshape(half, HEAD_DIM, CMP_HIDDEN)
        return jnp.einsum('ldj,gh->lgdhj', w, eye).reshape(width, G * CMP_HIDDEN).astype(BF16)

    def w2_block(w2):
        return jnp.einsum('jd,gh->gjhd', w2, eye).reshape(G * CMP_HIDDEN, N_KVG).astype(BF16)

    hw = half * HEAD_DIM
    pea = jnp.stack([pe_rows(pe_k[:half]), pe_rows(pe_v[:half])])
    peb = jnp.stack([pe_rows(pe_k[half:]), pe_rows(pe_v[half:])])
    wa = jnp.stack([w1_block(k_w1[:hw]), w1_block(v_w1[:hw])])
    wb = jnp.stack([w1_block(k_w1[hw:]), w1_block(v_w1[hw:])])
    w2 = jnp.stack([w2_block(k_w2), w2_block(v_w2)])
    blk = lambda a, b: pl.BlockSpec((1, a, b), lambda i: (i, 0, 0))
    nat, tr = pl.pallas_call(
        _compress_kernel,
        grid=(2,),
        in_specs=[blk(nc, width), blk(1, width), blk(1, width), blk(width, G * CMP_HIDDEN),
                  blk(width, G * CMP_HIDDEN), blk(G * CMP_HIDDEN, N_KVG)],
        out_specs=[blk(nc, N_KVG), blk(N_KVG, nc)],
        out_shape=[jax.ShapeDtypeStruct((2, nc, N_KVG), BF16), jax.ShapeDtypeStruct((2, N_KVG, nc), BF16)],
        compiler_params=pltpu.CompilerParams(dimension_semantics=("arbitrary",),
                                             vmem_limit_bytes=VMEM_LIMIT),
    )(x, pea, peb, wa, wb, w2)
    return nat[0], tr[1]


def _nsa_kernel(q_ref, kc_ref, vcT_ref, ovT_ref, ksel_ref, vselT_ref, *rest):
    kwin_refs = rest[0:WIN_TILES]
    vwin_refs = rest[WIN_TILES:2 * WIN_TILES]
    gT_ref, o_ref = rest[2 * WIN_TILES:2 * WIN_TILES + 2]
    scratch = rest[2 * WIN_TILES + 2:]
    G = NSA_KV_HEADS
    per = len(scratch) // G
    bias_scr, s0_scr, s1_scr, t0_scr, t1_scr, m_scr, acc_scr, oc_scr = (
        [scratch[g * per + n] for g in range(G)] for n in range(per))
    groups = range(G)
    i = pl.program_id(0)
    R, Q = NSA_GROUP, Q_BLOCK
    L = R * Q
    nsel = bias_scr[0].shape[0]

    zero = jnp.zeros((HEAD_DIM, L), BF16)
    qg, qz = [], []
    for g in groups:
        q = jnp.concatenate([q_ref[(g * R + r) * HEAD_DIM:(g * R + r + 1) * HEAD_DIM, :] for r in range(R)], axis=1)
        qg.append(q)
        qz.append(jnp.concatenate([q if gg == g else zero for gg in groups], axis=0))

    def own_rows(x, g):
        return x[g * HEAD_DIM:(g + 1) * HEAD_DIM]

    t_lane = i * Q + (lax.broadcasted_iota(jnp.int32, (1, L), 1) & (Q - 1))

    nc = kc_ref.shape[0]
    q_pos = i * Q + lax.broadcasted_iota(jnp.int32, (1, Q), 1)
    bq = (q_pos >> 6).astype(F32)

    def compress_and_select(n_eff, first_class):
        m_eff = n_eff // (SEL_LEN // CMP_STRIDE)
        cmp_end = lax.broadcasted_iota(jnp.int32, (n_eff, 1), 0) * CMP_STRIDE + (CMP_LEN - 1)
        visible = cmp_end <= t_lane
        sc = [_dot(kc_ref[0:n_eff, :], qz[g]) for g in groups]
        pc = []
        for g in groups:
            s = jnp.where(visible, sc[g], NEG)
            mxc = jnp.max(s, axis=0, keepdims=True)
            mxc = jnp.where(mxc < 0.5 * NEG, 0.0, mxc)
            ec = jnp.exp2(s - mxc)
            pc.append(ec * (1.0 / jnp.maximum(jnp.sum(ec, axis=0, keepdims=True), 1e-30)))
        for g in groups:
            oc_scr[g][...] = own_rows(_dot(vcT_ref[:, 0:n_eff], pc[g].astype(BF16)), g)
        ov = ovT_ref[0:m_eff, 0:n_eff]
        imp = []
        for g in groups:
            psum = pc[g][:, 0:Q]
            for r in range(1, R):
                psum = psum + pc[g][:, r * Q:(r + 1) * Q]
            imp.append(sum(_dot(ov, part) for part in _split3(psum)))
        m_idx = lax.broadcasted_iota(jnp.int32, (m_eff, Q), 0).astype(F32)
        allowed = m_idx <= bq
        forced = (m_idx == 0.0) | (m_idx == bq) | (m_idx == bq - 1.0)
        if first_class:
            score = [jnp.where(allowed, imp[g] + jnp.where(forced, SEL_BONUS, 0.0), NEG) for g in groups]
            n_pick = min(SEL_TOP, m_eff)
        else:
            free = allowed & jnp.logical_not(forced)
            score = [jnp.where(free, imp[g], NEG) for g in groups]
            n_pick = SEL_TOP - 3
        for _ in range(n_pick):
            for g in groups:
                best = jnp.max(score[g], axis=0, keepdims=True)
                first = jnp.min(jnp.where(score[g] == best, m_idx, float(m_eff)), axis=0, keepdims=True)
                score[g] = jnp.where(m_idx == first, LOWEST, score[g])
        for g in groups:
            picked = score[g] < 0.5 * LOWEST
            if not first_class:
                picked = picked | forced
            bias = jnp.where(allowed & picked, 0.0, NEG)
            bias_scr[g][0:m_eff, :] = jnp.concatenate([bias] * R, axis=1)
            if m_eff < nsel:
                bias_scr[g][m_eff:nsel, :] = jnp.full((nsel - m_eff, L), NEG, F32)

    sizes = sorted({max(Q, (nc * k // CMP_CLASSES) // Q * Q) for k in range(1, CMP_CLASSES + 1)})
    lo = 0
    for n_eff in sizes:
        hi = n_eff // (Q // CMP_STRIDE)
        pl.when((i >= lo) & (i < hi))(functools.partial(compress_and_select, n_eff, lo == 0))
        lo = hi

    for g in groups:
        m_scr[g][...] = jnp.full(m_scr[g].shape, NEG, F32)
        acc_scr[g][...] = jnp.zeros(acc_scr[g].shape, F32)
    rhs_pad = jnp.zeros((N_KVG - HEAD_DIM - BIAS_ROWS, L), BF16)
    bias_pad = jnp.zeros((BIAS_ROWS - SEL_BLOCKS_PER_TILE, L), F32)
    buf0, buf1 = (s0_scr, t0_scr), (s1_scr, t1_scr)

    def scores(kt, dst):
        for g in groups:
            brows = bias_scr[g][pl.ds(pl.multiple_of(kt * SEL_BLOCKS_PER_TILE, SEL_BLOCKS_PER_TILE),
                                      SEL_BLOCKS_PER_TILE), :]
            rhs = jnp.concatenate([qg[g], jnp.concatenate([brows, bias_pad], axis=0).astype(BF16), rhs_pad], axis=0)
            s = _dot(ksel_ref[g, kt], rhs)
            dst[0][g][...] = s
            dst[1][g][...] = jnp.max(s, axis=0, keepdims=True)

    def absorb(kt, src):
        for g in groups:
            m_old = m_scr[g][...]
            m_new = jnp.maximum(m_old, src[1][g][...])
            alpha = jnp.exp2(m_old - m_new)
            p = jnp.exp2(src[0][g][...] - m_new)
            acc_scr[g][...] = alpha * acc_scr[g][...] + _dot(vselT_ref[kt, g], p.astype(BF16))
            m_scr[g][...] = m_new

    def step(kt, src, dst):
        scores(kt + 1, dst)
        absorb(kt, src)

    def two_steps(kt):
        step(kt, buf0, buf1)
        step(kt + 1, buf1, buf0)

    k_row = lax.broadcasted_iota(jnp.int32, (Q, 1), 0)
    q_lane = lax.broadcasted_iota(jnp.int32, (1, L), 1) & (Q - 1)
    n_full = (i * Q) // SEL_TILE

    def last_tile(src):
        r0 = pl.multiple_of(i * Q - n_full * SEL_TILE, Q)
        for g in groups:
            src[0][g][pl.ds(r0, Q), :] = jnp.where(k_row <= q_lane, src[0][g][pl.ds(r0, Q), :], NEG)
            src[1][g][...] = jnp.max(src[0][g][...], axis=0, keepdims=True)
        absorb(n_full, src)

    odd = n_full & 1

    @pl.when(odd == 0)
    def _():
        scores(0, buf0)

    @pl.when(odd == 1)
    def _():
        scores(0, buf1)
        step(0, buf1, buf0)

    n_pairs = n_full >> 1

    def four_steps(j, carry):
        two_steps(odd + 4 * j)
        two_steps(odd + 4 * j + 2)
        return carry

    lax.fori_loop(0, n_pairs >> 1, four_steps, 0)

    @pl.when((n_pairs & 1) == 1)
    def _():
        two_steps(odd + 4 * (n_pairs >> 1))

    last_tile(buf0)

    sw = [[] for _ in groups]
    for w in range(WIN_TILES):
        j = i - (WIN_TILES - 1) + w
        for g in groups:
            s = _dot(kwin_refs[w][0], qz[g])
            if w == 0:
                s = jnp.where(k_row > q_lane, s, NEG)
            if w == WIN_TILES - 1:
                s = jnp.where(k_row <= q_lane, s, NEG)
            else:
                s = jnp.where(j >= 0, s, NEG)
            sw[g].append(s)
    ew, denw = [], []
    for g in groups:
        mxw = functools.reduce(jnp.maximum, [jnp.max(s, axis=0, keepdims=True) for s in sw[g]])
        ew.append([jnp.exp2(s - mxw) for s in sw[g]])
        denw.append(jnp.maximum(sum(jnp.sum(e, axis=0, keepdims=True) for e in ew[g]), 1e-30))
    owT = []
    for g in groups:
        prod = sum(_dot(vwin_refs[w][0], ew[g][w].astype(BF16)) for w in range(WIN_TILES))
        owT.append(own_rows(prod, g) * (1.0 / denw[g]))

    def gate_row(g, j):
        return jnp.concatenate([gT_ref[(g * R + r) * 3 + j:(g * R + r) * 3 + j + 1, :] for r in range(R)], axis=1)

    o_rows = []
    for g in groups:
        osT = acc_scr[g][0:HEAD_DIM, :] * (1.0 / acc_scr[g][HEAD_DIM:HEAD_DIM + 1, :])
        oT = gate_row(g, 0) * oc_scr[g][...] + gate_row(g, 1) * osT + gate_row(g, 2) * owT[g]
        o_rows += [oT[:, r * Q:(r + 1) * Q] for r in range(R)]
    o_ref[...] = jnp.concatenate(o_rows, axis=0).T


def _nsa(qT, kc, vcT, ksel, vselT, kwin, vwinT, gT):
    T = qT.shape[1]
    nb = T // Q_BLOCK
    nc = kc.shape[0]
    nsel = T // SEL_LEN
    ntile = T // SEL_TILE
    n = jnp.arange(nc)[None, :] * CMP_STRIDE
    m = jnp.arange(nsel)[:, None] * SEL_LEN
    ovT = ((n < m + SEL_LEN) & (n + CMP_LEN > m)).astype(BF16)
    ksel4 = ksel.reshape(NSA_KV_HEADS, ntile, SEL_TILE, N_KVG)
    kwin3 = kwin.reshape(nb, Q_BLOCK, N_KVG)
    L = NSA_GROUP * Q_BLOCK
    const = lambda shape: pl.BlockSpec(shape, lambda i: (0,) * len(shape))
    win_tile = lambda i, w: (jnp.maximum(i - (WIN_TILES - 1) + w, 0), 0, 0)
    kwin_specs = [pl.BlockSpec((1, Q_BLOCK, N_KVG), functools.partial(win_tile, w=w)) for w in range(WIN_TILES)]
    vwin_specs = [pl.BlockSpec((1, N_KVG, Q_BLOCK), functools.partial(win_tile, w=w)) for w in range(WIN_TILES)]
    group_scratch = [pltpu.VMEM((nsel, L), F32), pltpu.VMEM((SEL_TILE, L), F32), pltpu.VMEM((SEL_TILE, L), F32),
                     pltpu.VMEM((1, L), F32), pltpu.VMEM((1, L), F32),
                     pltpu.VMEM((1, L), F32), pltpu.VMEM((SEL_V_ROWS, L), F32), pltpu.VMEM((HEAD_DIM, L), F32)]
    return pl.pallas_call(
        _nsa_kernel,
        grid=(nb,),
        in_specs=[pl.BlockSpec((N_Q, Q_BLOCK), lambda i: (0, i)),
                  const((nc, N_KVG)), const((N_KVG, nc)), const((nsel, nc)),
                  const((NSA_KV_HEADS, ntile, SEL_TILE, N_KVG)), const((ntile, NSA_KV_HEADS, SEL_V_ROWS, SEL_TILE))]
                 + kwin_specs + vwin_specs
                 + [pl.BlockSpec((N_GATE_PAD, Q_BLOCK), lambda i: (0, i))],
        out_specs=pl.BlockSpec((Q_BLOCK, N_Q), lambda i: (i, 0)),
        out_shape=jax.ShapeDtypeStruct((T, N_Q), F32),
        scratch_shapes=group_scratch * NSA_KV_HEADS,
        compiler_params=pltpu.CompilerParams(dimension_semantics=("arbitrary",),
                                             vmem_limit_bytes=VMEM_LIMIT),
    )(qT, kc, vcT, ovT, ksel4, vselT, *([kwin3] * WIN_TILES), *([vwinT] * WIN_TILES), gT)


def _sb_kernel(qT_ref, k_ref, vT_ref, o_ref, *scr):
    i = pl.program_id(0)
    Q, W = Q_BLOCK, 2 * HEAD_DIM
    pairs = SB_HEADS // 2
    blk_row = lax.broadcasted_iota(jnp.int32, (W, 2 * Q), 0) < HEAD_DIM
    blk_lane = lax.broadcasted_iota(jnp.int32, (W, 2 * Q), 1) < Q
    zero = jnp.zeros((), BF16)
    q_pairs = []
    for pr in range(pairs):
        x = qT_ref[pr * W:(pr + 1) * W, :]
        q_pairs.append(jnp.where(blk_row == blk_lane, jnp.concatenate([x, x], axis=1), zero))
    k_row = lax.broadcasted_iota(jnp.int32, (SB_TILE, 1), 0)
    q_lane = lax.broadcasted_iota(jnp.int32, (1, SB_HEADS * Q), 1) & (Q - 1)
    rr = lax.broadcasted_iota(jnp.int32, (SB_TILE, SB_TILE), 0)
    cc = lax.broadcasted_iota(jnp.int32, (SB_TILE, SB_TILE), 1)
    from_here = (cc >= rr).astype(BF16)
    acc_scrs, c_scrs = scr[0:pairs], scr[pairs:pairs + 2]
    for buf in scr:
        buf[...] = jnp.zeros(buf.shape, F32)

    halves = ((0, 1), (2, 3))
    causal = k_row < q_lane[:, 0:4 * Q]

    def walk(tiles):
        rows = [pl.ds(pl.multiple_of(kt * SB_TILE, SB_TILE), SB_TILE) for kt, _, _ in tiles]
        chains = [(t, hf) for t in range(len(tiles)) for hf in range(2)]
        z, d, suffix = {}, {}, {}
        for t, hf in chains:
            z[t, hf] = jnp.concatenate(
                [_dot(k_ref[rows[t], pr * W:(pr + 1) * W], q_pairs[pr]) for pr in halves[hf]], axis=1)
        for t, hf in chains:
            x = jnp.maximum(z[t, hf], 0.0) + jnp.log(1.0 + jnp.exp(-jnp.abs(z[t, hf])))
            if tiles[t][1]:
                x = jnp.where(causal, x, 0.0)
            if tiles[t][2] is not None:
                x = jnp.where(tiles[t][2], x, 0.0)
            d[t, hf] = x
            hi = x.astype(BF16)
            lo = (x - hi.astype(F32)).astype(BF16)
            suffix[t, hf] = _dot(from_here, hi) + _dot(from_here, lo)
        least = None
        for hf in range(2):
            c = c_scrs[hf][...]
            for t, (kt, own, exists) in enumerate(tiles):
                a = jnp.exp(z[t, hf] - suffix[t, hf] - c)
                if own:
                    a = jnp.where(causal, a, 0.0)
                if exists is not None:
                    a = jnp.where(exists, a, 0.0)
                a = a.astype(BF16)
                for n, pr in enumerate(halves[hf]):
                    acc_scrs[pr][...] = acc_scrs[pr][...] + _dot(vT_ref[kt, pr * W:(pr + 1) * W, :],
                                                                 a[:, n * 2 * Q:(n + 1) * 2 * Q])
                c = c + jnp.sum(d[t, hf], axis=0, keepdims=True)
            c_scrs[hf][...] = c
            least = c if least is None else jnp.minimum(least, c)
        return -jnp.min(least)

    first = [(i, True, None)] + [(jnp.maximum(i - n, 0), False, i - n >= 0) for n in range(1, SB_FIRST_TILES)]
    worst0 = walk(first)

    def cond(carry):
        kt, worst = carry
        return (kt >= 0) & (worst >= SB_SKIP_LOG)

    def body(carry):
        kt, _ = carry
        return kt - 1, walk([(kt, False, None)])

    lax.while_loop(cond, body, (i - SB_FIRST_TILES, worst0))
    for pr in range(pairs):
        acc = acc_scrs[pr][...]
        o_ref[:, pr * W:(pr + 1) * W] = jnp.where(blk_row[:, 0:Q], acc[:, 0:Q], acc[:, Q:2 * Q]).T


def _stick_breaking(sbqT, sbk, sbvT):
    T = sbk.shape[0]
    nb = T // Q_BLOCK
    W = 2 * HEAD_DIM
    return pl.pallas_call(
        _sb_kernel,
        grid=(nb,),
        in_specs=[pl.BlockSpec((N_SBH, Q_BLOCK), lambda i: (0, i)),
                  pl.BlockSpec((T, N_SBH), lambda i: (0, 0)),
                  pl.BlockSpec((nb, N_SBH, Q_BLOCK), lambda i: (0, 0, 0))],
        out_specs=pl.BlockSpec((Q_BLOCK, N_SBH), lambda i: (i, 0)),
        out_shape=jax.ShapeDtypeStruct((T, N_SBH), F32),
        scratch_shapes=[pltpu.VMEM((W, 2 * Q_BLOCK), F32)] * (SB_HEADS // 2) + [pltpu.VMEM((1, 4 * Q_BLOCK), F32)] * 2,
        compiler_params=pltpu.CompilerParams(dimension_semantics=("arbitrary",),
                                             vmem_limit_bytes=VMEM_LIMIT),
    )(sbqT, sbk, sbvT)


def _mix_kernel(x_ref, on_ref, os_ref, nwn_ref, nws_ref, wo_ref, fw_ref, rhi_ref, rlo_ref, rb_ref,
                x1_ref, h2_ref, lg_ref, lgT_ref):
    n1 = _rms(on_ref[...], nwn_ref[...]).astype(BF16)
    n2 = _rms(os_ref[...], nws_ref[...]).astype(BF16)
    x1 = x_ref[...] + _dot(n1, wo_ref[0:N_Q]) + _dot(n2, wo_ref[N_Q:N_Q + N_SBH])
    x1_ref[...] = x1
    h2 = _rms(x1, fw_ref[...])
    hi = h2.astype(BF16)
    lo = (h2 - hi.astype(F32)).astype(BF16)
    h2_ref[...] = hi
    lg = _dot(hi, rhi_ref[...]) + _dot(hi, rlo_ref[...]) + _dot(lo, rhi_ref[...]) + rb_ref[...]
    lg_ref[...] = lg
    lgT_ref[...] = lg.T[0:ROUTER_ROWS]


def _mix(x, o_nsa, o_sb, nsa_norm_w, sb_norm_w, w_out, ffn_norm_w, rg_w, rg_b, re_w, re_b):
    T, D = x.shape
    R = PROJ_ROWS
    pad = ROUTER_LANES - N_GROUPS - N_EXPERTS
    wr = jnp.concatenate([rg_w, re_w, jnp.zeros((D, pad), F32)], axis=1)
    wr_hi = wr.astype(BF16)
    wr_lo = (wr - wr_hi.astype(F32)).astype(BF16)
    rb = jnp.concatenate([rg_b, re_b, jnp.zeros((pad,), F32)]).reshape(1, ROUTER_LANES)
    full = lambda shape: pl.BlockSpec(shape, lambda i: (0,) * len(shape))
    rows = lambda n: pl.BlockSpec((R, n), lambda i: (i, 0))
    return pl.pallas_call(
        _mix_kernel,
        grid=(T // R,),
        in_specs=[rows(D), rows(N_Q), rows(N_SBH), full((1, N_Q)), full((1, N_SBH)), full((N_Q + N_SBH, D)),
                  full((1, D)), full((D, ROUTER_LANES)), full((D, ROUTER_LANES)), full((1, ROUTER_LANES))],
        out_specs=[rows(D), rows(D), rows(ROUTER_LANES), pl.BlockSpec((ROUTER_ROWS, R), lambda i: (0, i))],
        out_shape=[jax.ShapeDtypeStruct((T, D), F32), jax.ShapeDtypeStruct((T, D), BF16),
                   jax.ShapeDtypeStruct((T, ROUTER_LANES), F32), jax.ShapeDtypeStruct((ROUTER_ROWS, T), F32)],
        compiler_params=pltpu.CompilerParams(dimension_semantics=("arbitrary",),
                                             vmem_limit_bytes=VMEM_LIMIT),
    )(x, o_nsa, o_sb, nsa_norm_w.reshape(1, N_Q), sb_norm_w.reshape(1, N_SBH), w_out.astype(BF16),
      ffn_norm_w.reshape(1, D), wr_hi, wr_lo, rb)


def _routing(lg, axis):
    pos_i = lax.broadcasted_iota(jnp.int32, lg.shape, axis)
    pos = pos_i.astype(F32)
    first_max = lambda v, mx: jnp.min(jnp.where(v == mx, pos, float(ROUTER_LANES)), axis=axis, keepdims=True)
    gl = jnp.where(pos_i < N_GROUPS, lg, -jnp.inf)
    gmax = jnp.max(gl, axis=axis, keepdims=True)
    grp = first_max(gl, gmax)
    g_gate = 1.0 / jnp.sum(jnp.exp(gl - gmax), axis=axis, keepdims=True)
    e_idx = pos_i - N_GROUPS
    e_grp = (e_idx >> 2).astype(F32)
    in_grp = (e_idx >= 0) & (e_idx < N_EXPERTS) & (e_grp == grp)
    el = jnp.where(in_grp, lg, -jnp.inf)
    top1 = jnp.max(el, axis=axis, keepdims=True)
    i1 = first_max(el, top1)
    el2 = jnp.where(pos == i1, -jnp.inf, el)
    top2 = jnp.max(el2, axis=axis, keepdims=True)
    i2 = first_max(el2, top2)
    e2 = jnp.exp(top2 - top1)
    w1 = 1.0 / (1.0 + e2)
    w2 = e2 / (1.0 + e2)
    weight = g_gate * (jnp.where(pos == i1, w1, 0.0) + jnp.where(pos == i2, w2, 0.0))
    routed = jnp.where(pos == i1, 1.0, 0.0) + jnp.where(pos == i2, 1.0, 0.0)
    return weight, routed


def _moe_kernel(h_ref, lg_ref, lgT_ref, x1_ref, before_ref, beforeT_ref, wg_ref, wu_ref, wd_ref, fw_ref, o_ref,
                acc_scr, rank_scr, cw_scr, rankT_scr):
    e = pl.program_id(1)
    rows = h_ref.shape[0]

    @pl.when(e == 0)
    def _():
        acc_scr[...] = jnp.zeros(acc_scr.shape, F32)
        weight, routed = _routing(lg_ref[...], 1)
        rank = _dot(beforeT_ref[...], routed.astype(BF16))
        rank_scr[...] = jnp.where(routed > 0.0, rank, -1.0)
        cw_scr[...] = weight
        _, routed_t = _routing(lgT_ref[...], 0)
        rank_t = _dot(routed_t.astype(BF16), before_ref[...])
        rankT_scr[...] = jnp.where(routed_t > 0.0, rank_t, -1.0)

    lane = lax.broadcasted_iota(jnp.int32, (rows, ROUTER_LANES), 1)
    mine = lane == e + N_GROUPS
    rank_col = jnp.sum(jnp.where(mine, rank_scr[...], 0.0), axis=1, keepdims=True)
    w_col = jnp.sum(jnp.where(mine, cw_scr[...], 0.0), axis=1, keepdims=True)
    rank_row = rankT_scr[pl.ds(e + N_GROUPS, 1), :]
    n_routed = jnp.max(rank_row).astype(jnp.int32) + 1
    slot_col = lax.broadcasted_iota(jnp.int32, (MOE_CAP, 1), 0)
    slot_row = lax.broadcasted_iota(jnp.int32, (1, MOE_CAP), 1)

    def chunk(ch, carry):
        base = ch * MOE_CAP
        gather = jnp.where(rank_row == (slot_col + base).astype(F32), 1.0, 0.0).astype(BF16)
        scatter = jnp.where(rank_col == (slot_row + base).astype(F32), 1.0, 0.0).astype(BF16)
        xg = _dot(gather, h_ref[...]).astype(BF16)
        a = _dot(xg, wg_ref[0])
        b = _dot(xg, wu_ref[0])
        act = (a * jax.nn.sigmoid(a) * b).astype(BF16)
        y = _dot(act, wd_ref[0])
        acc_scr[...] += _dot(scatter, y.astype(BF16)) * w_col
        return carry

    lax.fori_loop(0, (n_routed + (MOE_CAP - 1)) // MOE_CAP, chunk, 0)

    @pl.when(e == N_EXPERTS - 1)
    def _():
        o_ref[...] = _rms(x1_ref[...] + acc_scr[...], fw_ref[...])


def _moe(h2, logits, logitsT, x1, w_gate, w_up, w_down, final_norm_w):
    T, D = x1.shape
    R = min(MOE_ROWS, T)
    before = jnp.triu(jnp.ones((R, R), BF16), k=1)
    rows = lambda n: pl.BlockSpec((R, n), lambda i, e: (i, 0))
    const = lambda a, b: pl.BlockSpec((a, b), lambda i, e: (0, 0))
    per_expert = lambda a, b: pl.BlockSpec((1, a, b), lambda i, e: (e, 0, 0))
    return pl.pallas_call(
        _moe_kernel,
        grid=(T // R, N_EXPERTS),
        in_specs=[rows(D), rows(ROUTER_LANES), pl.BlockSpec((ROUTER_ROWS, R), lambda i, e: (0, i)), rows(D),
                  const(R, R), const(R, R),
                  per_expert(D, EXPERT_FF), per_expert(D, EXPERT_FF), per_expert(EXPERT_FF, D), const(1, D)],
        out_specs=rows(D),
        out_shape=jax.ShapeDtypeStruct((T, D), F32),
        scratch_shapes=[pltpu.VMEM((R, D), F32), pltpu.VMEM((R, ROUTER_LANES), F32),
                        pltpu.VMEM((R, ROUTER_LANES), F32), pltpu.VMEM((ROUTER_ROWS, R), F32)],
        compiler_params=pltpu.CompilerParams(dimension_semantics=("arbitrary", "arbitrary"),
                                             vmem_limit_bytes=VMEM_LIMIT),
    )(h2, logits, logitsT, x1, before, before.T, w_gate.astype(BF16), w_up.astype(BF16), w_down.astype(BF16),
      final_norm_w.reshape(1, D))


def kernel(x, positions, attn_norm_w, w_in, cmp_pe_k, cmp_pe_v, cmp_k_w1, cmp_k_w2, cmp_v_w1, cmp_v_w2,
           nsa_out_norm_w, sb_out_norm_w, w_out, ffn_norm_w, router_group_w, router_group_b,
           router_expert_w, router_expert_b, w_gate, w_up, w_down, final_norm_w):
    B, T, D = x.shape
    assert B == 1 and T % SEL_TILE == 0 and T % PROJ_ROWS == 0 and T // SEL_LEN >= SEL_TOP
    assert attn_norm_w.shape[0] == 1, "the final norm is fused into the (single) layer's MoE kernel"
    xs = x.reshape(T, D)
    pos = positions.reshape(T)
    (cmpk, cmpv, sbq, sbk, sbv, qT, ksel, kwin, vselT, vwinT, gT) = _project(xs, pos, attn_norm_w[0], w_in[0])
    kc, vcT = _compress(cmpk, cmpv, cmp_pe_k[0], cmp_pe_v[0], cmp_k_w1[0], cmp_k_w2[0], cmp_v_w1[0], cmp_v_w2[0])
    o_nsa = _nsa(qT, kc, vcT, ksel, vselT, kwin, vwinT, gT)
    o_sb = _stick_breaking(sbq, sbk, sbv)
    x1, h2, logits, logitsT = _mix(xs, o_nsa, o_sb, nsa_out_norm_w[0], sb_out_norm_w[0], w_out[0], ffn_norm_w[0],
                                   router_group_w[0], router_group_b[0], router_expert_w[0], router_expert_b[0])
    out = _moe(h2, logits, logitsT, x1, w_gate[0], w_up[0], w_down[0], final_norm_w)
    return out.reshape(B, T, D)
```

```python
import functools

import jax
import jax.numpy as jnp
from jax import lax
from jax.experimental import pallas as pl
from jax.experimental.pallas import tpu as pltpu

HEAD_DIM = 64
NSA_HEADS = 8
NSA_KV_HEADS = 2
NSA_GROUP = NSA_HEADS // NSA_KV_HEADS
SB_HEADS = 8
ROPE_THETA = 500000.0
ROPE_DIM = HEAD_DIM // 4
ROPE_HALF = ROPE_DIM // 2
CMP_LEN = 32
CMP_STRIDE = 16
CMP_HIDDEN = 256
SEL_LEN = 64
SEL_TOP = 16
SEL_BONUS = 1.0e4
WINDOW = 512
Q_BLOCK = 128
N_GROUPS = 4
EXPERTS_PER_GROUP = 4
N_EXPERTS = N_GROUPS * EXPERTS_PER_GROUP
EXPERT_FF = 512
EPS = 1e-6
NEG = -1e30
LOWEST = -3.0e38

N_Q = NSA_HEADS * HEAD_DIM
N_KVG = NSA_KV_HEADS * HEAD_DIM
N_GATE = NSA_HEADS * 3
N_GATE_PAD = 32
N_SBH = SB_HEADS * HEAD_DIM
SCALE = HEAD_DIM ** -0.5
LOG2E = 1.4426950408889634
SEL_BLOCKS_PER_TILE = 8
BIAS_ROWS = 16
CMP_CLASSES = 8
SEL_V_ROWS = HEAD_DIM + 16

PROJ_ROWS = 512
SEL_TILE = 512
WIN_TILES = WINDOW // Q_BLOCK + 1
SB_TILE = 128
MOE_ROWS = 1024
MOE_EXPERTS_PER_STEP = 2
MOE_SUB_SHIFT = 8
MOE_SUB = 1 << MOE_SUB_SHIFT
MOE_SUB_CAP = 64
ROUTER_LANES = 128
ROUTER_ROWS = 32
SB_FIRST_TILES = 3
SB_SKIP_LOG = -104.0

VMEM_LIMIT = 56 * 1024 * 1024

BF16 = jnp.bfloat16
F32 = jnp.float32


def _rms(x, w):
    return x * lax.rsqrt(jnp.mean(x * x, axis=-1, keepdims=True) + EPS) * w


def _dot(a, b):
    return jnp.dot(a, b, preferred_element_type=F32)


def _dot_nt(a, b):
    return lax.dot_general(a, b, (((1,), (1,)), ((), ())), preferred_element_type=F32)


def _split3(x):
    hi = x.astype(BF16)
    r1 = x - hi.astype(F32)
    mid = r1.astype(BF16)
    lo = (r1 - mid.astype(F32)).astype(BF16)
    return hi, mid, lo


def _proj_kernel(x_ref, nw_ref, wn_ref, wt_ref, pos_ref, invf_ref,
                 cmpk_ref, cmpv_ref, sbqT_ref, sbk_ref, sbvT_ref,
                 qT_ref, ksel_ref, kwin_ref, vselT_ref, vwinT_ref, gT_ref):
    h = _rms(x_ref[...], nw_ref[...]).astype(BF16)
    p1 = _dot(h, wn_ref[...])
    cmpk_ref[...] = p1[:, 0:N_KVG]
    cmpv_ref[...] = p1[:, N_KVG:2 * N_KVG]
    sbk_ref[...] = p1[:, 2 * N_KVG:2 * N_KVG + N_SBH].astype(BF16)

    p2 = _dot_nt(wt_ref[...], h)
    o = N_Q + 4 * N_KVG + N_GATE_PAD
    sbqT_ref[...] = (p2[o:o + N_SBH] * SCALE).astype(BF16)
    sbv = p2[o + N_SBH:o + 2 * N_SBH].astype(BF16)
    for j in range(PROJ_ROWS // Q_BLOCK):
        sbvT_ref[j] = sbv[:, j * Q_BLOCK:(j + 1) * Q_BLOCK]
    ang = invf_ref[...] * pos_ref[...].astype(F32)
    cos, sin = jnp.cos(ang), jnp.sin(ang)
    n_rope_heads = NSA_HEADS + 2 * NSA_KV_HEADS
    roped = []
    for hd in range(n_rope_heads):
        blk = p2[hd * HEAD_DIM:(hd + 1) * HEAD_DIM]
        x1, x2 = blk[0:ROPE_HALF], blk[ROPE_HALF:ROPE_DIM]
        roped.append(jnp.concatenate(
            [x1 * cos - x2 * sin, x2 * cos + x1 * sin, blk[ROPE_DIM:]], axis=0))
    qT_ref[...] = (jnp.concatenate(roped[:NSA_HEADS], axis=0) * (SCALE * LOG2E)).astype(BF16)
    kT = jnp.concatenate(roped[NSA_HEADS:], axis=0)
    kn = kT.T.astype(BF16)
    r_blk = lax.broadcasted_iota(jnp.int32, (PROJ_ROWS, HEAD_DIM), 0) >> 6
    c_idx = lax.broadcasted_iota(jnp.int32, (PROJ_ROWS, HEAD_DIM), 1)
    onehot = jnp.where(r_blk == c_idx, 1.0, 0.0).astype(BF16)
    for gk in range(NSA_KV_HEADS):
        ksel_ref[gk] = jnp.concatenate([kn[:, gk * HEAD_DIM:(gk + 1) * HEAD_DIM], onehot], axis=1)
    kwin_ref[...] = kn[:, N_KVG:2 * N_KVG]
    o = N_Q + 2 * N_KVG
    for gk in range(NSA_KV_HEADS):
        vselT_ref[0, gk] = jnp.concatenate(
            [p2[o + gk * HEAD_DIM:o + (gk + 1) * HEAD_DIM],
             jnp.where(lax.broadcasted_iota(jnp.int32, (SEL_V_ROWS - HEAD_DIM, PROJ_ROWS), 0) == 0, 1.0, 0.0)],
            axis=0).astype(BF16)
    vw = p2[o + N_KVG:o + 2 * N_KVG].astype(BF16)
    for j in range(PROJ_ROWS // Q_BLOCK):
        vwinT_ref[j] = vw[:, j * Q_BLOCK:(j + 1) * Q_BLOCK]
    o = o + 2 * N_KVG
    gT_ref[...] = jax.nn.sigmoid(p2[o:o + N_GATE_PAD])


def _project(x, positions, attn_norm_w, w_in):
    T, D = x.shape
    R = PROJ_ROWS
    o1, o2, o3 = N_Q, N_Q + 6 * N_KVG, N_Q + 6 * N_KVG + N_GATE
    kv = lambda i: w_in[:, o1 + i * N_KVG:o1 + (i + 1) * N_KVG]
    sb = lambda i: w_in[:, o3 + i * N_SBH:o3 + (i + 1) * N_SBH]
    w_nat = jnp.concatenate([kv(0), kv(1), sb(1)], axis=1).astype(BF16)
    w_t = jnp.concatenate(
        [w_in[:, :o1], kv(2), kv(4), kv(3), kv(5), w_in[:, o2:o3],
         jnp.zeros((D, N_GATE_PAD - N_GATE), w_in.dtype), sb(0), sb(2)], axis=1).T.astype(BF16)
    inv_freq = ROPE_THETA ** (-jnp.arange(0, ROPE_DIM, 2, dtype=F32) / ROPE_DIM)
    n_nat, n_t = w_nat.shape[1], w_t.shape[0]
    full = lambda shape: pl.BlockSpec(shape, lambda i: (0,) * len(shape))
    rows = lambda n: pl.BlockSpec((R, n), lambda i: (i, 0))
    cols = lambda n: pl.BlockSpec((n, R), lambda i: (0, i))
    return pl.pallas_call(
        _proj_kernel,
        grid=(T // R,),
        in_specs=[rows(D), full((1, D)), full((D, n_nat)), full((n_t, D)), cols(1), full((ROPE_HALF, 1))],
        out_specs=[rows(N_KVG), rows(N_KVG), cols(N_SBH), rows(N_SBH),
                   pl.BlockSpec((R // Q_BLOCK, N_SBH, Q_BLOCK), lambda i: (i, 0, 0)),
                   cols(N_Q), pl.BlockSpec((NSA_KV_HEADS, R, N_KVG), lambda i: (0, i, 0)), rows(N_KVG),
                   pl.BlockSpec((1, NSA_KV_HEADS, SEL_V_ROWS, R), lambda i: (i, 0, 0, 0)),
                   pl.BlockSpec((R // Q_BLOCK, N_KVG, Q_BLOCK), lambda i: (i, 0, 0)),
                   cols(N_GATE_PAD)],
        out_shape=[jax.ShapeDtypeStruct((T, N_KVG), F32), jax.ShapeDtypeStruct((T, N_KVG), F32),
                   jax.ShapeDtypeStruct((N_SBH, T), BF16), jax.ShapeDtypeStruct((T, N_SBH), BF16),
                   jax.ShapeDtypeStruct((T // Q_BLOCK, N_SBH, Q_BLOCK), BF16),
                   jax.ShapeDtypeStruct((N_Q, T), BF16),
                   jax.ShapeDtypeStruct((NSA_KV_HEADS, T, N_KVG), BF16), jax.ShapeDtypeStruct((T, N_KVG), BF16),
                   jax.ShapeDtypeStruct((T // R, NSA_KV_HEADS, SEL_V_ROWS, R), BF16),
                   jax.ShapeDtypeStruct((T // Q_BLOCK, N_KVG, Q_BLOCK), BF16),
                   jax.ShapeDtypeStruct((N_GATE_PAD, T), F32)],
        compiler_params=pltpu.CompilerParams(dimension_semantics=("arbitrary",),
                                             vmem_limit_bytes=VMEM_LIMIT),
    )(x, attn_norm_w.reshape(1, D), w_nat, w_t, positions.reshape(1, T), inv_freq.reshape(ROPE_HALF, 1))


def _compress_kernel(x_ref, pea_ref, peb_ref, wa_ref, wb_ref, w2_ref, nat_ref, tr_ref):
    x = x_ref[0]
    nc = x.shape[0]
    ha = _dot((x + pea_ref[0]).astype(BF16), wa_ref[0])
    hb = _dot((x + peb_ref[0]).astype(BF16), wb_ref[0])
    hid = ha + pltpu.roll(hb, nc - 1, 0)
    act = (hid * jax.nn.sigmoid(hid)).astype(BF16)
    out = _dot(act, w2_ref[0])
    nat_ref[0] = out.astype(BF16)
    tr_ref[0] = out.T.astype(BF16)


def _compress(cmpk, cmpv, pe_k, pe_v, k_w1, k_w2, v_w1, v_w2):
    T = cmpk.shape[0]
    nc = T // CMP_STRIDE
    half = CMP_LEN // 2
    G = NSA_KV_HEADS
    width = half * N_KVG
    x = jnp.stack([cmpk.reshape(nc, width), cmpv.reshape(nc, width)])
    eye = jnp.eye(G, dtype=F32)

    def pe_rows(pe):
        return jnp.broadcast_to(pe[:, None, :], (half, G, HEAD_DIM)).reshape(1, width)

    def w1_block(w1):
        w = w1.reshape(half, HEAD_DIM, CMP_HIDDEN)
        return jnp.einsum('ldj,gh->lgdhj', w, eye).reshape(width, G * CMP_HIDDEN).astype(BF16)

    def w2_block(w2):
        return jnp.einsum('jd,gh->gjhd', w2, eye).reshape(G * CMP_HIDDEN, N_KVG).astype(BF16)

    hw = half * HEAD_DIM
    pea = jnp.stack([pe_rows(pe_k[:half]), pe_rows(pe_v[:half])])
    peb = jnp.stack([pe_rows(pe_k[half:]), pe_rows(pe_v[half:])])
    wa = jnp.stack([w1_block(k_w1[:hw]), w1_block(v_w1[:hw])])
    wb = jnp.stack([w1_block(k_w1[hw:]), w1_block(v_w1[hw:])])
    w2 = jnp.stack([w2_block(k_w2), w2_block(v_w2)])
    blk = lambda a, b: pl.BlockSpec((1, a, b), lambda i: (i, 0, 0))
    nat, tr = pl.pallas_call(
        _compress_kernel,
        grid=(2,),
        in_specs=[blk(nc, width), blk(1, width), blk(1, width), blk(width, G * CMP_HIDDEN),
                  blk(width, G * CMP_HIDDEN), blk(G * CMP_HIDDEN, N_KVG)],
        out_specs=[blk(nc, N_KVG), blk(N_KVG, nc)],
        out_shape=[jax.ShapeDtypeStruct((2, nc, N_KVG), BF16), jax.ShapeDtypeStruct((2, N_KVG, nc), BF16)],
        compiler_params=pltpu.CompilerParams(dimension_semantics=("arbitrary",),
                                             vmem_limit_bytes=VMEM_LIMIT),
    )(x, pea, peb, wa, wb, w2)
    return nat[0], tr[1]


def _nsa_kernel(q_ref, kc_ref, vcT_ref, ovT_ref, ksel_ref, vselT_ref, *rest):
    kwin_refs = rest[0:WIN_TILES]
    vwin_refs = rest[WIN_TILES:2 * WIN_TILES]
    gT_ref, o_ref = rest[2 * WIN_TILES:2 * WIN_TILES + 2]
    scratch = rest[2 * WIN_TILES + 2:]
    G = NSA_KV_HEADS
    per = len(scratch) // G
    bias_scr, s0_scr, s1_scr, t0_scr, t1_scr, m_scr, acc_scr, oc_scr = (
        [scratch[g * per + n] for g in range(G)] for n in range(per))
    groups = range(G)
    i = pl.program_id(0)
    R, Q = NSA_GROUP, Q_BLOCK
    L = R * Q
    nsel = bias_scr[0].shape[0]

    zero = jnp.zeros((HEAD_DIM, L), BF16)
    qg, qz = [], []
    for g in groups:
        q = jnp.concatenate([q_ref[(g * R + r) * HEAD_DIM:(g * R + r + 1) * HEAD_DIM, :] for r in range(R)], axis=1)
        qg.append(q)
        qz.append(jnp.concatenate([q if gg == g else zero for gg in groups], axis=0))

    def own_rows(x, g):
        return x[g * HEAD_DIM:(g + 1) * HEAD_DIM]

    t_lane = i * Q + (lax.broadcasted_iota(jnp.int32, (1, L), 1) & (Q - 1))

    nc = kc_ref.shape[0]
    q_pos = i * Q + lax.broadcasted_iota(jnp.int32, (1, Q), 1)
    bq = (q_pos >> 6).astype(F32)

    def compress_and_select(n_eff, first_class):
        m_eff = n_eff // (SEL_LEN // CMP_STRIDE)
        cmp_end = lax.broadcasted_iota(jnp.int32, (n_eff, 1), 0) * CMP_STRIDE + (CMP_LEN - 1)
        visible = cmp_end <= t_lane
        sc = [_dot(kc_ref[0:n_eff, :], qz[g]) for g in groups]
        pc = []
        for g in groups:
            s = jnp.where(visible, sc[g], NEG)
            mxc = jnp.max(s, axis=0, keepdims=True)
            mxc = jnp.where(mxc < 0.5 * NEG, 0.0, mxc)
            ec = jnp.exp2(s - mxc)
            pc.append(ec * (1.0 / jnp.maximum(jnp.sum(ec, axis=0, keepdims=True), 1e-30)))
        for g in groups:
            oc_scr[g][...] = own_rows(_dot(vcT_ref[:, 0:n_eff], pc[g].astype(BF16)), g)
        ov = ovT_ref[0:m_eff, 0:n_eff]
        imp = []
        for g in groups:
            psum = pc[g][:, 0:Q]
            for r in range(1, R):
                psum = psum + pc[g][:, r * Q:(r + 1) * Q]
            imp.append(sum(_dot(ov, part) for part in _split3(psum)))
        m_idx = lax.broadcasted_iota(jnp.int32, (m_eff, Q), 0).astype(F32)
        allowed = m_idx <= bq
        forced = (m_idx == 0.0) | (m_idx == bq) | (m_idx == bq - 1.0)
        if first_class:
            score = [jnp.where(allowed, imp[g] + jnp.where(forced, SEL_BONUS, 0.0), NEG) for g in groups]
            n_pick = min(SEL_TOP, m_eff)
        else:
            free = allowed & jnp.logical_not(forced)
            score = [jnp.where(free, imp[g], NEG) for g in groups]
            n_pick = SEL_TOP - 3
        for _ in range(n_pick):
            for g in groups:
                best = jnp.max(score[g], axis=0, keepdims=True)
                first = jnp.min(jnp.where(score[g] == best, m_idx, float(m_eff)), axis=0, keepdims=True)
                score[g] = jnp.where(m_idx == first, LOWEST, score[g])
        for g in groups:
            picked = score[g] < 0.5 * LOWEST
            if not first_class:
                picked = picked | forced
            bias = jnp.where(allowed & picked, 0.0, NEG)
            bias_scr[g][0:m_eff, :] = jnp.concatenate([bias] * R, axis=1)
            if m_eff < nsel:
                bias_scr[g][m_eff:nsel, :] = jnp.full((nsel - m_eff, L), NEG, F32)

    sizes = sorted({max(Q, (nc * k // CMP_CLASSES) // Q * Q) for k in range(1, CMP_CLASSES + 1)})
    lo = 0
    for n_eff in sizes:
        hi = n_eff // (Q // CMP_STRIDE)
        pl.when((i >= lo) & (i < hi))(functools.partial(compress_and_select, n_eff, lo == 0))
        lo = hi

    for g in groups:
        m_scr[g][...] = jnp.full(m_scr[g].shape, NEG, F32)
        acc_scr[g][...] = jnp.zeros(acc_scr[g].shape, F32)
    rhs_pad = jnp.zeros((N_KVG - HEAD_DIM - BIAS_ROWS, L), BF16)
    bias_pad = jnp.zeros((BIAS_ROWS - SEL_BLOCKS_PER_TILE, L), F32)
    buf0, buf1 = (s0_scr, t0_scr), (s1_scr, t1_scr)

    def scores(kt, dst):
        for g in groups:
            brows = bias_scr[g][pl.ds(pl.multiple_of(kt * SEL_BLOCKS_PER_TILE, SEL_BLOCKS_PER_TILE),
                                      SEL_BLOCKS_PER_TILE), :]
            rhs = jnp.concatenate([qg[g], jnp.concatenate([brows, bias_pad], axis=0).astype(BF16), rhs_pad], axis=0)
            s = _dot(ksel_ref[g, kt], rhs)
            dst[0][g][...] = s
            dst[1][g][...] = jnp.max(s, axis=0, keepdims=True)

    def absorb(kt, src):
        for g in groups:
            m_old = m_scr[g][...]
            m_new = jnp.maximum(m_old, src[1][g][...])
            alpha = jnp.exp2(m_old - m_new)
            p = jnp.exp2(src[0][g][...] - m_new)
            acc_scr[g][...] = alpha * acc_scr[g][...] + _dot(vselT_ref[kt, g], p.astype(BF16))
            m_scr[g][...] = m_new

    def step(kt, src, dst):
        scores(kt + 1, dst)
        absorb(kt, src)

    def two_steps(kt):
        step(kt, buf0, buf1)
        step(kt + 1, buf1, buf0)

    k_row = lax.broadcasted_iota(jnp.int32, (Q, 1), 0)
    q_lane = lax.broadcasted_iota(jnp.int32, (1, L), 1) & (Q - 1)
    n_full = (i * Q) // SEL_TILE

    def last_tile(src):
        r0 = pl.multiple_of(i * Q - n_full * SEL_TILE, Q)
        for g in groups:
            src[0][g][pl.ds(r0, Q), :] = jnp.where(k_row <= q_lane, src[0][g][pl.ds(r0, Q), :], NEG)
            src[1][g][...] = jnp.max(src[0][g][...], axis=0, keepdims=True)
        absorb(n_full, src)

    odd = n_full & 1

    @pl.when(odd == 0)
    def _():
        scores(0, buf0)

    @pl.when(odd == 1)
    def _():
        scores(0, buf1)
        step(0, buf1, buf0)

    n_pairs = n_full >> 1

    def four_steps(j, carry):
        two_steps(odd + 4 * j)
        two_steps(odd + 4 * j + 2)
        return carry

    lax.fori_loop(0, n_pairs >> 1, four_steps, 0)

    @pl.when((n_pairs & 1) == 1)
    def _():
        two_steps(odd + 4 * (n_pairs >> 1))

    last_tile(buf0)

    sw = [[] for _ in groups]
    for w in range(WIN_TILES):
        j = i - (WIN_TILES - 1) + w
        for g in groups:
            s = _dot(kwin_refs[w][0], qz[g])
            if w == 0:
                s = jnp.where(k_row > q_lane, s, NEG)
            if w == WIN_TILES - 1:
                s = jnp.where(k_row <= q_lane, s, NEG)
            else:
                s = jnp.where(j >= 0, s, NEG)
            sw[g].append(s)
    ew, denw = [], []
    for g in groups:
        mxw = functools.reduce(jnp.maximum, [jnp.max(s, axis=0, keepdims=True) for s in sw[g]])
        ew.append([jnp.exp2(s - mxw) for s in sw[g]])
        denw.append(jnp.maximum(sum(jnp.sum(e, axis=0, keepdims=True) for e in ew[g]), 1e-30))
    owT = []
    for g in groups:
        prod = sum(_dot(vwin_refs[w][0], ew[g][w].astype(BF16)) for w in range(WIN_TILES))
        owT.append(own_rows(prod, g) * (1.0 / denw[g]))

    def gate_row(g, j):
        return jnp.concatenate([gT_ref[(g * R + r) * 3 + j:(g * R + r) * 3 + j + 1, :] for r in range(R)], axis=1)

    o_rows = []
    for g in groups:
        osT = acc_scr[g][0:HEAD_DIM, :] * (1.0 / acc_scr[g][HEAD_DIM:HEAD_DIM + 1, :])
        oT = gate_row(g, 0) * oc_scr[g][...] + gate_row(g, 1) * osT + gate_row(g, 2) * owT[g]
        o_rows += [oT[:, r * Q:(r + 1) * Q] for r in range(R)]
    o_ref[...] = jnp.concatenate(o_rows, axis=0).T


def _nsa(qT, kc, vcT, ksel, vselT, kwin, vwinT, gT):
    T = qT.shape[1]
    nb = T // Q_BLOCK
    nc = kc.shape[0]
    nsel = T // SEL_LEN
    ntile = T // SEL_TILE
    n = jnp.arange(nc)[None, :] * CMP_STRIDE
    m = jnp.arange(nsel)[:, None] * SEL_LEN
    ovT = ((n < m + SEL_LEN) & (n + CMP_LEN > m)).astype(BF16)
    ksel4 = ksel.reshape(NSA_KV_HEADS, ntile, SEL_TILE, N_KVG)
    kwin3 = kwin.reshape(nb, Q_BLOCK, N_KVG)
    L = NSA_GROUP * Q_BLOCK
    const = lambda shape: pl.BlockSpec(shape, lambda i: (0,) * len(shape))
    win_tile = lambda i, w: (jnp.maximum(i - (WIN_TILES - 1) + w, 0), 0, 0)
    kwin_specs = [pl.BlockSpec((1, Q_BLOCK, N_KVG), functools.partial(win_tile, w=w)) for w in range(WIN_TILES)]
    vwin_specs = [pl.BlockSpec((1, N_KVG, Q_BLOCK), functools.partial(win_tile, w=w)) for w in range(WIN_TILES)]
    group_scratch = [pltpu.VMEM((nsel, L), F32), pltpu.VMEM((SEL_TILE, L), F32), pltpu.VMEM((SEL_TILE, L), F32),
                     pltpu.VMEM((1, L), F32), pltpu.VMEM((1, L), F32),
                     pltpu.VMEM((1, L), F32), pltpu.VMEM((SEL_V_ROWS, L), F32), pltpu.VMEM((HEAD_DIM, L), F32)]
    return pl.pallas_call(
        _nsa_kernel,
        grid=(nb,),
        in_specs=[pl.BlockSpec((N_Q, Q_BLOCK), lambda i: (0, i)),
                  const((nc, N_KVG)), const((N_KVG, nc)), const((nsel, nc)),
                  const((NSA_KV_HEADS, ntile, SEL_TILE, N_KVG)), const((ntile, NSA_KV_HEADS, SEL_V_ROWS, SEL_TILE))]
                 + kwin_specs + vwin_specs
                 + [pl.BlockSpec((N_GATE_PAD, Q_BLOCK), lambda i: (0, i))],
        out_specs=pl.BlockSpec((Q_BLOCK, N_Q), lambda i: (i, 0)),
        out_shape=jax.ShapeDtypeStruct((T, N_Q), F32),
        scratch_shapes=group_scratch * NSA_KV_HEADS,
        compiler_params=pltpu.CompilerParams(dimension_semantics=("arbitrary",),
                                             vmem_limit_bytes=VMEM_LIMIT),
    )(qT, kc, vcT, ovT, ksel4, vselT, *([kwin3] * WIN_TILES), *([vwinT] * WIN_TILES), gT)


def _sb_kernel(qT_ref, k_ref, vT_ref, o_ref, *scr):
    i = pl.program_id(0)
    Q, W = Q_BLOCK, 2 * HEAD_DIM
    pairs = SB_HEADS // 2
    blk_row = lax.broadcasted_iota(jnp.int32, (W, 2 * Q), 0) < HEAD_DIM
    blk_lane = lax.broadcasted_iota(jnp.int32, (W, 2 * Q), 1) < Q
    zero = jnp.zeros((), BF16)
    q_pairs = []
    for pr in range(pairs):
        x = qT_ref[pr * W:(pr + 1) * W, :]
        q_pairs.append(jnp.where(blk_row == blk_lane, jnp.concatenate([x, x], axis=1), zero))
    k_row = lax.broadcasted_iota(jnp.int32, (SB_TILE, 1), 0)
    q_lane = lax.broadcasted_iota(jnp.int32, (1, SB_HEADS * Q), 1) & (Q - 1)
    rr = lax.broadcasted_iota(jnp.int32, (SB_TILE, SB_TILE), 0)
    cc = lax.broadcasted_iota(jnp.int32, (SB_TILE, SB_TILE), 1)
    from_here = (cc >= rr).astype(BF16)
    acc_scrs, c_scrs = scr[0:pairs], scr[pairs:pairs + 2]
    for buf in scr:
        buf[...] = jnp.zeros(buf.shape, F32)

    halves = ((0, 1), (2, 3))
    causal = k_row < q_lane[:, 0:4 * Q]

    def walk(tiles):
        rows = [pl.ds(pl.multiple_of(kt * SB_TILE, SB_TILE), SB_TILE) for kt, _, _ in tiles]
        chains = [(t, hf) for t in range(len(tiles)) for hf in range(2)]
        z, d, suffix = {}, {}, {}
        for t, hf in chains:
            z[t, hf] = jnp.concatenate(
                [_dot(k_ref[rows[t], pr * W:(pr + 1) * W], q_pairs[pr]) for pr in halves[hf]], axis=1)
        for t, hf in chains:
            x = jnp.maximum(z[t, hf], 0.0) + jnp.log(1.0 + jnp.exp(-jnp.abs(z[t, hf])))
            if tiles[t][1]:
                x = jnp.where(causal, x, 0.0)
            if tiles[t][2] is not None:
                x = jnp.where(tiles[t][2], x, 0.0)
            d[t, hf] = x
            hi = x.astype(BF16)
            lo = (x - hi.astype(F32)).astype(BF16)
            suffix[t, hf] = _dot(from_here, hi) + _dot(from_here, lo)
        least = None
        for hf in range(2):
            c = c_scrs[hf][...]
            for t, (kt, own, exists) in enumerate(tiles):
                a = jnp.exp(z[t, hf] - suffix[t, hf] - c)
                if own:
                    a = jnp.where(causal, a, 0.0)
                if exists is not None:
                    a = jnp.where(exists, a, 0.0)
                a = a.astype(BF16)
                for n, pr in enumerate(halves[hf]):
                    acc_scrs[pr][...] = acc_scrs[pr][...] + _dot(vT_ref[kt, pr * W:(pr + 1) * W, :],
                                                                 a[:, n * 2 * Q:(n + 1) * 2 * Q])
                c = c + jnp.sum(d[t, hf], axis=0, keepdims=True)
            c_scrs[hf][...] = c
            least = c if least is None else jnp.minimum(least, c)
        return -jnp.min(least)

    first = [(i, True, None)] + [(jnp.maximum(i - n, 0), False, i - n >= 0) for n in range(1, SB_FIRST_TILES)]
    worst0 = walk(first)

    def cond(carry):
        kt, worst = carry
        return (kt >= 0) & (worst >= SB_SKIP_LOG)

    def body(carry):
        kt, _ = carry
        return kt - 1, walk([(kt, False, None)])

    lax.while_loop(cond, body, (i - SB_FIRST_TILES, worst0))
    for pr in range(pairs):
        acc = acc_scrs[pr][...]
        o_ref[:, pr * W:(pr + 1) * W] = jnp.where(blk_row[:, 0:Q], acc[:, 0:Q], acc[:, Q:2 * Q]).T


def _stick_breaking(sbqT, sbk, sbvT):
    T = sbk.shape[0]
    nb = T // Q_BLOCK
    W = 2 * HEAD_DIM
    return pl.pallas_call(
        _sb_kernel,
        grid=(nb,),
        in_specs=[pl.BlockSpec((N_SBH, Q_BLOCK), lambda i: (0, i)),
                  pl.BlockSpec((T, N_SBH), lambda i: (0, 0)),
                  pl.BlockSpec((nb, N_SBH, Q_BLOCK), lambda i: (0, 0, 0))],
        out_specs=pl.BlockSpec((Q_BLOCK, N_SBH), lambda i: (i, 0)),
        out_shape=jax.ShapeDtypeStruct((T, N_SBH), F32),
        scratch_shapes=[pltpu.VMEM((W, 2 * Q_BLOCK), F32)] * (SB_HEADS // 2) + [pltpu.VMEM((1, 4 * Q_BLOCK), F32)] * 2,
        compiler_params=pltpu.CompilerParams(dimension_semantics=("arbitrary",),
                                             vmem_limit_bytes=VMEM_LIMIT),
    )(sbqT, sbk, sbvT)


def _mix_kernel(x_ref, on_ref, os_ref, nwn_ref, nws_ref, wo_ref, fw_ref, rhi_ref, rlo_ref, rb_ref,
                x1_ref, h2_ref, lg_ref, lgT_ref):
    n1 = _rms(on_ref[...], nwn_ref[...]).astype(BF16)
    n2 = _rms(os_ref[...], nws_ref[...]).astype(BF16)
    x1 = x_ref[...] + _dot(n1, wo_ref[0:N_Q]) + _dot(n2, wo_ref[N_Q:N_Q + N_SBH])
    x1_ref[...] = x1
    h2 = _rms(x1, fw_ref[...])
    hi = h2.astype(BF16)
    lo = (h2 - hi.astype(F32)).astype(BF16)
    h2_ref[...] = hi
    lg = _dot(hi, rhi_ref[...]) + _dot(hi, rlo_ref[...]) + _dot(lo, rhi_ref[...]) + rb_ref[...]
    lg_ref[...] = lg
    lgT_ref[...] = lg.T[0:ROUTER_ROWS]


def _mix(x, o_nsa, o_sb, nsa_norm_w, sb_norm_w, w_out, ffn_norm_w, rg_w, rg_b, re_w, re_b):
    T, D = x.shape
    R = PROJ_ROWS
    pad = ROUTER_LANES - N_GROUPS - N_EXPERTS
    wr = jnp.concatenate([rg_w, re_w, jnp.zeros((D, pad), F32)], axis=1)
    wr_hi = wr.astype(BF16)
    wr_lo = (wr - wr_hi.astype(F32)).astype(BF16)
    rb = jnp.concatenate([rg_b, re_b, jnp.zeros((pad,), F32)]).reshape(1, ROUTER_LANES)
    full = lambda shape: pl.BlockSpec(shape, lambda i: (0,) * len(shape))
    rows = lambda n: pl.BlockSpec((R, n), lambda i: (i, 0))
    return pl.pallas_call(
        _mix_kernel,
        grid=(T // R,),
        in_specs=[rows(D), rows(N_Q), rows(N_SBH), full((1, N_Q)), full((1, N_SBH)), full((N_Q + N_SBH, D)),
                  full((1, D)), full((D, ROUTER_LANES)), full((D, ROUTER_LANES)), full((1, ROUTER_LANES))],
        out_specs=[rows(D), rows(D), rows(ROUTER_LANES), pl.BlockSpec((ROUTER_ROWS, R), lambda i: (0, i))],
        out_shape=[jax.ShapeDtypeStruct((T, D), F32), jax.ShapeDtypeStruct((T, D), BF16),
                   jax.ShapeDtypeStruct((T, ROUTER_LANES), F32), jax.ShapeDtypeStruct((ROUTER_ROWS, T), F32)],
        compiler_params=pltpu.CompilerParams(dimension_semantics=("arbitrary",),
                                             vmem_limit_bytes=VMEM_LIMIT),
    )(x, o_nsa, o_sb, nsa_norm_w.reshape(1, N_Q), sb_norm_w.reshape(1, N_SBH), w_out.astype(BF16),
      ffn_norm_w.reshape(1, D), wr_hi, wr_lo, rb)


def _routing(lg, axis):
    pos_i = lax.broadcasted_iota(jnp.int32, lg.shape, axis)
    pos = pos_i.astype(F32)
    first_max = lambda v, mx: jnp.min(jnp.where(v == mx, pos, float(ROUTER_LANES)), axis=axis, keepdims=True)
    gl = jnp.where(pos_i < N_GROUPS, lg, -jnp.inf)
    gmax = jnp.max(gl, axis=axis, keepdims=True)
    grp = first_max(gl, gmax)
    g_gate = 1.0 / jnp.sum(jnp.exp(gl - gmax), axis=axis, keepdims=True)
    e_idx = pos_i - N_GROUPS
    e_grp = (e_idx >> 2).astype(F32)
    in_grp = (e_idx >= 0) & (e_idx < N_EXPERTS) & (e_grp == grp)
    el = jnp.where(in_grp, lg, -jnp.inf)
    top1 = jnp.max(el, axis=axis, keepdims=True)
    i1 = first_max(el, top1)
    el2 = jnp.where(pos == i1, -jnp.inf, el)
    top2 = jnp.max(el2, axis=axis, keepdims=True)
    i2 = first_max(el2, top2)
    e2 = jnp.exp(top2 - top1)
    w1 = 1.0 / (1.0 + e2)
    w2 = e2 / (1.0 + e2)
    weight = g_gate * (jnp.where(pos == i1, w1, 0.0) + jnp.where(pos == i2, w2, 0.0))
    routed = jnp.where(pos == i1, 1.0, 0.0) + jnp.where(pos == i2, 1.0, 0.0)
    return weight, routed


def _moe_kernel(h_ref, lg_ref, lgT_ref, x1_ref, before_ref, beforeT_ref, wg_ref, wu_ref, wd_ref, fw_ref, o_ref,
                acc_scr, rank_scr, cw_scr, rankT_scr):
    e = pl.program_id(1)
    rows = h_ref.shape[0]
    n_sub = rows // MOE_SUB
    cap = n_sub * MOE_SUB_CAP

    @pl.when(e == 0)
    def _():
        acc_scr[...] = jnp.zeros(acc_scr.shape, F32)
        weight, routed = _routing(lg_ref[...], 1)
        rank = _dot(beforeT_ref[...], routed.astype(BF16))
        rank_scr[...] = jnp.where(routed > 0.0, rank, -1.0)
        cw_scr[...] = weight
        _, routed_t = _routing(lgT_ref[...], 0)
        rank_t = _dot(routed_t.astype(BF16), before_ref[...])
        rankT_scr[...] = jnp.where(routed_t > 0.0, rank_t, -1.0)

    experts = range(MOE_EXPERTS_PER_STEP)
    lane = lax.broadcasted_iota(jnp.int32, (rows, ROUTER_LANES), 1)
    rank_all, cw_all = rank_scr[...], cw_scr[...]
    rank_col, w_col, rank_row = [], [], []
    for x in experts:
        pos = e * MOE_EXPERTS_PER_STEP + x + N_GROUPS
        mine = lane == pos
        rank_col.append(jnp.sum(jnp.where(mine, rank_all, 0.0), axis=1, keepdims=True))
        w_col.append(jnp.sum(jnp.where(mine, cw_all, 0.0), axis=1, keepdims=True))
        rank_row.append(rankT_scr[pl.ds(pos, 1), :])
    most = functools.reduce(jnp.maximum, rank_row)
    n_routed = jnp.max(most).astype(jnp.int32) + 1
    slot_col = lax.broadcasted_iota(jnp.int32, (MOE_SUB_CAP, 1), 0)
    slot_row = lax.broadcasted_iota(jnp.int32, (1, cap), 1).astype(F32)
    sub_first = ((lax.broadcasted_iota(jnp.int32, (rows, 1), 0) >> MOE_SUB_SHIFT) * MOE_SUB_CAP).astype(F32)

    def chunk(ch, carry):
        base = ch * MOE_SUB_CAP
        want = (slot_col + base).astype(F32)
        xg, scatter = [], []
        for x in experts:
            got = []
            for s in range(n_sub):
                pick = jnp.where(rank_row[x][:, s * MOE_SUB:(s + 1) * MOE_SUB] == want, 1.0, 0.0).astype(BF16)
                got.append(_dot(pick, h_ref[s * MOE_SUB:(s + 1) * MOE_SUB, :]))
            xg.append(jnp.concatenate(got, axis=0).astype(BF16))
            local = rank_col[x] - base.astype(F32)
            slot = jnp.where((local >= 0.0) & (local < float(MOE_SUB_CAP)), sub_first + local, -1.0)
            scatter.append(jnp.where(slot == slot_row, 1.0, 0.0).astype(BF16))
        a = [_dot(xg[x], wg_ref[x]) for x in experts]
        b = [_dot(xg[x], wu_ref[x]) for x in experts]
        act = [(a[x] * jax.nn.sigmoid(a[x]) * b[x]).astype(BF16) for x in experts]
        y = [_dot(act[x], wd_ref[x]).astype(BF16) for x in experts]
        out = [_dot(scatter[x], y[x]) * w_col[x] for x in experts]
        acc_scr[...] += functools.reduce(lambda p, q: p + q, out)
        return carry

    lax.fori_loop(0, (n_routed + (MOE_SUB_CAP - 1)) // MOE_SUB_CAP, chunk, 0)

    @pl.when(e == N_EXPERTS // MOE_EXPERTS_PER_STEP - 1)
    def _():
        o_ref[...] = _rms(x1_ref[...] + acc_scr[...], fw_ref[...])


def _moe(h2, logits, logitsT, x1, w_gate, w_up, w_down, final_norm_w):
    T, D = x1.shape
    R = min(MOE_ROWS, T)
    tok = jnp.arange(R)
    same_sub = (tok[:, None] >> MOE_SUB_SHIFT) == (tok[None, :] >> MOE_SUB_SHIFT)
    before = ((tok[:, None] < tok[None, :]) & same_sub).astype(BF16)
    rows = lambda n: pl.BlockSpec((R, n), lambda i, e: (i, 0))
    const = lambda a, b: pl.BlockSpec((a, b), lambda i, e: (0, 0))
    per_expert = lambda a, b: pl.BlockSpec((MOE_EXPERTS_PER_STEP, a, b), lambda i, e: (e, 0, 0))
    return pl.pallas_call(
        _moe_kernel,
        grid=(T // R, N_EXPERTS // MOE_EXPERTS_PER_STEP),
        in_specs=[rows(D), rows(ROUTER_LANES), pl.BlockSpec((ROUTER_ROWS, R), lambda i, e: (0, i)), rows(D),
                  const(R, R), const(R, R),
                  per_expert(D, EXPERT_FF), per_expert(D, EXPERT_FF), per_expert(EXPERT_FF, D), const(1, D)],
        out_specs=rows(D),
        out_shape=jax.ShapeDtypeStruct((T, D), F32),
        scratch_shapes=[pltpu.VMEM((R, D), F32), pltpu.VMEM((R, ROUTER_LANES), F32),
                        pltpu.VMEM((R, ROUTER_LANES), F32), pltpu.VMEM((ROUTER_ROWS, R), F32)],
        compiler_params=pltpu.CompilerParams(dimension_semantics=("arbitrary", "arbitrary"),
                                             vmem_limit_bytes=VMEM_LIMIT),
    )(h2, logits, logitsT, x1, before, before.T, w_gate.astype(BF16), w_up.astype(BF16), w_down.astype(BF16),
      final_norm_w.reshape(1, D))


def kernel(x, positions, attn_norm_w, w_in, cmp_pe_k, cmp_pe_v, cmp_k_w1, cmp_k_w2, cmp_v_w1, cmp_v_w2,
           nsa_out_norm_w, sb_out_norm_w, w_out, ffn_norm_w, router_group_w, router_group_b,
           router_expert_w, router_expert_b, w_gate, w_up, w_down, final_norm_w):
    B, T, D = x.shape
    assert B == 1 and T % SEL_TILE == 0 and T % PROJ_ROWS == 0 and T // SEL_LEN >= SEL_TOP
    assert attn_norm_w.shape[0] == 1, "the final norm is fused into the (single) layer's MoE kernel"
    xs = x.reshape(T, D)
    pos = positions.reshape(T)
    (cmpk, cmpv, sbq, sbk, sbv, qT, ksel, kwin, vselT, vwinT, gT) = _project(xs, pos, attn_norm_w[0], w_in[0])
    kc, vcT = _compress(cmpk, cmpv, cmp_pe_k[0], cmp_pe_v[0], cmp_k_w1[0], cmp_k_w2[0], cmp_v_w1[0], cmp_v_w2[0])
    o_nsa = _nsa(qT, kc, vcT, ksel, vselT, kwin, vwinT, gT)
    o_sb = _stick_breaking(sbq, sbk, sbv)
    x1, h2, logits, logitsT = _mix(xs, o_nsa, o_sb, nsa_out_norm_w[0], sb_out_norm_w[0], w_out[0], ffn_norm_w[0],
                                   router_group_w[0], router_group_b[0], router_expert_w[0], router_expert_b[0])
    out = _moe(h2, logits, logitsT, x1, w_gate[0], w_up[0], w_down[0], final_norm_w)
    return out.reshape(B, T, D)
```

```python
import functools

import jax
import jax.numpy as jnp
from jax import lax
from jax.experimental import pallas as pl
from jax.experimental.pallas import tpu as pltpu

HEAD_DIM = 64
NSA_HEADS = 8
NSA_KV_HEADS = 2
NSA_GROUP = NSA_HEADS // NSA_KV_HEADS
SB_HEADS = 8
ROPE_THETA = 500000.0
ROPE_DIM = HEAD_DIM // 4
ROPE_HALF = ROPE_DIM // 2
CMP_LEN = 32
CMP_STRIDE = 16
CMP_HIDDEN = 256
SEL_LEN = 64
SEL_TOP = 16
SEL_BONUS = 1.0e4
WINDOW = 512
Q_BLOCK = 128
N_GROUPS = 4
EXPERTS_PER_GROUP = 4
N_EXPERTS = N_GROUPS * EXPERTS_PER_GROUP
EXPERT_FF = 512
EPS = 1e-6
NEG = -1e30
LOWEST = -3.0e38

N_Q = NSA_HEADS * HEAD_DIM
N_KVG = NSA_KV_HEADS * HEAD_DIM
N_GATE = NSA_HEADS * 3
N_GATE_PAD = 32
N_SBH = SB_HEADS * HEAD_DIM
SCALE = HEAD_DIM ** -0.5
LOG2E = 1.4426950408889634
SEL_BLOCKS_PER_TILE = 8
BIAS_ROWS = 16
CMP_CLASSES = 8
SEL_V_ROWS = HEAD_DIM + 16

PROJ_ROWS = 512
SEL_TILE = 512
WIN_TILES = WINDOW // Q_BLOCK + 1
SB_TILE = 128
MOE_ROWS = 1024
MOE_EXPERTS_PER_STEP = 2
MOE_SLOT_ROWS = 128
MOE_SUB_SHIFT = 8
MOE_SUB = 1 << MOE_SUB_SHIFT
MOE_SUB_CAP = 48
ROUTER_LANES = 128
ROUTER_ROWS = 32
SB_FIRST_TILES = 3
SB_SKIP_LOG = -104.0

VMEM_LIMIT = 56 * 1024 * 1024

BF16 = jnp.bfloat16
F32 = jnp.float32


def _rms(x, w):
    return x * lax.rsqrt(jnp.mean(x * x, axis=-1, keepdims=True) + EPS) * w


def _dot(a, b):
    return jnp.dot(a, b, preferred_element_type=F32)


def _dot_nt(a, b):
    return lax.dot_general(a, b, (((1,), (1,)), ((), ())), preferred_element_type=F32)


def _split3(x):
    hi = x.astype(BF16)
    r1 = x - hi.astype(F32)
    mid = r1.astype(BF16)
    lo = (r1 - mid.astype(F32)).astype(BF16)
    return hi, mid, lo


def _proj_kernel(x_ref, nw_ref, wn_ref, wt_ref, pos_ref, invf_ref,
                 cmpk_ref, cmpv_ref, sbqT_ref, sbk_ref, sbvT_ref,
                 qT_ref, ksel_ref, kwin_ref, vselT_ref, vwinT_ref, gT_ref):
    h = _rms(x_ref[...], nw_ref[...]).astype(BF16)
    p1 = _dot(h, wn_ref[...])
    cmpk_ref[...] = p1[:, 0:N_KVG]
    cmpv_ref[...] = p1[:, N_KVG:2 * N_KVG]
    sbk_ref[...] = p1[:, 2 * N_KVG:2 * N_KVG + N_SBH].astype(BF16)

    p2 = _dot_nt(wt_ref[...], h)
    o = N_Q + 4 * N_KVG + N_GATE_PAD
    sbqT_ref[...] = (p2[o:o + N_SBH] * SCALE).astype(BF16)
    sbv = p2[o + N_SBH:o + 2 * N_SBH].astype(BF16)
    for j in range(PROJ_ROWS // Q_BLOCK):
        sbvT_ref[j] = sbv[:, j * Q_BLOCK:(j + 1) * Q_BLOCK]
    ang = invf_ref[...] * pos_ref[...].astype(F32)
    cos, sin = jnp.cos(ang), jnp.sin(ang)
    n_rope_heads = NSA_HEADS + 2 * NSA_KV_HEADS
    roped = []
    for hd in range(n_rope_heads):
        blk = p2[hd * HEAD_DIM:(hd + 1) * HEAD_DIM]
        x1, x2 = blk[0:ROPE_HALF], blk[ROPE_HALF:ROPE_DIM]
        roped.append(jnp.concatenate(
            [x1 * cos - x2 * sin, x2 * cos + x1 * sin, blk[ROPE_DIM:]], axis=0))
    qT_ref[...] = (jnp.concatenate(roped[:NSA_HEADS], axis=0) * (SCALE * LOG2E)).astype(BF16)
    kT = jnp.concatenate(roped[NSA_HEADS:], axis=0)
    kn = kT.T.astype(BF16)
    r_blk = lax.broadcasted_iota(jnp.int32, (PROJ_ROWS, HEAD_DIM), 0) >> 6
    c_idx = lax.broadcasted_iota(jnp.int32, (PROJ_ROWS, HEAD_DIM), 1)
    onehot = jnp.where(r_blk == c_idx, 1.0, 0.0).astype(BF16)
    for gk in range(NSA_KV_HEADS):
        ksel_ref[gk] = jnp.concatenate([kn[:, gk * HEAD_DIM:(gk + 1) * HEAD_DIM], onehot], axis=1)
    kwin_ref[...] = kn[:, N_KVG:2 * N_KVG]
    o = N_Q + 2 * N_KVG
    for gk in range(NSA_KV_HEADS):
        vselT_ref[0, gk] = jnp.concatenate(
            [p2[o + gk * HEAD_DIM:o + (gk + 1) * HEAD_DIM],
             jnp.where(lax.broadcasted_iota(jnp.int32, (SEL_V_ROWS - HEAD_DIM, PROJ_ROWS), 0) == 0, 1.0, 0.0)],
            axis=0).astype(BF16)
    vw = p2[o + N_KVG:o + 2 * N_KVG].astype(BF16)
    for j in range(PROJ_ROWS // Q_BLOCK):
        vwinT_ref[j] = vw[:, j * Q_BLOCK:(j + 1) * Q_BLOCK]
    o = o + 2 * N_KVG
    gT_ref[...] = jax.nn.sigmoid(p2[o:o + N_GATE_PAD])


def _project(x, positions, attn_norm_w, w_in):
    T, D = x.shape
    R = PROJ_ROWS
    o1, o2, o3 = N_Q, N_Q + 6 * N_KVG, N_Q + 6 * N_KVG + N_GATE
    kv = lambda i: w_in[:, o1 + i * N_KVG:o1 + (i + 1) * N_KVG]
    sb = lambda i: w_in[:, o3 + i * N_SBH:o3 + (i + 1) * N_SBH]
    w_nat = jnp.concatenate([kv(0), kv(1), sb(1)], axis=1).astype(BF16)
    w_t = jnp.concatenate(
        [w_in[:, :o1], kv(2), kv(4), kv(3), kv(5), w_in[:, o2:o3],
         jnp.zeros((D, N_GATE_PAD - N_GATE), w_in.dtype), sb(0), sb(2)], axis=1).T.astype(BF16)
    inv_freq = ROPE_THETA ** (-jnp.arange(0, ROPE_DIM, 2, dtype=F32) / ROPE_DIM)
    n_nat, n_t = w_nat.shape[1], w_t.shape[0]
    full = lambda shape: pl.BlockSpec(shape, lambda i: (0,) * len(shape))
    rows = lambda n: pl.BlockSpec((R, n), lambda i: (i, 0))
    cols = lambda n: pl.BlockSpec((n, R), lambda i: (0, i))
    return pl.pallas_call(
        _proj_kernel,
        grid=(T // R,),
        in_specs=[rows(D), full((1, D)), full((D, n_nat)), full((n_t, D)), cols(1), full((ROPE_HALF, 1))],
        out_specs=[rows(N_KVG), rows(N_KVG), cols(N_SBH), rows(N_SBH),
                   pl.BlockSpec((R // Q_BLOCK, N_SBH, Q_BLOCK), lambda i: (i, 0, 0)),
                   cols(N_Q), pl.BlockSpec((NSA_KV_HEADS, R, N_KVG), lambda i: (0, i, 0)), rows(N_KVG),
                   pl.BlockSpec((1, NSA_KV_HEADS, SEL_V_ROWS, R), lambda i: (i, 0, 0, 0)),
                   pl.BlockSpec((R // Q_BLOCK, N_KVG, Q_BLOCK), lambda i: (i, 0, 0)),
                   cols(N_GATE_PAD)],
        out_shape=[jax.ShapeDtypeStruct((T, N_KVG), F32), jax.ShapeDtypeStruct((T, N_KVG), F32),
                   jax.ShapeDtypeStruct((N_SBH, T), BF16), jax.ShapeDtypeStruct((T, N_SBH), BF16),
                   jax.ShapeDtypeStruct((T // Q_BLOCK, N_SBH, Q_BLOCK), BF16),
                   jax.ShapeDtypeStruct((N_Q, T), BF16),
                   jax.ShapeDtypeStruct((NSA_KV_HEADS, T, N_KVG), BF16), jax.ShapeDtypeStruct((T, N_KVG), BF16),
                   jax.ShapeDtypeStruct((T // R, NSA_KV_HEADS, SEL_V_ROWS, R), BF16),
                   jax.ShapeDtypeStruct((T // Q_BLOCK, N_KVG, Q_BLOCK), BF16),
                   jax.ShapeDtypeStruct((N_GATE_PAD, T), F32)],
        compiler_params=pltpu.CompilerParams(dimension_semantics=("arbitrary",),
                                             vmem_limit_bytes=VMEM_LIMIT),
    )(x, attn_norm_w.reshape(1, D), w_nat, w_t, positions.reshape(1, T), inv_freq.reshape(ROPE_HALF, 1))


def _compress_kernel(x_ref, pea_ref, peb_ref, wa_ref, wb_ref, w2_ref, nat_ref, tr_ref):
    x = x_ref[0]
    nc = x.shape[0]
    ha = _dot((x + pea_ref[0]).astype(BF16), wa_ref[0])
    hb = _dot((x + peb_ref[0]).astype(BF16), wb_ref[0])
    hid = ha + pltpu.roll(hb, nc - 1, 0)
    act = (hid * jax.nn.sigmoid(hid)).astype(BF16)
    out = _dot(act, w2_ref[0])
    nat_ref[0] = out.astype(BF16)
    tr_ref[0] = out.T.astype(BF16)


def _compress(cmpk, cmpv, pe_k, pe_v, k_w1, k_w2, v_w1, v_w2):
    T = cmpk.shape[0]
    nc = T // CMP_STRIDE
    half = CMP_LEN // 2
    G = NSA_KV_HEADS
    width = half * N_KVG
    x = jnp.stack([cmpk.reshape(nc, width), cmpv.reshape(nc, width)])
    eye = jnp.eye(G, dtype=F32)

    def pe_rows(pe):
        return jnp.broadcast_to(pe[:, None, :], (half, G, HEAD_DIM)).reshape(1, width)

    def w1_block(w1):
        w = w1.reshape(half, HEAD_DIM, CMP_HIDDEN)
        return jnp.einsum('ldj,gh->lgdhj', w, eye).reshape(width, G * CMP_HIDDEN).astype(BF16)

    def w2_block(w2):
        return jnp.einsum('jd,gh->gjhd', w2, eye).reshape(G * CMP_HIDDEN, N_KVG).astype(BF16)

    hw = half * HEAD_DIM
    pea = jnp.stack([pe_rows(pe_k[:half]), pe_rows(pe_v[:half])])
    peb = jnp.stack([pe_rows(pe_k[half:]), pe_rows(pe_v[half:])])
    wa = jnp.stack([w1_block(k_w1[:hw]), w1_block(v_w1[:hw])])
    wb = jnp.stack([w1_block(k_w1[hw:]), w1_block(v_w1[hw:])])
    w2 = jnp.stack([w2_block(k_w2), w2_block(v_w2)])
    blk = lambda a, b: pl.BlockSpec((1, a, b), lambda i: (i, 0, 0))
    nat, tr = pl.pallas_call(
        _compress_kernel,
        grid=(2,),
        in_specs=[blk(nc, width), blk(1, width), blk(1, width), blk(width, G * CMP_HIDDEN),
                  blk(width, G * CMP_HIDDEN), blk(G * CMP_HIDDEN, N_KVG)],
        out_specs=[blk(nc, N_KVG), blk(N_KVG, nc)],
        out_shape=[jax.ShapeDtypeStruct((2, nc, N_KVG), BF16), jax.ShapeDtypeStruct((2, N_KVG, nc), BF16)],
        compiler_params=pltpu.CompilerParams(dimension_semantics=("arbitrary",),
                                             vmem_limit_bytes=VMEM_LIMIT),
    )(x, pea, peb, wa, wb, w2)
    return nat[0], tr[1]


def _nsa_kernel(q_ref, kc_ref, vcT_ref, ovT_ref, ksel_ref, vselT_ref, *rest):
    kwin_refs = rest[0:WIN_TILES]
    vwin_refs = rest[WIN_TILES:2 * WIN_TILES]
    gT_ref, o_ref = rest[2 * WIN_TILES:2 * WIN_TILES + 2]
    scratch = rest[2 * WIN_TILES + 2:]
    G = NSA_KV_HEADS
    per = len(scratch) // G
    bias_scr, s0_scr, s1_scr, t0_scr, t1_scr, m_scr, acc_scr, oc_scr = (
        [scratch[g * per + n] for g in range(G)] for n in range(per))
    groups = range(G)
    i = pl.program_id(0)
    R, Q = NSA_GROUP, Q_BLOCK
    L = R * Q
    nsel = bias_scr[0].shape[0]

    zero = jnp.zeros((HEAD_DIM, L), BF16)
    qg, qz = [], []
    for g in groups:
        q = jnp.concatenate([q_ref[(g * R + r) * HEAD_DIM:(g * R + r + 1) * HEAD_DIM, :] for r in range(R)], axis=1)
        qg.append(q)
        qz.append(jnp.concatenate([q if gg == g else zero for gg in groups], axis=0))

    def own_rows(x, g):
        return x[g * HEAD_DIM:(g + 1) * HEAD_DIM]

    t_lane = i * Q + (lax.broadcasted_iota(jnp.int32, (1, L), 1) & (Q - 1))

    nc = kc_ref.shape[0]
    q_pos = i * Q + lax.broadcasted_iota(jnp.int32, (1, Q), 1)
    bq = (q_pos >> 6).astype(F32)

    def compress_and_select(n_eff, first_class):
        m_eff = n_eff // (SEL_LEN // CMP_STRIDE)
        cmp_end = lax.broadcasted_iota(jnp.int32, (n_eff, 1), 0) * CMP_STRIDE + (CMP_LEN - 1)
        visible = cmp_end <= t_lane
        sc = [_dot(kc_ref[0:n_eff, :], qz[g]) for g in groups]
        pc = []
        for g in groups:
            s = jnp.where(visible, sc[g], NEG)
            mxc = jnp.max(s, axis=0, keepdims=True)
            mxc = jnp.where(mxc < 0.5 * NEG, 0.0, mxc)
            ec = jnp.exp2(s - mxc)
            pc.append(ec * (1.0 / jnp.maximum(jnp.sum(ec, axis=0, keepdims=True), 1e-30)))
        for g in groups:
            oc_scr[g][...] = own_rows(_dot(vcT_ref[:, 0:n_eff], pc[g].astype(BF16)), g)
        ov = ovT_ref[0:m_eff, 0:n_eff]
        imp = []
        for g in groups:
            psum = pc[g][:, 0:Q]
            for r in range(1, R):
                psum = psum + pc[g][:, r * Q:(r + 1) * Q]
            imp.append(sum(_dot(ov, part) for part in _split3(psum)))
        m_idx = lax.broadcasted_iota(jnp.int32, (m_eff, Q), 0).astype(F32)
        allowed = m_idx <= bq
        forced = (m_idx == 0.0) | (m_idx == bq) | (m_idx == bq - 1.0)
        if first_class:
            score = [jnp.where(allowed, imp[g] + jnp.where(forced, SEL_BONUS, 0.0), NEG) for g in groups]
            n_pick = min(SEL_TOP, m_eff)
        else:
            free = allowed & jnp.logical_not(forced)
            score = [jnp.where(free, imp[g], NEG) for g in groups]
            n_pick = SEL_TOP - 3
        for _ in range(n_pick):
            for g in groups:
                best = jnp.max(score[g], axis=0, keepdims=True)
                first = jnp.min(jnp.where(score[g] == best, m_idx, float(m_eff)), axis=0, keepdims=True)
                score[g] = jnp.where(m_idx == first, LOWEST, score[g])
        for g in groups:
            picked = score[g] < 0.5 * LOWEST
            if not first_class:
                picked = picked | forced
            bias = jnp.where(allowed & picked, 0.0, NEG)
            bias_scr[g][0:m_eff, :] = jnp.concatenate([bias] * R, axis=1)
            if m_eff < nsel:
                bias_scr[g][m_eff:nsel, :] = jnp.full((nsel - m_eff, L), NEG, F32)

    sizes = sorted({max(Q, (nc * k // CMP_CLASSES) // Q * Q) for k in range(1, CMP_CLASSES + 1)})
    lo = 0
    for n_eff in sizes:
        hi = n_eff // (Q // CMP_STRIDE)
        pl.when((i >= lo) & (i < hi))(functools.partial(compress_and_select, n_eff, lo == 0))
        lo = hi

    for g in groups:
        m_scr[g][...] = jnp.full(m_scr[g].shape, NEG, F32)
        acc_scr[g][...] = jnp.zeros(acc_scr[g].shape, F32)
    rhs_pad = jnp.zeros((N_KVG - HEAD_DIM - BIAS_ROWS, L), BF16)
    bias_pad = jnp.zeros((BIAS_ROWS - SEL_BLOCKS_PER_TILE, L), F32)
    buf0, buf1 = (s0_scr, t0_scr), (s1_scr, t1_scr)

    def scores(kt, dst):
        for g in groups:
            brows = bias_scr[g][pl.ds(pl.multiple_of(kt * SEL_BLOCKS_PER_TILE, SEL_BLOCKS_PER_TILE),
                                      SEL_BLOCKS_PER_TILE), :]
            rhs = jnp.concatenate([qg[g], jnp.concatenate([brows, bias_pad], axis=0).astype(BF16), rhs_pad], axis=0)
            s = _dot(ksel_ref[g, kt], rhs)
            dst[0][g][...] = s
            dst[1][g][...] = jnp.max(s, axis=0, keepdims=True)

    def absorb(kt, src):
        for g in groups:
            m_old = m_scr[g][...]
            m_new = jnp.maximum(m_old, src[1][g][...])
            alpha = jnp.exp2(m_old - m_new)
            p = jnp.exp2(src[0][g][...] - m_new)
            acc_scr[g][...] = alpha * acc_scr[g][...] + _dot(vselT_ref[kt, g], p.astype(BF16))
            m_scr[g][...] = m_new

    def step(kt, src, dst):
        scores(kt + 1, dst)
        absorb(kt, src)

    def two_steps(kt):
        step(kt, buf0, buf1)
        step(kt + 1, buf1, buf0)

    k_row = lax.broadcasted_iota(jnp.int32, (Q, 1), 0)
    q_lane = lax.broadcasted_iota(jnp.int32, (1, L), 1) & (Q - 1)
    n_full = (i * Q) // SEL_TILE

    def last_tile(src):
        r0 = pl.multiple_of(i * Q - n_full * SEL_TILE, Q)
        for g in groups:
            src[0][g][pl.ds(r0, Q), :] = jnp.where(k_row <= q_lane, src[0][g][pl.ds(r0, Q), :], NEG)
            src[1][g][...] = jnp.max(src[0][g][...], axis=0, keepdims=True)
        absorb(n_full, src)

    odd = n_full & 1

    @pl.when(odd == 0)
    def _():
        scores(0, buf0)

    @pl.when(odd == 1)
    def _():
        scores(0, buf1)
        step(0, buf1, buf0)

    n_pairs = n_full >> 1

    def four_steps(j, carry):
        two_steps(odd + 4 * j)
        two_steps(odd + 4 * j + 2)
        return carry

    lax.fori_loop(0, n_pairs >> 1, four_steps, 0)

    @pl.when((n_pairs & 1) == 1)
    def _():
        two_steps(odd + 4 * (n_pairs >> 1))

    last_tile(buf0)

    sw = [[] for _ in groups]
    for w in range(WIN_TILES):
        j = i - (WIN_TILES - 1) + w
        for g in groups:
            s = _dot(kwin_refs[w][0], qz[g])
            if w == 0:
                s = jnp.where(k_row > q_lane, s, NEG)
            if w == WIN_TILES - 1:
                s = jnp.where(k_row <= q_lane, s, NEG)
            else:
                s = jnp.where(j >= 0, s, NEG)
            sw[g].append(s)
    ew, denw = [], []
    for g in groups:
        mxw = functools.reduce(jnp.maximum, [jnp.max(s, axis=0, keepdims=True) for s in sw[g]])
        ew.append([jnp.exp2(s - mxw) for s in sw[g]])
        denw.append(jnp.maximum(sum(jnp.sum(e, axis=0, keepdims=True) for e in ew[g]), 1e-30))
    owT = []
    for g in groups:
        prod = sum(_dot(vwin_refs[w][0], ew[g][w].astype(BF16)) for w in range(WIN_TILES))
        owT.append(own_rows(prod, g) * (1.0 / denw[g]))

    def gate_row(g, j):
        return jnp.concatenate([gT_ref[(g * R + r) * 3 + j:(g * R + r) * 3 + j + 1, :] for r in range(R)], axis=1)

    o_rows = []
    for g in groups:
        osT = acc_scr[g][0:HEAD_DIM, :] * (1.0 / acc_scr[g][HEAD_DIM:HEAD_DIM + 1, :])
        oT = gate_row(g, 0) * oc_scr[g][...] + gate_row(g, 1) * osT + gate_row(g, 2) * owT[g]
        o_rows += [oT[:, r * Q:(r + 1) * Q] for r in range(R)]
    o_ref[...] = jnp.concatenate(o_rows, axis=0).T


def _nsa(qT, kc, vcT, ksel, vselT, kwin, vwinT, gT):
    T = qT.shape[1]
    nb = T // Q_BLOCK
    nc = kc.shape[0]
    nsel = T // SEL_LEN
    ntile = T // SEL_TILE
    n = jnp.arange(nc)[None, :] * CMP_STRIDE
    m = jnp.arange(nsel)[:, None] * SEL_LEN
    ovT = ((n < m + SEL_LEN) & (n + CMP_LEN > m)).astype(BF16)
    ksel4 = ksel.reshape(NSA_KV_HEADS, ntile, SEL_TILE, N_KVG)
    kwin3 = kwin.reshape(nb, Q_BLOCK, N_KVG)
    L = NSA_GROUP * Q_BLOCK
    const = lambda shape: pl.BlockSpec(shape, lambda i: (0,) * len(shape))
    win_tile = lambda i, w: (jnp.maximum(i - (WIN_TILES - 1) + w, 0), 0, 0)
    kwin_specs = [pl.BlockSpec((1, Q_BLOCK, N_KVG), functools.partial(win_tile, w=w)) for w in range(WIN_TILES)]
    vwin_specs = [pl.BlockSpec((1, N_KVG, Q_BLOCK), functools.partial(win_tile, w=w)) for w in range(WIN_TILES)]
    group_scratch = [pltpu.VMEM((nsel, L), F32), pltpu.VMEM((SEL_TILE, L), F32), pltpu.VMEM((SEL_TILE, L), F32),
                     pltpu.VMEM((1, L), F32), pltpu.VMEM((1, L), F32),
                     pltpu.VMEM((1, L), F32), pltpu.VMEM((SEL_V_ROWS, L), F32), pltpu.VMEM((HEAD_DIM, L), F32)]
    return pl.pallas_call(
        _nsa_kernel,
        grid=(nb,),
        in_specs=[pl.BlockSpec((N_Q, Q_BLOCK), lambda i: (0, i)),
                  const((nc, N_KVG)), const((N_KVG, nc)), const((nsel, nc)),
                  const((NSA_KV_HEADS, ntile, SEL_TILE, N_KVG)), const((ntile, NSA_KV_HEADS, SEL_V_ROWS, SEL_TILE))]
                 + kwin_specs + vwin_specs
                 + [pl.BlockSpec((N_GATE_PAD, Q_BLOCK), lambda i: (0, i))],
        out_specs=pl.BlockSpec((Q_BLOCK, N_Q), lambda i: (i, 0)),
        out_shape=jax.ShapeDtypeStruct((T, N_Q), F32),
        scratch_shapes=group_scratch * NSA_KV_HEADS,
        compiler_params=pltpu.CompilerParams(dimension_semantics=("arbitrary",),
                                             vmem_limit_bytes=VMEM_LIMIT),
    )(qT, kc, vcT, ovT, ksel4, vselT, *([kwin3] * WIN_TILES), *([vwinT] * WIN_TILES), gT)


def _sb_kernel(qT_ref, k_ref, vT_ref, o_ref, *scr):
    i = pl.program_id(0)
    Q, W = Q_BLOCK, 2 * HEAD_DIM
    pairs = SB_HEADS // 2
    blk_row = lax.broadcasted_iota(jnp.int32, (W, 2 * Q), 0) < HEAD_DIM
    blk_lane = lax.broadcasted_iota(jnp.int32, (W, 2 * Q), 1) < Q
    zero = jnp.zeros((), BF16)
    q_pairs = []
    for pr in range(pairs):
        x = qT_ref[pr * W:(pr + 1) * W, :]
        q_pairs.append(jnp.where(blk_row == blk_lane, jnp.concatenate([x, x], axis=1), zero))
    k_row = lax.broadcasted_iota(jnp.int32, (SB_TILE, 1), 0)
    q_lane = lax.broadcasted_iota(jnp.int32, (1, SB_HEADS * Q), 1) & (Q - 1)
    rr = lax.broadcasted_iota(jnp.int32, (SB_TILE, SB_TILE), 0)
    cc = lax.broadcasted_iota(jnp.int32, (SB_TILE, SB_TILE), 1)
    from_here = (cc >= rr).astype(BF16)
    acc_scrs, c_scrs = scr[0:pairs], scr[pairs:pairs + 2]
    for buf in scr:
        buf[...] = jnp.zeros(buf.shape, F32)

    halves = ((0, 1), (2, 3))
    causal = k_row < q_lane[:, 0:4 * Q]

    def walk(tiles):
        rows = [pl.ds(pl.multiple_of(kt * SB_TILE, SB_TILE), SB_TILE) for kt, _, _ in tiles]
        chains = [(t, hf) for t in range(len(tiles)) for hf in range(2)]
        z, d, suffix = {}, {}, {}
        for t, hf in chains:
            z[t, hf] = jnp.concatenate(
                [_dot(k_ref[rows[t], pr * W:(pr + 1) * W], q_pairs[pr]) for pr in halves[hf]], axis=1)
        for t, hf in chains:
            x = jnp.maximum(z[t, hf], 0.0) + jnp.log(1.0 + jnp.exp(-jnp.abs(z[t, hf])))
            if tiles[t][1]:
                x = jnp.where(causal, x, 0.0)
            if tiles[t][2] is not None:
                x = jnp.where(tiles[t][2], x, 0.0)
            d[t, hf] = x
            hi = x.astype(BF16)
            lo = (x - hi.astype(F32)).astype(BF16)
            suffix[t, hf] = _dot(from_here, hi) + _dot(from_here, lo)
        least = None
        for hf in range(2):
            c = c_scrs[hf][...]
            for t, (kt, own, exists) in enumerate(tiles):
                a = jnp.exp(z[t, hf] - suffix[t, hf] - c)
                if own:
                    a = jnp.where(causal, a, 0.0)
                if exists is not None:
                    a = jnp.where(exists, a, 0.0)
                a = a.astype(BF16)
                for n, pr in enumerate(halves[hf]):
                    acc_scrs[pr][...] = acc_scrs[pr][...] + _dot(vT_ref[kt, pr * W:(pr + 1) * W, :],
                                                                 a[:, n * 2 * Q:(n + 1) * 2 * Q])
                c = c + jnp.sum(d[t, hf], axis=0, keepdims=True)
            c_scrs[hf][...] = c
            least = c if least is None else jnp.minimum(least, c)
        return -jnp.min(least)

    first = [(i, True, None)] + [(jnp.maximum(i - n, 0), False, i - n >= 0) for n in range(1, SB_FIRST_TILES)]
    worst0 = walk(first)

    def cond(carry):
        kt, worst = carry
        return (kt >= 0) & (worst >= SB_SKIP_LOG)

    def body(carry):
        kt, _ = carry
        return kt - 1, walk([(kt, False, None)])

    lax.while_loop(cond, body, (i - SB_FIRST_TILES, worst0))
    for pr in range(pairs):
        acc = acc_scrs[pr][...]
        o_ref[:, pr * W:(pr + 1) * W] = jnp.where(blk_row[:, 0:Q], acc[:, 0:Q], acc[:, Q:2 * Q]).T


def _stick_breaking(sbqT, sbk, sbvT):
    T = sbk.shape[0]
    nb = T // Q_BLOCK
    W = 2 * HEAD_DIM
    return pl.pallas_call(
        _sb_kernel,
        grid=(nb,),
        in_specs=[pl.BlockSpec((N_SBH, Q_BLOCK), lambda i: (0, i)),
                  pl.BlockSpec((T, N_SBH), lambda i: (0, 0)),
                  pl.BlockSpec((nb, N_SBH, Q_BLOCK), lambda i: (0, 0, 0))],
        out_specs=pl.BlockSpec((Q_BLOCK, N_SBH), lambda i: (i, 0)),
        out_shape=jax.ShapeDtypeStruct((T, N_SBH), F32),
        scratch_shapes=[pltpu.VMEM((W, 2 * Q_BLOCK), F32)] * (SB_HEADS // 2) + [pltpu.VMEM((1, 4 * Q_BLOCK), F32)] * 2,
        compiler_params=pltpu.CompilerParams(dimension_semantics=("arbitrary",),
                                             vmem_limit_bytes=VMEM_LIMIT),
    )(sbqT, sbk, sbvT)


def _mix_kernel(x_ref, on_ref, os_ref, nwn_ref, nws_ref, wo_ref, fw_ref, rhi_ref, rlo_ref, rb_ref,
                x1_ref, h2_ref, lgT_ref):
    n1 = _rms(on_ref[...], nwn_ref[...]).astype(BF16)
    n2 = _rms(os_ref[...], nws_ref[...]).astype(BF16)
    x1 = x_ref[...] + _dot(n1, wo_ref[0:N_Q]) + _dot(n2, wo_ref[N_Q:N_Q + N_SBH])
    x1_ref[...] = x1
    h2 = _rms(x1, fw_ref[...])
    hi = h2.astype(BF16)
    lo = (h2 - hi.astype(F32)).astype(BF16)
    h2_ref[...] = hi
    lg = _dot(hi, rhi_ref[...]) + _dot(hi, rlo_ref[...]) + _dot(lo, rhi_ref[...]) + rb_ref[...]
    lgT_ref[...] = lg.T[0:ROUTER_ROWS]


def _mix(x, o_nsa, o_sb, nsa_norm_w, sb_norm_w, w_out, ffn_norm_w, rg_w, rg_b, re_w, re_b):
    T, D = x.shape
    R = PROJ_ROWS
    pad = ROUTER_LANES - N_GROUPS - N_EXPERTS
    wr = jnp.concatenate([rg_w, re_w, jnp.zeros((D, pad), F32)], axis=1)
    wr_hi = wr.astype(BF16)
    wr_lo = (wr - wr_hi.astype(F32)).astype(BF16)
    rb = jnp.concatenate([rg_b, re_b, jnp.zeros((pad,), F32)]).reshape(1, ROUTER_LANES)
    full = lambda shape: pl.BlockSpec(shape, lambda i: (0,) * len(shape))
    rows = lambda n: pl.BlockSpec((R, n), lambda i: (i, 0))
    return pl.pallas_call(
        _mix_kernel,
        grid=(T // R,),
        in_specs=[rows(D), rows(N_Q), rows(N_SBH), full((1, N_Q)), full((1, N_SBH)), full((N_Q + N_SBH, D)),
                  full((1, D)), full((D, ROUTER_LANES)), full((D, ROUTER_LANES)), full((1, ROUTER_LANES))],
        out_specs=[rows(D), rows(D), pl.BlockSpec((ROUTER_ROWS, R), lambda i: (0, i))],
        out_shape=[jax.ShapeDtypeStruct((T, D), F32), jax.ShapeDtypeStruct((T, D), BF16),
                   jax.ShapeDtypeStruct((ROUTER_ROWS, T), F32)],
        compiler_params=pltpu.CompilerParams(dimension_semantics=("arbitrary",),
                                             vmem_limit_bytes=VMEM_LIMIT),
    )(x, o_nsa, o_sb, nsa_norm_w.reshape(1, N_Q), sb_norm_w.reshape(1, N_SBH), w_out.astype(BF16),
      ffn_norm_w.reshape(1, D), wr_hi, wr_lo, rb)


def _routing(lg, axis):
    pos_i = lax.broadcasted_iota(jnp.int32, lg.shape, axis)
    pos = pos_i.astype(F32)
    first_max = lambda v, mx: jnp.min(jnp.where(v == mx, pos, float(ROUTER_LANES)), axis=axis, keepdims=True)
    gl = jnp.where(pos_i < N_GROUPS, lg, -jnp.inf)
    gmax = jnp.max(gl, axis=axis, keepdims=True)
    grp = first_max(gl, gmax)
    g_gate = 1.0 / jnp.sum(jnp.exp(gl - gmax), axis=axis, keepdims=True)
    e_idx = pos_i - N_GROUPS
    e_grp = (e_idx >> 2).astype(F32)
    in_grp = (e_idx >= 0) & (e_idx < N_EXPERTS) & (e_grp == grp)
    el = jnp.where(in_grp, lg, -jnp.inf)
    top1 = jnp.max(el, axis=axis, keepdims=True)
    i1 = first_max(el, top1)
    el2 = jnp.where(pos == i1, -jnp.inf, el)
    top2 = jnp.max(el2, axis=axis, keepdims=True)
    i2 = first_max(el2, top2)
    e2 = jnp.exp(top2 - top1)
    w1 = 1.0 / (1.0 + e2)
    w2 = e2 / (1.0 + e2)
    weight = g_gate * (jnp.where(pos == i1, w1, 0.0) + jnp.where(pos == i2, w2, 0.0))
    routed = jnp.where(pos == i1, 1.0, 0.0) + jnp.where(pos == i2, 1.0, 0.0)
    return weight, routed


def _moe_kernel(h_ref, lgT_ref, x1_ref, before_ref, wg_ref, wu_ref, wd_ref, fw_ref, o_ref,
                acc_scr, rankT_scr, wparts_scr):
    e = pl.program_id(1)
    rows, width = h_ref.shape
    n_sub = rows // MOE_SUB

    @pl.when(e == 0)
    def _():
        acc_scr[...] = jnp.zeros(acc_scr.shape, F32)
        weight, routed = _routing(lgT_ref[...], 0)
        rank = _dot(routed.astype(BF16), before_ref[...])
        rankT_scr[...] = jnp.where(routed > 0.0, rank, -1.0)
        parts = [p.astype(F32) for p in _split3(weight)]
        pad = jnp.zeros((MOE_SLOT_ROWS - 3 * ROUTER_ROWS, rows), F32)
        wparts_scr[...] = jnp.concatenate(parts + [pad], axis=0).T.astype(BF16)

    experts = range(MOE_EXPERTS_PER_STEP)
    pos = [e * MOE_EXPERTS_PER_STEP + x + N_GROUPS for x in experts]
    rank_row = [rankT_scr[pl.ds(pos[x], 1), :] for x in experts]
    n_routed = jnp.max(functools.reduce(jnp.maximum, rank_row)).astype(jnp.int32) + 1
    slot_col = lax.broadcasted_iota(jnp.int32, (MOE_SLOT_ROWS, 1), 0)
    lane = lax.broadcasted_iota(jnp.int32, (n_sub * MOE_SUB_CAP, ROUTER_LANES), 1)
    y_pad = jnp.zeros((MOE_SLOT_ROWS - MOE_SUB_CAP, width), BF16)

    def chunk(ch, carry):
        want = jnp.where(slot_col < MOE_SUB_CAP, slot_col + ch * MOE_SUB_CAP, -2).astype(F32)
        xg, w_slot, spread = [], [], []
        for x in experts:
            got, got_w, back = [], [], []
            for s in range(n_sub):
                sub = slice(s * MOE_SUB, (s + 1) * MOE_SUB)
                onehot = jnp.where(rank_row[x][:, sub] == want, 1.0, 0.0)
                pick = onehot[0:MOE_SUB_CAP].astype(BF16)
                got.append(_dot(pick, h_ref[sub, :]))
                got_w.append(_dot(pick, wparts_scr[sub, :]))
                back.append(onehot.T.astype(BF16))
            xg.append(jnp.concatenate(got, axis=0).astype(BF16))
            mine = (lane == pos[x]) | (lane == pos[x] + ROUTER_ROWS) | (lane == pos[x] + 2 * ROUTER_ROWS)
            w_slot.append(jnp.sum(jnp.where(mine, jnp.concatenate(got_w, axis=0), 0.0), axis=1, keepdims=True))
            spread.append(back)
        a = [_dot(xg[x], wg_ref[x]) for x in experts]
        b = [_dot(xg[x], wu_ref[x]) for x in experts]
        act = [(a[x] * jax.nn.sigmoid(a[x]) * b[x]).astype(BF16) for x in experts]
        y = [(_dot(act[x], wd_ref[x]) * w_slot[x]).astype(BF16) for x in experts]
        for s in range(n_sub):
            out = [_dot(spread[x][s],
                        jnp.concatenate([y[x][s * MOE_SUB_CAP:(s + 1) * MOE_SUB_CAP], y_pad], axis=0))
                   for x in experts]
            acc_scr[s * MOE_SUB:(s + 1) * MOE_SUB, :] += functools.reduce(lambda p, q: p + q, out)
        return carry

    lax.fori_loop(0, (n_routed + (MOE_SUB_CAP - 1)) // MOE_SUB_CAP, chunk, 0)

    @pl.when(e == N_EXPERTS // MOE_EXPERTS_PER_STEP - 1)
    def _():
        o_ref[...] = _rms(x1_ref[...] + acc_scr[...], fw_ref[...])


def _moe(h2, logitsT, x1, w_gate, w_up, w_down, final_norm_w):
    T, D = x1.shape
    R = min(MOE_ROWS, T)
    tok = jnp.arange(R)
    same_sub = (tok[:, None] >> MOE_SUB_SHIFT) == (tok[None, :] >> MOE_SUB_SHIFT)
    before = ((tok[:, None] < tok[None, :]) & same_sub).astype(BF16)
    rows = lambda n: pl.BlockSpec((R, n), lambda i, e: (i, 0))
    const = lambda a, b: pl.BlockSpec((a, b), lambda i, e: (0, 0))
    per_expert = lambda a, b: pl.BlockSpec((MOE_EXPERTS_PER_STEP, a, b), lambda i, e: (e, 0, 0))
    return pl.pallas_call(
        _moe_kernel,
        grid=(T // R, N_EXPERTS // MOE_EXPERTS_PER_STEP),
        in_specs=[rows(D), pl.BlockSpec((ROUTER_ROWS, R), lambda i, e: (0, i)), rows(D), const(R, R),
                  per_expert(D, EXPERT_FF), per_expert(D, EXPERT_FF), per_expert(EXPERT_FF, D), const(1, D)],
        out_specs=rows(D),
        out_shape=jax.ShapeDtypeStruct((T, D), F32),
        scratch_shapes=[pltpu.VMEM((R, D), F32), pltpu.VMEM((ROUTER_ROWS, R), F32),
                        pltpu.VMEM((R, ROUTER_LANES), BF16)],
        compiler_params=pltpu.CompilerParams(dimension_semantics=("arbitrary", "arbitrary"),
                                             vmem_limit_bytes=VMEM_LIMIT),
    )(h2, logitsT, x1, before, w_gate.astype(BF16), w_up.astype(BF16), w_down.astype(BF16),
      final_norm_w.reshape(1, D))


def kernel(x, positions, attn_norm_w, w_in, cmp_pe_k, cmp_pe_v, cmp_k_w1, cmp_k_w2, cmp_v_w1, cmp_v_w2,
           nsa_out_norm_w, sb_out_norm_w, w_out, ffn_norm_w, router_group_w, router_group_b,
           router_expert_w, router_expert_b, w_gate, w_up, w_down, final_norm_w):
    B, T, D = x.shape
    assert B == 1 and T % SEL_TILE == 0 and T % PROJ_ROWS == 0 and T // SEL_LEN >= SEL_TOP
    assert attn_norm_w.shape[0] == 1, "the final norm is fused into the (single) layer's MoE kernel"
    xs = x.reshape(T, D)
    pos = positions.reshape(T)
    (cmpk, cmpv, sbq, sbk, sbv, qT, ksel, kwin, vselT, vwinT, gT) = _project(xs, pos, attn_norm_w[0], w_in[0])
    kc, vcT = _compress(cmpk, cmpv, cmp_pe_k[0], cmp_pe_v[0], cmp_k_w1[0], cmp_k_w2[0], cmp_v_w1[0], cmp_v_w2[0])
    o_nsa = _nsa(qT, kc, vcT, ksel, vselT, kwin, vwinT, gT)
    o_sb = _stick_breaking(sbq, sbk, sbv)
    x1, h2, logitsT = _mix(xs, o_nsa, o_sb, nsa_out_norm_w[0], sb_out_norm_w[0], w_out[0], ffn_norm_w[0],
                           router_group_w[0], router_group_b[0], router_expert_w[0], router_expert_b[0])
    out = _moe(h2, logitsT, x1, w_gate[0], w_up[0], w_down[0], final_norm_w)
    return out.reshape(B, T, D)
```

```python
import functools

import jax
import jax.numpy as jnp
from jax import lax
from jax.experimental import pallas as pl
from jax.experimental.pallas import tpu as pltpu

HEAD_DIM = 64
NSA_HEADS = 8
NSA_KV_HEADS = 2
NSA_GROUP = NSA_HEADS // NSA_KV_HEADS
SB_HEADS = 8
ROPE_THETA = 500000.0
ROPE_DIM = HEAD_DIM // 4
ROPE_HALF = ROPE_DIM // 2
CMP_LEN = 32
CMP_STRIDE = 16
CMP_HIDDEN = 256
SEL_LEN = 64
SEL_TOP = 16
SEL_BONUS = 1.0e4
WINDOW = 512
Q_BLOCK = 128
N_GROUPS = 4
EXPERTS_PER_GROUP = 4
N_EXPERTS = N_GROUPS * EXPERTS_PER_GROUP
EXPERT_FF = 512
EPS = 1e-6
NEG = -1e30
LOWEST = -3.0e38

N_Q = NSA_HEADS * HEAD_DIM
N_KVG = NSA_KV_HEADS * HEAD_DIM
N_GATE = NSA_HEADS * 3
N_GATE_PAD = 32
N_SBH = SB_HEADS * HEAD_DIM
SCALE = HEAD_DIM ** -0.5
LOG2E = 1.4426950408889634
SEL_BLOCKS_PER_TILE = 8
BIAS_ROWS = 16
CMP_CLASSES = 8
SEL_V_ROWS = HEAD_DIM + 16

PROJ_ROWS = 512
SEL_TILE = 512
WIN_TILES = WINDOW // Q_BLOCK + 1
SB_TILE = 128
MOE_ROWS = 2048
MOE_EXPERTS_PER_STEP = 2
MOE_SLOT_ROWS = 128
MOE_SUB_SHIFT = 8
MOE_SUB = 1 << MOE_SUB_SHIFT
MOE_SUB_CAP = 48
ROUTER_LANES = 128
ROUTER_ROWS = 32
SB_FIRST_TILES = 3
SB_SKIP_LOG = -104.0

VMEM_LIMIT = 56 * 1024 * 1024

BF16 = jnp.bfloat16
F32 = jnp.float32


def _rms(x, w):
    return x * lax.rsqrt(jnp.mean(x * x, axis=-1, keepdims=True) + EPS) * w


def _dot(a, b):
    return jnp.dot(a, b, preferred_element_type=F32)


def _dot_nt(a, b):
    return lax.dot_general(a, b, (((1,), (1,)), ((), ())), preferred_element_type=F32)


def _split3(x):
    hi = x.astype(BF16)
    r1 = x - hi.astype(F32)
    mid = r1.astype(BF16)
    lo = (r1 - mid.astype(F32)).astype(BF16)
    return hi, mid, lo


def _proj_kernel(x_ref, nw_ref, wn_ref, wt_ref, pos_ref, invf_ref,
                 cmpk_ref, cmpv_ref, sbqT_ref, sbk_ref, sbvT_ref,
                 qT_ref, ksel_ref, kwin_ref, vselT_ref, vwinT_ref, gT_ref):
    h = _rms(x_ref[...], nw_ref[...]).astype(BF16)
    p1 = _dot(h, wn_ref[...])
    cmpk_ref[...] = p1[:, 0:N_KVG]
    cmpv_ref[...] = p1[:, N_KVG:2 * N_KVG]
    sbk_ref[...] = p1[:, 2 * N_KVG:2 * N_KVG + N_SBH].astype(BF16)

    p2 = _dot_nt(wt_ref[...], h)
    o = N_Q + 4 * N_KVG + N_GATE_PAD
    sbqT_ref[...] = (p2[o:o + N_SBH] * SCALE).astype(BF16)
    sbv = p2[o + N_SBH:o + 2 * N_SBH].astype(BF16)
    for j in range(PROJ_ROWS // Q_BLOCK):
        sbvT_ref[j] = sbv[:, j * Q_BLOCK:(j + 1) * Q_BLOCK]
    ang = invf_ref[...] * pos_ref[...].astype(F32)
    cos, sin = jnp.cos(ang), jnp.sin(ang)
    n_rope_heads = NSA_HEADS + 2 * NSA_KV_HEADS
    roped = []
    for hd in range(n_rope_heads):
        blk = p2[hd * HEAD_DIM:(hd + 1) * HEAD_DIM]
        x1, x2 = blk[0:ROPE_HALF], blk[ROPE_HALF:ROPE_DIM]
        roped.append(jnp.concatenate(
            [x1 * cos - x2 * sin, x2 * cos + x1 * sin, blk[ROPE_DIM:]], axis=0))
    qT_ref[...] = (jnp.concatenate(roped[:NSA_HEADS], axis=0) * (SCALE * LOG2E)).astype(BF16)
    kT = jnp.concatenate(roped[NSA_HEADS:], axis=0)
    kn = kT.T.astype(BF16)
    r_blk = lax.broadcasted_iota(jnp.int32, (PROJ_ROWS, HEAD_DIM), 0) >> 6
    c_idx = lax.broadcasted_iota(jnp.int32, (PROJ_ROWS, HEAD_DIM), 1)
    onehot = jnp.where(r_blk == c_idx, 1.0, 0.0).astype(BF16)
    for gk in range(NSA_KV_HEADS):
        ksel_ref[gk] = jnp.concatenate([kn[:, gk * HEAD_DIM:(gk + 1) * HEAD_DIM], onehot], axis=1)
    kwin_ref[...] = kn[:, N_KVG:2 * N_KVG]
    o = N_Q + 2 * N_KVG
    for gk in range(NSA_KV_HEADS):
        vselT_ref[0, gk] = jnp.concatenate(
            [p2[o + gk * HEAD_DIM:o + (gk + 1) * HEAD_DIM],
             jnp.where(lax.broadcasted_iota(jnp.int32, (SEL_V_ROWS - HEAD_DIM, PROJ_ROWS), 0) == 0, 1.0, 0.0)],
            axis=0).astype(BF16)
    vw = p2[o + N_KVG:o + 2 * N_KVG].astype(BF16)
    for j in range(PROJ_ROWS // Q_BLOCK):
        vwinT_ref[j] = vw[:, j * Q_BLOCK:(j + 1) * Q_BLOCK]
    o = o + 2 * N_KVG
    gT_ref[...] = jax.nn.sigmoid(p2[o:o + N_GATE_PAD])


def _project(x, positions, attn_norm_w, w_in):
    T, D = x.shape
    R = PROJ_ROWS
    o1, o2, o3 = N_Q, N_Q + 6 * N_KVG, N_Q + 6 * N_KVG + N_GATE
    kv = lambda i: w_in[:, o1 + i * N_KVG:o1 + (i + 1) * N_KVG]
    sb = lambda i: w_in[:, o3 + i * N_SBH:o3 + (i + 1) * N_SBH]
    w_nat = jnp.concatenate([kv(0), kv(1), sb(1)], axis=1).astype(BF16)
    w_t = jnp.concatenate(
        [w_in[:, :o1], kv(2), kv(4), kv(3), kv(5), w_in[:, o2:o3],
         jnp.zeros((D, N_GATE_PAD - N_GATE), w_in.dtype), sb(0), sb(2)], axis=1).T.astype(BF16)
    inv_freq = ROPE_THETA ** (-jnp.arange(0, ROPE_DIM, 2, dtype=F32) / ROPE_DIM)
    n_nat, n_t = w_nat.shape[1], w_t.shape[0]
    full = lambda shape: pl.BlockSpec(shape, lambda i: (0,) * len(shape))
    rows = lambda n: pl.BlockSpec((R, n), lambda i: (i, 0))
    cols = lambda n: pl.BlockSpec((n, R), lambda i: (0, i))
    return pl.pallas_call(
        _proj_kernel,
        grid=(T // R,),
        in_specs=[rows(D), full((1, D)), full((D, n_nat)), full((n_t, D)), cols(1), full((ROPE_HALF, 1))],
        out_specs=[rows(N_KVG), rows(N_KVG), cols(N_SBH), rows(N_SBH),
                   pl.BlockSpec((R // Q_BLOCK, N_SBH, Q_BLOCK), lambda i: (i, 0, 0)),
                   cols(N_Q), pl.BlockSpec((NSA_KV_HEADS, R, N_KVG), lambda i: (0, i, 0)), rows(N_KVG),
                   pl.BlockSpec((1, NSA_KV_HEADS, SEL_V_ROWS, R), lambda i: (i, 0, 0, 0)),
                   pl.BlockSpec((R // Q_BLOCK, N_KVG, Q_BLOCK), lambda i: (i, 0, 0)),
                   cols(N_GATE_PAD)],
        out_shape=[jax.ShapeDtypeStruct((T, N_KVG), F32), jax.ShapeDtypeStruct((T, N_KVG), F32),
                   jax.ShapeDtypeStruct((N_SBH, T), BF16), jax.ShapeDtypeStruct((T, N_SBH), BF16),
                   jax.ShapeDtypeStruct((T // Q_BLOCK, N_SBH, Q_BLOCK), BF16),
                   jax.ShapeDtypeStruct((N_Q, T), BF16),
                   jax.ShapeDtypeStruct((NSA_KV_HEADS, T, N_KVG), BF16), jax.ShapeDtypeStruct((T, N_KVG), BF16),
                   jax.ShapeDtypeStruct((T // R, NSA_KV_HEADS, SEL_V_ROWS, R), BF16),
                   jax.ShapeDtypeStruct((T // Q_BLOCK, N_KVG, Q_BLOCK), BF16),
                   jax.ShapeDtypeStruct((N_GATE_PAD, T), F32)],
        compiler_params=pltpu.CompilerParams(dimension_semantics=("arbitrary",),
                                             vmem_limit_bytes=VMEM_LIMIT),
    )(x, attn_norm_w.reshape(1, D), w_nat, w_t, positions.reshape(1, T), inv_freq.reshape(ROPE_HALF, 1))


def _compress_kernel(x_ref, pea_ref, peb_ref, wa_ref, wb_ref, w2_ref, nat_ref, tr_ref):
    x = x_ref[0]
    nc = x.shape[0]
    ha = _dot((x + pea_ref[0]).astype(BF16), wa_ref[0])
    hb = _dot((x + peb_ref[0]).astype(BF16), wb_ref[0])
    hid = ha + pltpu.roll(hb, nc - 1, 0)
    act = (hid * jax.nn.sigmoid(hid)).astype(BF16)
    out = _dot(act, w2_ref[0])
    nat_ref[0] = out.astype(BF16)
    tr_ref[0] = out.T.astype(BF16)


def _compress(cmpk, cmpv, pe_k, pe_v, k_w1, k_w2, v_w1, v_w2):
    T = cmpk.shape[0]
    nc = T // CMP_STRIDE
    half = CMP_LEN // 2
    G = NSA_KV_HEADS
    width = half * N_KVG
    x = jnp.stack([cmpk.reshape(nc, width), cmpv.reshape(nc, width)])
    eye = jnp.eye(G, dtype=F32)

    def pe_rows(pe):
        return jnp.broadcast_to(pe[:, None, :], (half, G, HEAD_DIM)).reshape(1, width)

    def w1_block(w1):
        w = w1.reshape(half, HEAD_DIM, CMP_HIDDEN)
        return jnp.einsum('ldj,gh->lgdhj', w, eye).reshape(width, G * CMP_HIDDEN).astype(BF16)

    def w2_block(w2):
        return jnp.einsum('jd,gh->gjhd', w2, eye).reshape(G * CMP_HIDDEN, N_KVG).astype(BF16)

    hw = half * HEAD_DIM
    pea = jnp.stack([pe_rows(pe_k[:half]), pe_rows(pe_v[:half])])
    peb = jnp.stack([pe_rows(pe_k[half:]), pe_rows(pe_v[half:])])
    wa = jnp.stack([w1_block(k_w1[:hw]), w1_block(v_w1[:hw])])
    wb = jnp.stack([w1_block(k_w1[hw:]), w1_block(v_w1[hw:])])
    w2 = jnp.stack([w2_block(k_w2), w2_block(v_w2)])
    blk = lambda a, b: pl.BlockSpec((1, a, b), lambda i: (i, 0, 0))
    nat, tr = pl.pallas_call(
        _compress_kernel,
        grid=(2,),
        in_specs=[blk(nc, width), blk(1, width), blk(1, width), blk(width, G * CMP_HIDDEN),
                  blk(width, G * CMP_HIDDEN), blk(G * CMP_HIDDEN, N_KVG)],
        out_specs=[blk(nc, N_KVG), blk(N_KVG, nc)],
        out_shape=[jax.ShapeDtypeStruct((2, nc, N_KVG), BF16), jax.ShapeDtypeStruct((2, N_KVG, nc), BF16)],
        compiler_params=pltpu.CompilerParams(dimension_semantics=("arbitrary",),
                                             vmem_limit_bytes=VMEM_LIMIT),
    )(x, pea, peb, wa, wb, w2)
    return nat[0], tr[1]


def _nsa_kernel(q_ref, kc_ref, vcT_ref, ovT_ref, ksel_ref, vselT_ref, *rest):
    kwin_refs = rest[0:WIN_TILES]
    vwin_refs = rest[WIN_TILES:2 * WIN_TILES]
    gT_ref, o_ref = rest[2 * WIN_TILES:2 * WIN_TILES + 2]
    scratch = rest[2 * WIN_TILES + 2:]
    G = NSA_KV_HEADS
    per = len(scratch) // G
    bias_scr, s0_scr, s1_scr, t0_scr, t1_scr, m_scr, acc_scr, oc_scr = (
        [scratch[g * per + n] for g in range(G)] for n in range(per))
    groups = range(G)
    i = pl.program_id(0)
    R, Q = NSA_GROUP, Q_BLOCK
    L = R * Q
    nsel = bias_scr[0].shape[0]

    zero = jnp.zeros((HEAD_DIM, L), BF16)
    qg, qz = [], []
    for g in groups:
        q = jnp.concatenate([q_ref[(g * R + r) * HEAD_DIM:(g * R + r + 1) * HEAD_DIM, :] for r in range(R)], axis=1)
        qg.append(q)
        qz.append(jnp.concatenate([q if gg == g else zero for gg in groups], axis=0))

    def own_rows(x, g):
        return x[g * HEAD_DIM:(g + 1) * HEAD_DIM]

    t_lane = i * Q + (lax.broadcasted_iota(jnp.int32, (1, L), 1) & (Q - 1))

    nc = kc_ref.shape[0]
    q_pos = i * Q + lax.broadcasted_iota(jnp.int32, (1, Q), 1)
    bq = (q_pos >> 6).astype(F32)

    def compress_and_select(n_eff, first_class):
        m_eff = n_eff // (SEL_LEN // CMP_STRIDE)
        cmp_end = lax.broadcasted_iota(jnp.int32, (n_eff, 1), 0) * CMP_STRIDE + (CMP_LEN - 1)
        visible = cmp_end <= t_lane
        sc = [_dot(kc_ref[0:n_eff, :], qz[g]) for g in groups]
        pc = []
        for g in groups:
            s = jnp.where(visible, sc[g], NEG)
            mxc = jnp.max(s, axis=0, keepdims=True)
            mxc = jnp.where(mxc < 0.5 * NEG, 0.0, mxc)
            ec = jnp.exp2(s - mxc)
            pc.append(ec * (1.0 / jnp.maximum(jnp.sum(ec, axis=0, keepdims=True), 1e-30)))
        for g in groups:
            oc_scr[g][...] = own_rows(_dot(vcT_ref[:, 0:n_eff], pc[g].astype(BF16)), g)
        ov = ovT_ref[0:m_eff, 0:n_eff]
        imp = []
        for g in groups:
            psum = pc[g][:, 0:Q]
            for r in range(1, R):
                psum = psum + pc[g][:, r * Q:(r + 1) * Q]
            imp.append(sum(_dot(ov, part) for part in _split3(psum)))
        m_idx = lax.broadcasted_iota(jnp.int32, (m_eff, Q), 0).astype(F32)
        allowed = m_idx <= bq
        forced = (m_idx == 0.0) | (m_idx == bq) | (m_idx == bq - 1.0)
        if first_class:
            score = [jnp.where(allowed, imp[g] + jnp.where(forced, SEL_BONUS, 0.0), NEG) for g in groups]
            n_pick = min(SEL_TOP, m_eff)
        else:
            free = allowed & jnp.logical_not(forced)
            score = [jnp.where(free, imp[g], NEG) for g in groups]
            n_pick = SEL_TOP - 3
        for _ in range(n_pick):
            for g in groups:
                best = jnp.max(score[g], axis=0, keepdims=True)
                first = jnp.min(jnp.where(score[g] == best, m_idx, float(m_eff)), axis=0, keepdims=True)
                score[g] = jnp.where(m_idx == first, LOWEST, score[g])
        for g in groups:
            picked = score[g] < 0.5 * LOWEST
            if not first_class:
                picked = picked | forced
            bias = jnp.where(allowed & picked, 0.0, NEG)
            bias_scr[g][0:m_eff, :] = jnp.concatenate([bias] * R, axis=1)
            if m_eff < nsel:
                bias_scr[g][m_eff:nsel, :] = jnp.full((nsel - m_eff, L), NEG, F32)

    sizes = sorted({max(Q, (nc * k // CMP_CLASSES) // Q * Q) for k in range(1, CMP_CLASSES + 1)})
    lo = 0
    for n_eff in sizes:
        hi = n_eff // (Q // CMP_STRIDE)
        pl.when((i >= lo) & (i < hi))(functools.partial(compress_and_select, n_eff, lo == 0))
        lo = hi

    for g in groups:
        m_scr[g][...] = jnp.full(m_scr[g].shape, NEG, F32)
        acc_scr[g][...] = jnp.zeros(acc_scr[g].shape, F32)
    rhs_pad = jnp.zeros((N_KVG - HEAD_DIM - BIAS_ROWS, L), BF16)
    bias_pad = jnp.zeros((BIAS_ROWS - SEL_BLOCKS_PER_TILE, L), F32)
    buf0, buf1 = (s0_scr, t0_scr), (s1_scr, t1_scr)

    def scores(kt, dst):
        for g in groups:
            brows = bias_scr[g][pl.ds(pl.multiple_of(kt * SEL_BLOCKS_PER_TILE, SEL_BLOCKS_PER_TILE),
                                      SEL_BLOCKS_PER_TILE), :]
            rhs = jnp.concatenate([qg[g], jnp.concatenate([brows, bias_pad], axis=0).astype(BF16), rhs_pad], axis=0)
            s = _dot(ksel_ref[g, kt], rhs)
            dst[0][g][...] = s
            dst[1][g][...] = jnp.max(s, axis=0, keepdims=True)

    def absorb(kt, src):
        for g in groups:
            m_old = m_scr[g][...]
            m_new = jnp.maximum(m_old, src[1][g][...])
            alpha = jnp.exp2(m_old - m_new)
            p = jnp.exp2(src[0][g][...] - m_new)
            acc_scr[g][...] = alpha * acc_scr[g][...] + _dot(vselT_ref[kt, g], p.astype(BF16))
            m_scr[g][...] = m_new

    def step(kt, src, dst):
        scores(kt + 1, dst)
        absorb(kt, src)

    def two_steps(kt):
        step(kt, buf0, buf1)
        step(kt + 1, buf1, buf0)

    k_row = lax.broadcasted_iota(jnp.int32, (Q, 1), 0)
    q_lane = lax.broadcasted_iota(jnp.int32, (1, L), 1) & (Q - 1)
    n_full = (i * Q) // SEL_TILE

    def last_tile(src):
        r0 = pl.multiple_of(i * Q - n_full * SEL_TILE, Q)
        for g in groups:
            src[0][g][pl.ds(r0, Q), :] = jnp.where(k_row <= q_lane, src[0][g][pl.ds(r0, Q), :], NEG)
            src[1][g][...] = jnp.max(src[0][g][...], axis=0, keepdims=True)
        absorb(n_full, src)

    odd = n_full & 1

    @pl.when(odd == 0)
    def _():
        scores(0, buf0)

    @pl.when(odd == 1)
    def _():
        scores(0, buf1)
        step(0, buf1, buf0)

    n_pairs = n_full >> 1

    def four_steps(j, carry):
        two_steps(odd + 4 * j)
        two_steps(odd + 4 * j + 2)
        return carry

    lax.fori_loop(0, n_pairs >> 1, four_steps, 0)

    @pl.when((n_pairs & 1) == 1)
    def _():
        two_steps(odd + 4 * (n_pairs >> 1))

    last_tile(buf0)

    sw = [[] for _ in groups]
    for w in range(WIN_TILES):
        j = i - (WIN_TILES - 1) + w
        for g in groups:
            s = _dot(kwin_refs[w][0], qz[g])
            if w == 0:
                s = jnp.where(k_row > q_lane, s, NEG)
            if w == WIN_TILES - 1:
                s = jnp.where(k_row <= q_lane, s, NEG)
            else:
                s = jnp.where(j >= 0, s, NEG)
            sw[g].append(s)
    ew, denw = [], []
    for g in groups:
        mxw = functools.reduce(jnp.maximum, [jnp.max(s, axis=0, keepdims=True) for s in sw[g]])
        ew.append([jnp.exp2(s - mxw) for s in sw[g]])
        denw.append(jnp.maximum(sum(jnp.sum(e, axis=0, keepdims=True) for e in ew[g]), 1e-30))
    owT = []
    for g in groups:
        prod = sum(_dot(vwin_refs[w][0], ew[g][w].astype(BF16)) for w in range(WIN_TILES))
        owT.append(own_rows(prod, g) * (1.0 / denw[g]))

    def gate_row(g, j):
        return jnp.concatenate([gT_ref[(g * R + r) * 3 + j:(g * R + r) * 3 + j + 1, :] for r in range(R)], axis=1)

    o_rows = []
    for g in groups:
        osT = acc_scr[g][0:HEAD_DIM, :] * (1.0 / acc_scr[g][HEAD_DIM:HEAD_DIM + 1, :])
        oT = gate_row(g, 0) * oc_scr[g][...] + gate_row(g, 1) * osT + gate_row(g, 2) * owT[g]
        o_rows += [oT[:, r * Q:(r + 1) * Q] for r in range(R)]
    o_ref[...] = jnp.concatenate(o_rows, axis=0).T


def _nsa(qT, kc, vcT, ksel, vselT, kwin, vwinT, gT):
    T = qT.shape[1]
    nb = T // Q_BLOCK
    nc = kc.shape[0]
    nsel = T // SEL_LEN
    ntile = T // SEL_TILE
    n = jnp.arange(nc)[None, :] * CMP_STRIDE
    m = jnp.arange(nsel)[:, None] * SEL_LEN
    ovT = ((n < m + SEL_LEN) & (n + CMP_LEN > m)).astype(BF16)
    ksel4 = ksel.reshape(NSA_KV_HEADS, ntile, SEL_TILE, N_KVG)
    kwin3 = kwin.reshape(nb, Q_BLOCK, N_KVG)
    L = NSA_GROUP * Q_BLOCK
    const = lambda shape: pl.BlockSpec(shape, lambda i: (0,) * len(shape))
    win_tile = lambda i, w: (jnp.maximum(i - (WIN_TILES - 1) + w, 0), 0, 0)
    kwin_specs = [pl.BlockSpec((1, Q_BLOCK, N_KVG), functools.partial(win_tile, w=w)) for w in range(WIN_TILES)]
    vwin_specs = [pl.BlockSpec((1, N_KVG, Q_BLOCK), functools.partial(win_tile, w=w)) for w in range(WIN_TILES)]
    group_scratch = [pltpu.VMEM((nsel, L), F32), pltpu.VMEM((SEL_TILE, L), F32), pltpu.VMEM((SEL_TILE, L), F32),
                     pltpu.VMEM((1, L), F32), pltpu.VMEM((1, L), F32),
                     pltpu.VMEM((1, L), F32), pltpu.VMEM((SEL_V_ROWS, L), F32), pltpu.VMEM((HEAD_DIM, L), F32)]
    return pl.pallas_call(
        _nsa_kernel,
        grid=(nb,),
        in_specs=[pl.BlockSpec((N_Q, Q_BLOCK), lambda i: (0, i)),
                  const((nc, N_KVG)), const((N_KVG, nc)), const((nsel, nc)),
                  const((NSA_KV_HEADS, ntile, SEL_TILE, N_KVG)), const((ntile, NSA_KV_HEADS, SEL_V_ROWS, SEL_TILE))]
                 + kwin_specs + vwin_specs
                 + [pl.BlockSpec((N_GATE_PAD, Q_BLOCK), lambda i: (0, i))],
        out_specs=pl.BlockSpec((Q_BLOCK, N_Q), lambda i: (i, 0)),
        out_shape=jax.ShapeDtypeStruct((T, N_Q), F32),
        scratch_shapes=group_scratch * NSA_KV_HEADS,
        compiler_params=pltpu.CompilerParams(dimension_semantics=("arbitrary",),
                                             vmem_limit_bytes=VMEM_LIMIT),
    )(qT, kc, vcT, ovT, ksel4, vselT, *([kwin3] * WIN_TILES), *([vwinT] * WIN_TILES), gT)


def _sb_kernel(qT_ref, k_ref, vT_ref, o_ref, *scr):
    i = pl.program_id(0)
    Q, W = Q_BLOCK, 2 * HEAD_DIM
    pairs = SB_HEADS // 2
    blk_row = lax.broadcasted_iota(jnp.int32, (W, 2 * Q), 0) < HEAD_DIM
    blk_lane = lax.broadcasted_iota(jnp.int32, (W, 2 * Q), 1) < Q
    zero = jnp.zeros((), BF16)
    q_pairs = []
    for pr in range(pairs):
        x = qT_ref[pr * W:(pr + 1) * W, :]
        q_pairs.append(jnp.where(blk_row == blk_lane, jnp.concatenate([x, x], axis=1), zero))
    k_row = lax.broadcasted_iota(jnp.int32, (SB_TILE, 1), 0)
    q_lane = lax.broadcasted_iota(jnp.int32, (1, SB_HEADS * Q), 1) & (Q - 1)
    rr = lax.broadcasted_iota(jnp.int32, (SB_TILE, SB_TILE), 0)
    cc = lax.broadcasted_iota(jnp.int32, (SB_TILE, SB_TILE), 1)
    from_here = (cc >= rr).astype(BF16)
    acc_scrs, c_scrs = scr[0:pairs], scr[pairs:pairs + 2]
    for buf in scr:
        buf[...] = jnp.zeros(buf.shape, F32)

    halves = ((0, 1), (2, 3))
    causal = k_row < q_lane[:, 0:4 * Q]

    def walk(tiles):
        rows = [pl.ds(pl.multiple_of(kt * SB_TILE, SB_TILE), SB_TILE) for kt, _, _ in tiles]
        chains = [(t, hf) for t in range(len(tiles)) for hf in range(2)]
        z, d, suffix = {}, {}, {}
        for t, hf in chains:
            z[t, hf] = jnp.concatenate(
                [_dot(k_ref[rows[t], pr * W:(pr + 1) * W], q_pairs[pr]) for pr in halves[hf]], axis=1)
        for t, hf in chains:
            x = jnp.maximum(z[t, hf], 0.0) + jnp.log(1.0 + jnp.exp(-jnp.abs(z[t, hf])))
            if tiles[t][1]:
                x = jnp.where(causal, x, 0.0)
            if tiles[t][2] is not None:
                x = jnp.where(tiles[t][2], x, 0.0)
            d[t, hf] = x
            hi = x.astype(BF16)
            lo = (x - hi.astype(F32)).astype(BF16)
            suffix[t, hf] = _dot(from_here, hi) + _dot(from_here, lo)
        least = None
        for hf in range(2):
            c = c_scrs[hf][...]
            for t, (kt, own, exists) in enumerate(tiles):
                a = jnp.exp(z[t, hf] - suffix[t, hf] - c)
                if own:
                    a = jnp.where(causal, a, 0.0)
                if exists is not None:
                    a = jnp.where(exists, a, 0.0)
                a = a.astype(BF16)
                for n, pr in enumerate(halves[hf]):
                    acc_scrs[pr][...] = acc_scrs[pr][...] + _dot(vT_ref[kt, pr * W:(pr + 1) * W, :],
                                                                 a[:, n * 2 * Q:(n + 1) * 2 * Q])
                c = c + jnp.sum(d[t, hf], axis=0, keepdims=True)
            c_scrs[hf][...] = c
            least = c if least is None else jnp.minimum(least, c)
        return -jnp.min(least)

    first = [(i, True, None)] + [(jnp.maximum(i - n, 0), False, i - n >= 0) for n in range(1, SB_FIRST_TILES)]
    worst0 = walk(first)

    def cond(carry):
        kt, worst = carry
        return (kt >= 0) & (worst >= SB_SKIP_LOG)

    def body(carry):
        kt, _ = carry
        return kt - 1, walk([(kt, False, None)])

    lax.while_loop(cond, body, (i - SB_FIRST_TILES, worst0))
    for pr in range(pairs):
        acc = acc_scrs[pr][...]
        o_ref[:, pr * W:(pr + 1) * W] = jnp.where(blk_row[:, 0:Q], acc[:, 0:Q], acc[:, Q:2 * Q]).T


def _stick_breaking(sbqT, sbk, sbvT):
    T = sbk.shape[0]
    nb = T // Q_BLOCK
    W = 2 * HEAD_DIM
    return pl.pallas_call(
        _sb_kernel,
        grid=(nb,),
        in_specs=[pl.BlockSpec((N_SBH, Q_BLOCK), lambda i: (0, i)),
                  pl.BlockSpec((T, N_SBH), lambda i: (0, 0)),
                  pl.BlockSpec((nb, N_SBH, Q_BLOCK), lambda i: (0, 0, 0))],
        out_specs=pl.BlockSpec((Q_BLOCK, N_SBH), lambda i: (i, 0)),
        out_shape=jax.ShapeDtypeStruct((T, N_SBH), F32),
        scratch_shapes=[pltpu.VMEM((W, 2 * Q_BLOCK), F32)] * (SB_HEADS // 2) + [pltpu.VMEM((1, 4 * Q_BLOCK), F32)] * 2,
        compiler_params=pltpu.CompilerParams(dimension_semantics=("arbitrary",),
                                             vmem_limit_bytes=VMEM_LIMIT),
    )(sbqT, sbk, sbvT)


def _mix_kernel(x_ref, on_ref, os_ref, nwn_ref, nws_ref, wo_ref, fw_ref, rhi_ref, rlo_ref, rb_ref,
                x1_ref, h2_ref, lgT_ref):
    n1 = _rms(on_ref[...], nwn_ref[...]).astype(BF16)
    n2 = _rms(os_ref[...], nws_ref[...]).astype(BF16)
    x1 = x_ref[...] + _dot(n1, wo_ref[0:N_Q]) + _dot(n2, wo_ref[N_Q:N_Q + N_SBH])
    x1_ref[...] = x1
    h2 = _rms(x1, fw_ref[...])
    hi = h2.astype(BF16)
    lo = (h2 - hi.astype(F32)).astype(BF16)
    h2_ref[...] = hi
    lg = _dot(hi, rhi_ref[...]) + _dot(hi, rlo_ref[...]) + _dot(lo, rhi_ref[...]) + rb_ref[...]
    lgT_ref[...] = lg.T[0:ROUTER_ROWS]


def _mix(x, o_nsa, o_sb, nsa_norm_w, sb_norm_w, w_out, ffn_norm_w, rg_w, rg_b, re_w, re_b):
    T, D = x.shape
    R = PROJ_ROWS
    pad = ROUTER_LANES - N_GROUPS - N_EXPERTS
    wr = jnp.concatenate([rg_w, re_w, jnp.zeros((D, pad), F32)], axis=1)
    wr_hi = wr.astype(BF16)
    wr_lo = (wr - wr_hi.astype(F32)).astype(BF16)
    rb = jnp.concatenate([rg_b, re_b, jnp.zeros((pad,), F32)]).reshape(1, ROUTER_LANES)
    full = lambda shape: pl.BlockSpec(shape, lambda i: (0,) * len(shape))
    rows = lambda n: pl.BlockSpec((R, n), lambda i: (i, 0))
    return pl.pallas_call(
        _mix_kernel,
        grid=(T // R,),
        in_specs=[rows(D), rows(N_Q), rows(N_SBH), full((1, N_Q)), full((1, N_SBH)), full((N_Q + N_SBH, D)),
                  full((1, D)), full((D, ROUTER_LANES)), full((D, ROUTER_LANES)), full((1, ROUTER_LANES))],
        out_specs=[rows(D), rows(D), pl.BlockSpec((ROUTER_ROWS, R), lambda i: (0, i))],
        out_shape=[jax.ShapeDtypeStruct((T, D), F32), jax.ShapeDtypeStruct((T, D), BF16),
                   jax.ShapeDtypeStruct((ROUTER_ROWS, T), F32)],
        compiler_params=pltpu.CompilerParams(dimension_semantics=("arbitrary",),
                                             vmem_limit_bytes=VMEM_LIMIT),
    )(x, o_nsa, o_sb, nsa_norm_w.reshape(1, N_Q), sb_norm_w.reshape(1, N_SBH), w_out.astype(BF16),
      ffn_norm_w.reshape(1, D), wr_hi, wr_lo, rb)


def _routing(lg, axis):
    pos_i = lax.broadcasted_iota(jnp.int32, lg.shape, axis)
    pos = pos_i.astype(F32)
    first_max = lambda v, mx: jnp.min(jnp.where(v == mx, pos, float(ROUTER_LANES)), axis=axis, keepdims=True)
    gl = jnp.where(pos_i < N_GROUPS, lg, -jnp.inf)
    gmax = jnp.max(gl, axis=axis, keepdims=True)
    grp = first_max(gl, gmax)
    g_gate = 1.0 / jnp.sum(jnp.exp(gl - gmax), axis=axis, keepdims=True)
    e_idx = pos_i - N_GROUPS
    e_grp = (e_idx >> 2).astype(F32)
    in_grp = (e_idx >= 0) & (e_idx < N_EXPERTS) & (e_grp == grp)
    el = jnp.where(in_grp, lg, -jnp.inf)
    top1 = jnp.max(el, axis=axis, keepdims=True)
    i1 = first_max(el, top1)
    el2 = jnp.where(pos == i1, -jnp.inf, el)
    top2 = jnp.max(el2, axis=axis, keepdims=True)
    i2 = first_max(el2, top2)
    e2 = jnp.exp(top2 - top1)
    w1 = 1.0 / (1.0 + e2)
    w2 = e2 / (1.0 + e2)
    weight = g_gate * (jnp.where(pos == i1, w1, 0.0) + jnp.where(pos == i2, w2, 0.0))
    routed = jnp.where(pos == i1, 1.0, 0.0) + jnp.where(pos == i2, 1.0, 0.0)
    return weight, routed


def _moe_kernel(h_ref, lgT_ref, before_ref, wg_ref, wu_ref, wd_ref, o_ref, rankT_scr, wparts_scr):
    e = pl.program_id(1)
    rows, width = h_ref.shape
    n_sub = rows // MOE_SUB

    @pl.when(e == 0)
    def _():
        o_ref[...] = jnp.zeros(o_ref.shape, F32)
        weight, routed = _routing(lgT_ref[...], 0)
        flags = routed.astype(BF16)
        for s in range(n_sub):
            sub = slice(s * MOE_SUB, (s + 1) * MOE_SUB)
            rank = _dot(flags[:, sub], before_ref[...])
            rankT_scr[:, sub] = jnp.where(routed[:, sub] > 0.0, rank, -1.0)
        parts = [p.astype(F32) for p in _split3(weight)]
        pad = jnp.zeros((MOE_SLOT_ROWS - 3 * ROUTER_ROWS, rows), F32)
        wparts_scr[...] = jnp.concatenate(parts + [pad], axis=0).T.astype(BF16)

    experts = range(MOE_EXPERTS_PER_STEP)
    pos = [e * MOE_EXPERTS_PER_STEP + x + N_GROUPS for x in experts]
    rank_row = [rankT_scr[pl.ds(pos[x], 1), :] for x in experts]
    n_routed = jnp.max(functools.reduce(jnp.maximum, rank_row)).astype(jnp.int32) + 1
    slot_col = lax.broadcasted_iota(jnp.int32, (MOE_SLOT_ROWS, 1), 0)
    lane = lax.broadcasted_iota(jnp.int32, (n_sub * MOE_SUB_CAP, ROUTER_LANES), 1)
    y_pad = jnp.zeros((MOE_SLOT_ROWS - MOE_SUB_CAP, width), BF16)

    def chunk(ch, carry):
        want = jnp.where(slot_col < MOE_SUB_CAP, slot_col + ch * MOE_SUB_CAP, -2).astype(F32)
        xg, w_slot, spread = [], [], []
        for x in experts:
            got, got_w, back = [], [], []
            for s in range(n_sub):
                sub = slice(s * MOE_SUB, (s + 1) * MOE_SUB)
                onehot = jnp.where(rank_row[x][:, sub] == want, 1.0, 0.0)
                pick = onehot[0:MOE_SUB_CAP].astype(BF16)
                got.append(_dot(pick, h_ref[sub, :]))
                got_w.append(_dot(pick, wparts_scr[sub, :]))
                back.append(onehot.T.astype(BF16))
            xg.append(jnp.concatenate(got, axis=0).astype(BF16))
            mine = (lane == pos[x]) | (lane == pos[x] + ROUTER_ROWS) | (lane == pos[x] + 2 * ROUTER_ROWS)
            w_slot.append(jnp.sum(jnp.where(mine, jnp.concatenate(got_w, axis=0), 0.0), axis=1, keepdims=True))
            spread.append(back)
        a = [_dot(xg[x], wg_ref[x]) for x in experts]
        b = [_dot(xg[x], wu_ref[x]) for x in experts]
        act = [(a[x] * jax.nn.sigmoid(a[x]) * b[x]).astype(BF16) for x in experts]
        y = [(_dot(act[x], wd_ref[x]) * w_slot[x]).astype(BF16) for x in experts]
        for s in range(n_sub):
            out = [_dot(spread[x][s],
                        jnp.concatenate([y[x][s * MOE_SUB_CAP:(s + 1) * MOE_SUB_CAP], y_pad], axis=0))
                   for x in experts]
            o_ref[s * MOE_SUB:(s + 1) * MOE_SUB, :] += functools.reduce(lambda p, q: p + q, out)
        return carry

    lax.fori_loop(0, (n_routed + (MOE_SUB_CAP - 1)) // MOE_SUB_CAP, chunk, 0)


def _moe(h2, logitsT, w_gate, w_up, w_down):
    T, D = h2.shape
    R = min(MOE_ROWS, T)
    before = jnp.triu(jnp.ones((MOE_SUB, MOE_SUB), BF16), k=1)
    rows = lambda n: pl.BlockSpec((R, n), lambda i, e: (i, 0))
    const = lambda a, b: pl.BlockSpec((a, b), lambda i, e: (0, 0))
    per_expert = lambda a, b: pl.BlockSpec((MOE_EXPERTS_PER_STEP, a, b), lambda i, e: (e, 0, 0))
    return pl.pallas_call(
        _moe_kernel,
        grid=(T // R, N_EXPERTS // MOE_EXPERTS_PER_STEP),
        in_specs=[rows(D), pl.BlockSpec((ROUTER_ROWS, R), lambda i, e: (0, i)), const(MOE_SUB, MOE_SUB),
                  per_expert(D, EXPERT_FF), per_expert(D, EXPERT_FF), per_expert(EXPERT_FF, D)],
        out_specs=rows(D),
        out_shape=jax.ShapeDtypeStruct((T, D), F32),
        scratch_shapes=[pltpu.VMEM((ROUTER_ROWS, R), F32), pltpu.VMEM((R, ROUTER_LANES), BF16)],
        compiler_params=pltpu.CompilerParams(dimension_semantics=("arbitrary", "arbitrary"),
                                             vmem_limit_bytes=VMEM_LIMIT),
    )(h2, logitsT, before, w_gate.astype(BF16), w_up.astype(BF16), w_down.astype(BF16))


def _final_kernel(x1_ref, moe_ref, fw_ref, o_ref):
    o_ref[...] = _rms(x1_ref[...] + moe_ref[...], fw_ref[...])


def _final(x1, moe_out, final_norm_w):
    T, D = x1.shape
    R = PROJ_ROWS
    rows = pl.BlockSpec((R, D), lambda i: (i, 0))
    return pl.pallas_call(
        _final_kernel,
        grid=(T // R,),
        in_specs=[rows, rows, pl.BlockSpec((1, D), lambda i: (0, 0))],
        out_specs=rows,
        out_shape=jax.ShapeDtypeStruct((T, D), F32),
        compiler_params=pltpu.CompilerParams(dimension_semantics=("arbitrary",), vmem_limit_bytes=VMEM_LIMIT),
    )(x1, moe_out, final_norm_w.reshape(1, D))


def kernel(x, positions, attn_norm_w, w_in, cmp_pe_k, cmp_pe_v, cmp_k_w1, cmp_k_w2, cmp_v_w1, cmp_v_w2,
           nsa_out_norm_w, sb_out_norm_w, w_out, ffn_norm_w, router_group_w, router_group_b,
           router_expert_w, router_expert_b, w_gate, w_up, w_down, final_norm_w):
    B, T, D = x.shape
    assert B == 1 and T % SEL_TILE == 0 and T % PROJ_ROWS == 0 and T // SEL_LEN >= SEL_TOP
    assert attn_norm_w.shape[0] == 1, "one layer: the last kernel applies the model's final norm"
    xs = x.reshape(T, D)
    pos = positions.reshape(T)
    (cmpk, cmpv, sbq, sbk, sbv, qT, ksel, kwin, vselT, vwinT, gT) = _project(xs, pos, attn_norm_w[0], w_in[0])
    kc, vcT = _compress(cmpk, cmpv, cmp_pe_k[0], cmp_pe_v[0], cmp_k_w1[0], cmp_k_w2[0], cmp_v_w1[0], cmp_v_w2[0])
    o_nsa = _nsa(qT, kc, vcT, ksel, vselT, kwin, vwinT, gT)
    o_sb = _stick_breaking(sbq, sbk, sbv)
    x1, h2, logitsT = _mix(xs, o_nsa, o_sb, nsa_out_norm_w[0], sb_out_norm_w[0], w_out[0], ffn_norm_w[0],
                           router_group_w[0], router_group_b[0], router_expert_w[0], router_expert_b[0])
    moe_out = _moe(h2, logitsT, w_gate[0], w_up[0], w_down[0])
    return _final(x1, moe_out, final_norm_w).reshape(B, T, D)
```

```python
import functools

import jax
import jax.numpy as jnp
from jax import lax
from jax.experimental import pallas as pl
from jax.experimental.pallas import tpu as pltpu

HEAD_DIM = 64
NSA_HEADS = 8
NSA_KV_HEADS = 2
NSA_GROUP = NSA_HEADS // NSA_KV_HEADS
SB_HEADS = 8
ROPE_THETA = 500000.0
ROPE_DIM = HEAD_DIM // 4
ROPE_HALF = ROPE_DIM // 2
CMP_LEN = 32
CMP_STRIDE = 16
CMP_HIDDEN = 256
SEL_LEN = 64
SEL_TOP = 16
SEL_BONUS = 1.0e4
WINDOW = 512
Q_BLOCK = 128
N_GROUPS = 4
EXPERTS_PER_GROUP = 4
N_EXPERTS = N_GROUPS * EXPERTS_PER_GROUP
EXPERT_FF = 512
EPS = 1e-6
NEG = -1e30
LOWEST = -3.0e38

N_Q = NSA_HEADS * HEAD_DIM
N_KVG = NSA_KV_HEADS * HEAD_DIM
N_GATE = NSA_HEADS * 3
N_GATE_PAD = 32
N_SBH = SB_HEADS * HEAD_DIM
SCALE = HEAD_DIM ** -0.5
LOG2E = 1.4426950408889634
SEL_BLOCKS_PER_TILE = 8
BIAS_ROWS = 16
CMP_CLASSES = 8
SEL_V_ROWS = HEAD_DIM + 16

PROJ_ROWS = 512
SEL_TILE = 512
WIN_TILES = WINDOW // Q_BLOCK + 1
SB_TILE = 128
MOE_ROWS = 1024
MOE_EXPERTS_PER_STEP = 2
MOE_SLOT_ROWS = 128
MOE_SUB_SHIFT = 8
MOE_SUB = 1 << MOE_SUB_SHIFT
MOE_SUB_CAP = 48
ROUTER_LANES = 128
ROUTER_ROWS = 32
SB_FIRST_TILES = 3
SB_SKIP_LOG = -104.0

VMEM_LIMIT = 56 * 1024 * 1024

BF16 = jnp.bfloat16
F32 = jnp.float32


def _rms(x, w):
    return x * lax.rsqrt(jnp.mean(x * x, axis=-1, keepdims=True) + EPS) * w


def _dot(a, b):
    return jnp.dot(a, b, preferred_element_type=F32)


def _dot_nt(a, b):
    return lax.dot_general(a, b, (((1,), (1,)), ((), ())), preferred_element_type=F32)


def _split3(x):
    hi = x.astype(BF16)
    r1 = x - hi.astype(F32)
    mid = r1.astype(BF16)
    lo = (r1 - mid.astype(F32)).astype(BF16)
    return hi, mid, lo


def _proj_kernel(x_ref, nw_ref, wn_ref, wt_ref, pos_ref, invf_ref,
                 cmpk_ref, cmpv_ref, sbqT_ref, sbk_ref, sbvT_ref,
                 qT_ref, ksel_ref, kwin_ref, vselT_ref, vwinT_ref, gT_ref):
    h = _rms(x_ref[...], nw_ref[...]).astype(BF16)
    p1 = _dot(h, wn_ref[...])
    cmpk_ref[...] = p1[:, 0:N_KVG]
    cmpv_ref[...] = p1[:, N_KVG:2 * N_KVG]
    sbk_ref[...] = p1[:, 2 * N_KVG:2 * N_KVG + N_SBH].astype(BF16)

    p2 = _dot_nt(wt_ref[...], h)
    o = N_Q + 4 * N_KVG + N_GATE_PAD
    sbqT_ref[...] = (p2[o:o + N_SBH] * SCALE).astype(BF16)
    sbv = p2[o + N_SBH:o + 2 * N_SBH].astype(BF16)
    for j in range(PROJ_ROWS // Q_BLOCK):
        sbvT_ref[j] = sbv[:, j * Q_BLOCK:(j + 1) * Q_BLOCK]
    ang = invf_ref[...] * pos_ref[...].astype(F32)
    cos, sin = jnp.cos(ang), jnp.sin(ang)
    n_rope_heads = NSA_HEADS + 2 * NSA_KV_HEADS
    roped = []
    for hd in range(n_rope_heads):
        blk = p2[hd * HEAD_DIM:(hd + 1) * HEAD_DIM]
        x1, x2 = blk[0:ROPE_HALF], blk[ROPE_HALF:ROPE_DIM]
        roped.append(jnp.concatenate(
            [x1 * cos - x2 * sin, x2 * cos + x1 * sin, blk[ROPE_DIM:]], axis=0))
    qT_ref[...] = (jnp.concatenate(roped[:NSA_HEADS], axis=0) * (SCALE * LOG2E)).astype(BF16)
    kT = jnp.concatenate(roped[NSA_HEADS:], axis=0)
    kn = kT.T.astype(BF16)
    r_blk = lax.broadcasted_iota(jnp.int32, (PROJ_ROWS, HEAD_DIM), 0) >> 6
    c_idx = lax.broadcasted_iota(jnp.int32, (PROJ_ROWS, HEAD_DIM), 1)
    onehot = jnp.where(r_blk == c_idx, 1.0, 0.0).astype(BF16)
    for gk in range(NSA_KV_HEADS):
        ksel_ref[gk] = jnp.concatenate([kn[:, gk * HEAD_DIM:(gk + 1) * HEAD_DIM], onehot], axis=1)
    kwin_ref[...] = kn[:, N_KVG:2 * N_KVG]
    o = N_Q + 2 * N_KVG
    for gk in range(NSA_KV_HEADS):
        vselT_ref[0, gk] = jnp.concatenate(
            [p2[o + gk * HEAD_DIM:o + (gk + 1) * HEAD_DIM],
             jnp.where(lax.broadcasted_iota(jnp.int32, (SEL_V_ROWS - HEAD_DIM, PROJ_ROWS), 0) == 0, 1.0, 0.0)],
            axis=0).astype(BF16)
    vw = p2[o + N_KVG:o + 2 * N_KVG].astype(BF16)
    for j in range(PROJ_ROWS // Q_BLOCK):
        vwinT_ref[j] = vw[:, j * Q_BLOCK:(j + 1) * Q_BLOCK]
    o = o + 2 * N_KVG
    gT_ref[...] = jax.nn.sigmoid(p2[o:o + N_GATE_PAD])


def _project(x, positions, attn_norm_w, w_in):
    T, D = x.shape
    R = PROJ_ROWS
    o1, o2, o3 = N_Q, N_Q + 6 * N_KVG, N_Q + 6 * N_KVG + N_GATE
    kv = lambda i: w_in[:, o1 + i * N_KVG:o1 + (i + 1) * N_KVG]
    sb = lambda i: w_in[:, o3 + i * N_SBH:o3 + (i + 1) * N_SBH]
    w_nat = jnp.concatenate([kv(0), kv(1), sb(1)], axis=1).astype(BF16)
    w_t = jnp.concatenate(
        [w_in[:, :o1], kv(2), kv(4), kv(3), kv(5), w_in[:, o2:o3],
         jnp.zeros((D, N_GATE_PAD - N_GATE), w_in.dtype), sb(0), sb(2)], axis=1).T.astype(BF16)
    inv_freq = ROPE_THETA ** (-jnp.arange(0, ROPE_DIM, 2, dtype=F32) / ROPE_DIM)
    n_nat, n_t = w_nat.shape[1], w_t.shape[0]
    full = lambda shape: pl.BlockSpec(shape, lambda i: (0,) * len(shape))
    rows = lambda n: pl.BlockSpec((R, n), lambda i: (i, 0))
    cols = lambda n: pl.BlockSpec((n, R), lambda i: (0, i))
    return pl.pallas_call(
        _proj_kernel,
        grid=(T // R,),
        in_specs=[rows(D), full((1, D)), full((D, n_nat)), full((n_t, D)), cols(1), full((ROPE_HALF, 1))],
        out_specs=[rows(N_KVG), rows(N_KVG), cols(N_SBH), rows(N_SBH),
                   pl.BlockSpec((R // Q_BLOCK, N_SBH, Q_BLOCK), lambda i: (i, 0, 0)),
                   cols(N_Q), pl.BlockSpec((NSA_KV_HEADS, R, N_KVG), lambda i: (0, i, 0)), rows(N_KVG),
                   pl.BlockSpec((1, NSA_KV_HEADS, SEL_V_ROWS, R), lambda i: (i, 0, 0, 0)),
                   pl.BlockSpec((R // Q_BLOCK, N_KVG, Q_BLOCK), lambda i: (i, 0, 0)),
                   cols(N_GATE_PAD)],
        out_shape=[jax.ShapeDtypeStruct((T, N_KVG), F32), jax.ShapeDtypeStruct((T, N_KVG), F32),
                   jax.ShapeDtypeStruct((N_SBH, T), BF16), jax.ShapeDtypeStruct((T, N_SBH), BF16),
                   jax.ShapeDtypeStruct((T // Q_BLOCK, N_SBH, Q_BLOCK), BF16),
                   jax.ShapeDtypeStruct((N_Q, T), BF16),
                   jax.ShapeDtypeStruct((NSA_KV_HEADS, T, N_KVG), BF16), jax.ShapeDtypeStruct((T, N_KVG), BF16),
                   jax.ShapeDtypeStruct((T // R, NSA_KV_HEADS, SEL_V_ROWS, R), BF16),
                   jax.ShapeDtypeStruct((T // Q_BLOCK, N_KVG, Q_BLOCK), BF16),
                   jax.ShapeDtypeStruct((N_GATE_PAD, T), F32)],
        compiler_params=pltpu.CompilerParams(dimension_semantics=("arbitrary",),
                                             vmem_limit_bytes=VMEM_LIMIT),
    )(x, attn_norm_w.reshape(1, D), w_nat, w_t, positions.reshape(1, T), inv_freq.reshape(ROPE_HALF, 1))


def _compress_kernel(x_ref, pea_ref, peb_ref, wa_ref, wb_ref, w2_ref, nat_ref, tr_ref):
    x = x_ref[0]
    nc = x.shape[0]
    ha = _dot((x + pea_ref[0]).astype(BF16), wa_ref[0])
    hb = _dot((x + peb_ref[0]).astype(BF16), wb_ref[0])
    hid = ha + pltpu.roll(hb, nc - 1, 0)
    act = (hid * jax.nn.sigmoid(hid)).astype(BF16)
    out = _dot(act, w2_ref[0])
    nat_ref[0] = out.astype(BF16)
    tr_ref[0] = out.T.astype(BF16)


def _compress(cmpk, cmpv, pe_k, pe_v, k_w1, k_w2, v_w1, v_w2):
    T = cmpk.shape[0]
    nc = T // CMP_STRIDE
    half = CMP_LEN // 2
    G = NSA_KV_HEADS
    width = half * N_KVG
    x = jnp.stack([cmpk.reshape(nc, width), cmpv.reshape(nc, width)])
    eye = jnp.eye(G, dtype=F32)

    def pe_rows(pe):
        return jnp.broadcast_to(pe[:, None, :], (half, G, HEAD_DIM)).reshape(1, width)

    def w1_block(w1):
        w = w1.reshape(half, HEAD_DIM, CMP_HIDDEN)
        return jnp.einsum('ldj,gh->lgdhj', w, eye).reshape(width, G * CMP_HIDDEN).astype(BF16)

    def w2_block(w2):
        return jnp.einsum('jd,gh->gjhd', w2, eye).reshape(G * CMP_HIDDEN, N_KVG).astype(BF16)

    hw = half * HEAD_DIM
    pea = jnp.stack([pe_rows(pe_k[:half]), pe_rows(pe_v[:half])])
    peb = jnp.stack([pe_rows(pe_k[half:]), pe_rows(pe_v[half:])])
    wa = jnp.stack([w1_block(k_w1[:hw]), w1_block(v_w1[:hw])])
    wb = jnp.stack([w1_block(k_w1[hw:]), w1_block(v_w1[hw:])])
    w2 = jnp.stack([w2_block(k_w2), w2_block(v_w2)])
    blk = lambda a, b: pl.BlockSpec((1, a, b), lambda i: (i, 0, 0))
    nat, tr = pl.pallas_call(
        _compress_kernel,
        grid=(2,),
        in_specs=[blk(nc, width), blk(1, width), blk(1, width), blk(width, G * CMP_HIDDEN),
                  blk(width, G * CMP_HIDDEN), blk(G * CMP_HIDDEN, N_KVG)],
        out_specs=[blk(nc, N_KVG), blk(N_KVG, nc)],
        out_shape=[jax.ShapeDtypeStruct((2, nc, N_KVG), BF16), jax.ShapeDtypeStruct((2, N_KVG, nc), BF16)],
        compiler_params=pltpu.CompilerParams(dimension_semantics=("arbitrary",),
                                             vmem_limit_bytes=VMEM_LIMIT),
    )(x, pea, peb, wa, wb, w2)
    return nat[0], tr[1]


def _nsa_kernel(q_ref, kc_ref, vcT_ref, ovT_ref, ksel_ref, vselT_ref, *rest):
    kwin_refs = rest[0:WIN_TILES]
    vwin_refs = rest[WIN_TILES:2 * WIN_TILES]
    gT_ref, o_ref = rest[2 * WIN_TILES:2 * WIN_TILES + 2]
    scratch = rest[2 * WIN_TILES + 2:]
    G = NSA_KV_HEADS
    per = len(scratch) // G
    bias_scr, s0_scr, s1_scr, t0_scr, t1_scr, m_scr, acc_scr, oc_scr = (
        [scratch[g * per + n] for g in range(G)] for n in range(per))
    groups = range(G)
    i = pl.program_id(0)
    R, Q = NSA_GROUP, Q_BLOCK
    L = R * Q
    nsel = bias_scr[0].shape[0]

    zero = jnp.zeros((HEAD_DIM, L), BF16)
    qg, qz = [], []
    for g in groups:
        q = jnp.concatenate([q_ref[(g * R + r) * HEAD_DIM:(g * R + r + 1) * HEAD_DIM, :] for r in range(R)], axis=1)
        qg.append(q)
        qz.append(jnp.concatenate([q if gg == g else zero for gg in groups], axis=0))

    def own_rows(x, g):
        return x[g * HEAD_DIM:(g + 1) * HEAD_DIM]

    t_lane = i * Q + (lax.broadcasted_iota(jnp.int32, (1, L), 1) & (Q - 1))

    nc = kc_ref.shape[0]
    q_pos = i * Q + lax.broadcasted_iota(jnp.int32, (1, Q), 1)
    bq = (q_pos >> 6).astype(F32)

    def compress_and_select(n_eff, first_class):
        m_eff = n_eff // (SEL_LEN // CMP_STRIDE)
        cmp_end = lax.broadcasted_iota(jnp.int32, (n_eff, 1), 0) * CMP_STRIDE + (CMP_LEN - 1)
        visible = cmp_end <= t_lane
        sc = [_dot(kc_ref[0:n_eff, :], qz[g]) for g in groups]
        pc = []
        for g in groups:
            s = jnp.where(visible, sc[g], NEG)
            mxc = jnp.max(s, axis=0, keepdims=True)
            mxc = jnp.where(mxc < 0.5 * NEG, 0.0, mxc)
            ec = jnp.exp2(s - mxc)
            pc.append(ec * (1.0 / jnp.maximum(jnp.sum(ec, axis=0, keepdims=True), 1e-30)))
        for g in groups:
            oc_scr[g][...] = own_rows(_dot(vcT_ref[:, 0:n_eff], pc[g].astype(BF16)), g)
        ov = ovT_ref[0:m_eff, 0:n_eff]
        imp = []
        for g in groups:
            psum = pc[g][:, 0:Q]
            for r in range(1, R):
                psum = psum + pc[g][:, r * Q:(r + 1) * Q]
            imp.append(sum(_dot(ov, part) for part in _split3(psum)))
        m_idx = lax.broadcasted_iota(jnp.int32, (m_eff, Q), 0).astype(F32)
        allowed = m_idx <= bq
        forced = (m_idx == 0.0) | (m_idx == bq) | (m_idx == bq - 1.0)
        if first_class:
            score = [jnp.where(allowed, imp[g] + jnp.where(forced, SEL_BONUS, 0.0), NEG) for g in groups]
            n_pick = min(SEL_TOP, m_eff)
        else:
            free = allowed & jnp.logical_not(forced)
            score = [jnp.where(free, imp[g], NEG) for g in groups]
            n_pick = SEL_TOP - 3
        for _ in range(n_pick):
            for g in groups:
                best = jnp.max(score[g], axis=0, keepdims=True)
                first = jnp.min(jnp.where(score[g] == best, m_idx, float(m_eff)), axis=0, keepdims=True)
                score[g] = jnp.where(m_idx == first, LOWEST, score[g])
        for g in groups:
            picked = score[g] < 0.5 * LOWEST
            if not first_class:
                picked = picked | forced
            bias = jnp.where(allowed & picked, 0.0, NEG)
            bias_scr[g][0:m_eff, :] = jnp.concatenate([bias] * R, axis=1)
            if m_eff < nsel:
                bias_scr[g][m_eff:nsel, :] = jnp.full((nsel - m_eff, L), NEG, F32)

    sizes = sorted({max(Q, (nc * k // CMP_CLASSES) // Q * Q) for k in range(1, CMP_CLASSES + 1)})
    lo = 0
    for n_eff in sizes:
        hi = n_eff // (Q // CMP_STRIDE)
        pl.when((i >= lo) & (i < hi))(functools.partial(compress_and_select, n_eff, lo == 0))
        lo = hi

    for g in groups:
        m_scr[g][...] = jnp.full(m_scr[g].shape, NEG, F32)
        acc_scr[g][...] = jnp.zeros(acc_scr[g].shape, F32)
    rhs_pad = jnp.zeros((N_KVG - HEAD_DIM - BIAS_ROWS, L), BF16)
    bias_pad = jnp.zeros((BIAS_ROWS - SEL_BLOCKS_PER_TILE, L), F32)
    buf0, buf1 = (s0_scr, t0_scr), (s1_scr, t1_scr)

    def scores(kt, dst):
        for g in groups:
            brows = bias_scr[g][pl.ds(pl.multiple_of(kt * SEL_BLOCKS_PER_TILE, SEL_BLOCKS_PER_TILE),
                                      SEL_BLOCKS_PER_TILE), :]
            rhs = jnp.concatenate([qg[g], jnp.concatenate([brows, bias_pad], axis=0).astype(BF16), rhs_pad], axis=0)
            s = _dot(ksel_ref[g, kt], rhs)
            dst[0][g][...] = s
            dst[1][g][...] = jnp.max(s, axis=0, keepdims=True)

    def absorb(kt, src):
        for g in groups:
            m_old = m_scr[g][...]
            m_new = jnp.maximum(m_old, src[1][g][...])
            alpha = jnp.exp2(m_old - m_new)
            p = jnp.exp2(src[0][g][...] - m_new)
            acc_scr[g][...] = alpha * acc_scr[g][...] + _dot(vselT_ref[kt, g], p.astype(BF16))
            m_scr[g][...] = m_new

    def step(kt, src, dst):
        scores(kt + 1, dst)
        absorb(kt, src)

    def two_steps(kt):
        step(kt, buf0, buf1)
        step(kt + 1, buf1, buf0)

    k_row = lax.broadcasted_iota(jnp.int32, (Q, 1), 0)
    q_lane = lax.broadcasted_iota(jnp.int32, (1, L), 1) & (Q - 1)
    n_full = (i * Q) // SEL_TILE

    def last_tile(src):
        r0 = pl.multiple_of(i * Q - n_full * SEL_TILE, Q)
        for g in groups:
            src[0][g][pl.ds(r0, Q), :] = jnp.where(k_row <= q_lane, src[0][g][pl.ds(r0, Q), :], NEG)
            src[1][g][...] = jnp.max(src[0][g][...], axis=0, keepdims=True)
        absorb(n_full, src)

    odd = n_full & 1

    @pl.when(odd == 0)
    def _():
        scores(0, buf0)

    @pl.when(odd == 1)
    def _():
        scores(0, buf1)
        step(0, buf1, buf0)

    n_pairs = n_full >> 1

    def four_steps(j, carry):
        two_steps(odd + 4 * j)
        two_steps(odd + 4 * j + 2)
        return carry

    lax.fori_loop(0, n_pairs >> 1, four_steps, 0)

    @pl.when((n_pairs & 1) == 1)
    def _():
        two_steps(odd + 4 * (n_pairs >> 1))

    last_tile(buf0)

    sw = [[] for _ in groups]
    for w in range(WIN_TILES):
        j = i - (WIN_TILES - 1) + w
        for g in groups:
            s = _dot(kwin_refs[w][0], qz[g])
            if w == 0:
                s = jnp.where(k_row > q_lane, s, NEG)
            if w == WIN_TILES - 1:
                s = jnp.where(k_row <= q_lane, s, NEG)
            else:
                s = jnp.where(j >= 0, s, NEG)
            sw[g].append(s)
    ew, denw = [], []
    for g in groups:
        mxw = functools.reduce(jnp.maximum, [jnp.max(s, axis=0, keepdims=True) for s in sw[g]])
        ew.append([jnp.exp2(s - mxw) for s in sw[g]])
        denw.append(jnp.maximum(sum(jnp.sum(e, axis=0, keepdims=True) for e in ew[g]), 1e-30))
    owT = []
    for g in groups:
        prod = sum(_dot(vwin_refs[w][0], ew[g][w].astype(BF16)) for w in range(WIN_TILES))
        owT.append(own_rows(prod, g) * (1.0 / denw[g]))

    def gate_row(g, j):
        return jnp.concatenate([gT_ref[(g * R + r) * 3 + j:(g * R + r) * 3 + j + 1, :] for r in range(R)], axis=1)

    o_rows = []
    for g in groups:
        osT = acc_scr[g][0:HEAD_DIM, :] * (1.0 / acc_scr[g][HEAD_DIM:HEAD_DIM + 1, :])
        oT = gate_row(g, 0) * oc_scr[g][...] + gate_row(g, 1) * osT + gate_row(g, 2) * owT[g]
        o_rows += [oT[:, r * Q:(r + 1) * Q] for r in range(R)]
    o_ref[...] = jnp.concatenate(o_rows, axis=0).T


def _nsa(qT, kc, vcT, ksel, vselT, kwin, vwinT, gT):
    T = qT.shape[1]
    nb = T // Q_BLOCK
    nc = kc.shape[0]
    nsel = T // SEL_LEN
    ntile = T // SEL_TILE
    n = jnp.arange(nc)[None, :] * CMP_STRIDE
    m = jnp.arange(nsel)[:, None] * SEL_LEN
    ovT = ((n < m + SEL_LEN) & (n + CMP_LEN > m)).astype(BF16)
    ksel4 = ksel.reshape(NSA_KV_HEADS, ntile, SEL_TILE, N_KVG)
    kwin3 = kwin.reshape(nb, Q_BLOCK, N_KVG)
    L = NSA_GROUP * Q_BLOCK
    const = lambda shape: pl.BlockSpec(shape, lambda i: (0,) * len(shape))
    win_tile = lambda i, w: (jnp.maximum(i - (WIN_TILES - 1) + w, 0), 0, 0)
    kwin_specs = [pl.BlockSpec((1, Q_BLOCK, N_KVG), functools.partial(win_tile, w=w)) for w in range(WIN_TILES)]
    vwin_specs = [pl.BlockSpec((1, N_KVG, Q_BLOCK), functools.partial(win_tile, w=w)) for w in range(WIN_TILES)]
    group_scratch = [pltpu.VMEM((nsel, L), F32), pltpu.VMEM((SEL_TILE, L), F32), pltpu.VMEM((SEL_TILE, L), F32),
                     pltpu.VMEM((1, L), F32), pltpu.VMEM((1, L), F32),
                     pltpu.VMEM((1, L), F32), pltpu.VMEM((SEL_V_ROWS, L), F32), pltpu.VMEM((HEAD_DIM, L), F32)]
    return pl.pallas_call(
        _nsa_kernel,
        grid=(nb,),
        in_specs=[pl.BlockSpec((N_Q, Q_BLOCK), lambda i: (0, i)),
                  const((nc, N_KVG)), const((N_KVG, nc)), const((nsel, nc)),
                  const((NSA_KV_HEADS, ntile, SEL_TILE, N_KVG)), const((ntile, NSA_KV_HEADS, SEL_V_ROWS, SEL_TILE))]
                 + kwin_specs + vwin_specs
                 + [pl.BlockSpec((N_GATE_PAD, Q_BLOCK), lambda i: (0, i))],
        out_specs=pl.BlockSpec((Q_BLOCK, N_Q), lambda i: (i, 0)),
        out_shape=jax.ShapeDtypeStruct((T, N_Q), F32),
        scratch_shapes=group_scratch * NSA_KV_HEADS,
        compiler_params=pltpu.CompilerParams(dimension_semantics=("arbitrary",),
                                             vmem_limit_bytes=VMEM_LIMIT),
    )(qT, kc, vcT, ovT, ksel4, vselT, *([kwin3] * WIN_TILES), *([vwinT] * WIN_TILES), gT)


def _sb_kernel(qT_ref, k_ref, vT_ref, o_ref, *scr):
    i = pl.program_id(0)
    Q, W = Q_BLOCK, 2 * HEAD_DIM
    pairs = SB_HEADS // 2
    blk_row = lax.broadcasted_iota(jnp.int32, (W, 2 * Q), 0) < HEAD_DIM
    blk_lane = lax.broadcasted_iota(jnp.int32, (W, 2 * Q), 1) < Q
    zero = jnp.zeros((), BF16)
    q_pairs = []
    for pr in range(pairs):
        x = qT_ref[pr * W:(pr + 1) * W, :]
        q_pairs.append(jnp.where(blk_row == blk_lane, jnp.concatenate([x, x], axis=1), zero))
    k_row = lax.broadcasted_iota(jnp.int32, (SB_TILE, 1), 0)
    q_lane = lax.broadcasted_iota(jnp.int32, (1, SB_HEADS * Q), 1) & (Q - 1)
    rr = lax.broadcasted_iota(jnp.int32, (SB_TILE, SB_TILE), 0)
    cc = lax.broadcasted_iota(jnp.int32, (SB_TILE, SB_TILE), 1)
    from_here = (cc >= rr).astype(BF16)
    acc_scrs, c_scrs = scr[0:pairs], scr[pairs:pairs + 2]
    for buf in scr:
        buf[...] = jnp.zeros(buf.shape, F32)

    halves = ((0, 1), (2, 3))
    causal = k_row < q_lane[:, 0:4 * Q]

    def walk(tiles):
        rows = [pl.ds(pl.multiple_of(kt * SB_TILE, SB_TILE), SB_TILE) for kt, _, _ in tiles]
        chains = [(t, hf) for t in range(len(tiles)) for hf in range(2)]
        z, d, suffix = {}, {}, {}
        for t, hf in chains:
            z[t, hf] = jnp.concatenate(
                [_dot(k_ref[rows[t], pr * W:(pr + 1) * W], q_pairs[pr]) for pr in halves[hf]], axis=1)
        for t, hf in chains:
            x = jnp.maximum(z[t, hf], 0.0) + jnp.log(1.0 + jnp.exp(-jnp.abs(z[t, hf])))
            if tiles[t][1]:
                x = jnp.where(causal, x, 0.0)
            if tiles[t][2] is not None:
                x = jnp.where(tiles[t][2], x, 0.0)
            d[t, hf] = x
            hi = x.astype(BF16)
            lo = (x - hi.astype(F32)).astype(BF16)
            suffix[t, hf] = _dot(from_here, hi) + _dot(from_here, lo)
        least = None
        for hf in range(2):
            c = c_scrs[hf][...]
            for t, (kt, own, exists) in enumerate(tiles):
                a = jnp.exp(z[t, hf] - suffix[t, hf] - c)
                if own:
                    a = jnp.where(causal, a, 0.0)
                if exists is not None:
                    a = jnp.where(exists, a, 0.0)
                a = a.astype(BF16)
                for n, pr in enumerate(halves[hf]):
                    acc_scrs[pr][...] = acc_scrs[pr][...] + _dot(vT_ref[kt, pr * W:(pr + 1) * W, :],
                                                                 a[:, n * 2 * Q:(n + 1) * 2 * Q])
                c = c + jnp.sum(d[t, hf], axis=0, keepdims=True)
            c_scrs[hf][...] = c
            least = c if least is None else jnp.minimum(least, c)
        return -jnp.min(least)

    first = [(i, True, None)] + [(jnp.maximum(i - n, 0), False, i - n >= 0) for n in range(1, SB_FIRST_TILES)]
    worst0 = walk(first)

    def cond(carry):
        kt, worst = carry
        return (kt >= 0) & (worst >= SB_SKIP_LOG)

    def body(carry):
        kt, _ = carry
        return kt - 1, walk([(kt, False, None)])

    lax.while_loop(cond, body, (i - SB_FIRST_TILES, worst0))
    for pr in range(pairs):
        acc = acc_scrs[pr][...]
        o_ref[:, pr * W:(pr + 1) * W] = jnp.where(blk_row[:, 0:Q], acc[:, 0:Q], acc[:, Q:2 * Q]).T


def _stick_breaking(sbqT, sbk, sbvT):
    T = sbk.shape[0]
    nb = T // Q_BLOCK
    W = 2 * HEAD_DIM
    return pl.pallas_call(
        _sb_kernel,
        grid=(nb,),
        in_specs=[pl.BlockSpec((N_SBH, Q_BLOCK), lambda i: (0, i)),
                  pl.BlockSpec((T, N_SBH), lambda i: (0, 0)),
                  pl.BlockSpec((nb, N_SBH, Q_BLOCK), lambda i: (0, 0, 0))],
        out_specs=pl.BlockSpec((Q_BLOCK, N_SBH), lambda i: (i, 0)),
        out_shape=jax.ShapeDtypeStruct((T, N_SBH), F32),
        scratch_shapes=[pltpu.VMEM((W, 2 * Q_BLOCK), F32)] * (SB_HEADS // 2) + [pltpu.VMEM((1, 4 * Q_BLOCK), F32)] * 2,
        compiler_params=pltpu.CompilerParams(dimension_semantics=("arbitrary",),
                                             vmem_limit_bytes=VMEM_LIMIT),
    )(sbqT, sbk, sbvT)


def _mix_kernel(x_ref, on_ref, os_ref, nwn_ref, nws_ref, wo_ref, fw_ref, rhi_ref, rlo_ref, rb_ref,
                x1_ref, h2_ref, lgT_ref):
    n1 = _rms(on_ref[...], nwn_ref[...]).astype(BF16)
    n2 = _rms(os_ref[...], nws_ref[...]).astype(BF16)
    x1 = x_ref[...] + _dot(n1, wo_ref[0:N_Q]) + _dot(n2, wo_ref[N_Q:N_Q + N_SBH])
    x1_ref[...] = x1
    h2 = _rms(x1, fw_ref[...])
    hi = h2.astype(BF16)
    lo = (h2 - hi.astype(F32)).astype(BF16)
    h2_ref[...] = hi
    lg = _dot(hi, rhi_ref[...]) + _dot(hi, rlo_ref[...]) + _dot(lo, rhi_ref[...]) + rb_ref[...]
    lgT_ref[...] = lg.T[0:ROUTER_ROWS]


def _mix(x, o_nsa, o_sb, nsa_norm_w, sb_norm_w, w_out, ffn_norm_w, rg_w, rg_b, re_w, re_b):
    T, D = x.shape
    R = PROJ_ROWS
    pad = ROUTER_LANES - N_GROUPS - N_EXPERTS
    wr = jnp.concatenate([rg_w, re_w, jnp.zeros((D, pad), F32)], axis=1)
    wr_hi = wr.astype(BF16)
    wr_lo = (wr - wr_hi.astype(F32)).astype(BF16)
    rb = jnp.concatenate([rg_b, re_b, jnp.zeros((pad,), F32)]).reshape(1, ROUTER_LANES)
    full = lambda shape: pl.BlockSpec(shape, lambda i: (0,) * len(shape))
    rows = lambda n: pl.BlockSpec((R, n), lambda i: (i, 0))
    return pl.pallas_call(
        _mix_kernel,
        grid=(T // R,),
        in_specs=[rows(D), rows(N_Q), rows(N_SBH), full((1, N_Q)), full((1, N_SBH)), full((N_Q + N_SBH, D)),
                  full((1, D)), full((D, ROUTER_LANES)), full((D, ROUTER_LANES)), full((1, ROUTER_LANES))],
        out_specs=[rows(D), rows(D), pl.BlockSpec((ROUTER_ROWS, R), lambda i: (0, i))],
        out_shape=[jax.ShapeDtypeStruct((T, D), F32), jax.ShapeDtypeStruct((T, D), BF16),
                   jax.ShapeDtypeStruct((ROUTER_ROWS, T), F32)],
        compiler_params=pltpu.CompilerParams(dimension_semantics=("arbitrary",),
                                             vmem_limit_bytes=VMEM_LIMIT),
    )(x, o_nsa, o_sb, nsa_norm_w.reshape(1, N_Q), sb_norm_w.reshape(1, N_SBH), w_out.astype(BF16),
      ffn_norm_w.reshape(1, D), wr_hi, wr_lo, rb)


def _routing(lg, axis):
    pos_i = lax.broadcasted_iota(jnp.int32, lg.shape, axis)
    pos = pos_i.astype(F32)
    first_max = lambda v, mx: jnp.min(jnp.where(v == mx, pos, float(ROUTER_LANES)), axis=axis, keepdims=True)
    gl = jnp.where(pos_i < N_GROUPS, lg, -jnp.inf)
    gmax = jnp.max(gl, axis=axis, keepdims=True)
    grp = first_max(gl, gmax)
    g_gate = 1.0 / jnp.sum(jnp.exp(gl - gmax), axis=axis, keepdims=True)
    e_idx = pos_i - N_GROUPS
    e_grp = (e_idx >> 2).astype(F32)
    in_grp = (e_idx >= 0) & (e_idx < N_EXPERTS) & (e_grp == grp)
    el = jnp.where(in_grp, lg, -jnp.inf)
    top1 = jnp.max(el, axis=axis, keepdims=True)
    i1 = first_max(el, top1)
    el2 = jnp.where(pos == i1, -jnp.inf, el)
    top2 = jnp.max(el2, axis=axis, keepdims=True)
    i2 = first_max(el2, top2)
    e2 = jnp.exp(top2 - top1)
    w1 = 1.0 / (1.0 + e2)
    w2 = e2 / (1.0 + e2)
    weight = g_gate * (jnp.where(pos == i1, w1, 0.0) + jnp.where(pos == i2, w2, 0.0))
    routed = jnp.where(pos == i1, 1.0, 0.0) + jnp.where(pos == i2, 1.0, 0.0)
    return weight, routed


def _moe_kernel(h_ref, lgT_ref, x1_ref, before_ref, wg_ref, wu_ref, wd_ref, fw_ref, o_ref,
                acc_scr, rankT_scr, wparts_scr):
    e = pl.program_id(1)
    rows, width = h_ref.shape
    n_sub = rows // MOE_SUB

    @pl.when(e == 0)
    def _():
        acc_scr[...] = jnp.zeros(acc_scr.shape, F32)
        weight, routed = _routing(lgT_ref[...], 0)
        rank = _dot(routed.astype(BF16), before_ref[...])
        rankT_scr[...] = jnp.where(routed > 0.0, rank, -1.0)
        parts = [p.astype(F32) for p in _split3(weight)]
        pad = jnp.zeros((MOE_SLOT_ROWS - 3 * ROUTER_ROWS, rows), F32)
        wparts_scr[...] = jnp.concatenate(parts + [pad], axis=0).T.astype(BF16)

    experts = range(MOE_EXPERTS_PER_STEP)
    pos = [e * MOE_EXPERTS_PER_STEP + x + N_GROUPS for x in experts]
    rank_row = [rankT_scr[pl.ds(pos[x], 1), :] for x in experts]
    n_routed = jnp.max(functools.reduce(jnp.maximum, rank_row)).astype(jnp.int32) + 1
    slot_col = lax.broadcasted_iota(jnp.int32, (MOE_SLOT_ROWS, 1), 0)
    lane = lax.broadcasted_iota(jnp.int32, (n_sub * MOE_SUB_CAP, ROUTER_LANES), 1)
    y_pad = jnp.zeros((MOE_SLOT_ROWS - MOE_SUB_CAP, width), BF16)

    def chunk(ch, carry):
        want = jnp.where(slot_col < MOE_SUB_CAP, slot_col + ch * MOE_SUB_CAP, -2).astype(F32)
        xg, w_slot, spread = [], [], []
        for x in experts:
            got, got_w, back = [], [], []
            for s in range(n_sub):
                sub = slice(s * MOE_SUB, (s + 1) * MOE_SUB)
                onehot = jnp.where(rank_row[x][:, sub] == want, 1.0, 0.0)
                pick = onehot[0:MOE_SUB_CAP].astype(BF16)
                got.append(_dot(pick, h_ref[sub, :]))
                got_w.append(_dot(pick, wparts_scr[sub, :]))
                back.append(onehot.T.astype(BF16))
            xg.append(jnp.concatenate(got, axis=0).astype(BF16))
            mine = (lane == pos[x]) | (lane == pos[x] + ROUTER_ROWS) | (lane == pos[x] + 2 * ROUTER_ROWS)
            w_slot.append(jnp.sum(jnp.where(mine, jnp.concatenate(got_w, axis=0), 0.0), axis=1, keepdims=True))
            spread.append(back)
        a = [_dot(xg[x], wg_ref[x]) for x in experts]
        b = [_dot(xg[x], wu_ref[x]) for x in experts]
        act = [(a[x] * jax.nn.sigmoid(a[x]) * b[x]).astype(BF16) for x in experts]
        y = [(_dot(act[x], wd_ref[x]) * w_slot[x]).astype(BF16) for x in experts]
        for s in range(n_sub):
            back = jnp.concatenate([spread[x][s] for x in experts], axis=1)
            vals = jnp.concatenate([jnp.concatenate([y[x][s * MOE_SUB_CAP:(s + 1) * MOE_SUB_CAP], y_pad], axis=0)
                                    for x in experts], axis=0)
            acc_scr[s * MOE_SUB:(s + 1) * MOE_SUB, :] += _dot(back, vals)
        return carry

    lax.fori_loop(0, (n_routed + (MOE_SUB_CAP - 1)) // MOE_SUB_CAP, chunk, 0)

    @pl.when(e == N_EXPERTS // MOE_EXPERTS_PER_STEP - 1)
    def _():
        o_ref[...] = _rms(x1_ref[...] + acc_scr[...], fw_ref[...])


def _moe(h2, logitsT, x1, w_gate, w_up, w_down, final_norm_w):
    T, D = x1.shape
    R = min(MOE_ROWS, T)
    tok = jnp.arange(R)
    same_sub = (tok[:, None] >> MOE_SUB_SHIFT) == (tok[None, :] >> MOE_SUB_SHIFT)
    before = ((tok[:, None] < tok[None, :]) & same_sub).astype(BF16)
    rows = lambda n: pl.BlockSpec((R, n), lambda i, e: (i, 0))
    const = lambda a, b: pl.BlockSpec((a, b), lambda i, e: (0, 0))
    per_expert = lambda a, b: pl.BlockSpec((MOE_EXPERTS_PER_STEP, a, b), lambda i, e: (e, 0, 0))
    return pl.pallas_call(
        _moe_kernel,
        grid=(T // R, N_EXPERTS // MOE_EXPERTS_PER_STEP),
        in_specs=[rows(D), pl.BlockSpec((ROUTER_ROWS, R), lambda i, e: (0, i)), rows(D), const(R, R),
                  per_expert(D, EXPERT_FF), per_expert(D, EXPERT_FF), per_expert(EXPERT_FF, D), const(1, D)],
        out_specs=rows(D),
        out_shape=jax.ShapeDtypeStruct((T, D), F32),
        scratch_shapes=[pltpu.VMEM((R, D), F32), pltpu.VMEM((ROUTER_ROWS, R), F32),
                        pltpu.VMEM((R, ROUTER_LANES), BF16)],
        compiler_params=pltpu.CompilerParams(dimension_semantics=("arbitrary", "arbitrary"),
                                             vmem_limit_bytes=VMEM_LIMIT),
    )(h2, logitsT, x1, before, w_gate.astype(BF16), w_up.astype(BF16), w_down.astype(BF16),
      final_norm_w.reshape(1, D))


def kernel(x, positions, attn_norm_w, w_in, cmp_pe_k, cmp_pe_v, cmp_k_w1, cmp_k_w2, cmp_v_w1, cmp_v_w2,
           nsa_out_norm_w, sb_out_norm_w, w_out, ffn_norm_w, router_group_w, router_group_b,
           router_expert_w, router_expert_b, w_gate, w_up, w_down, final_norm_w):
    B, T, D = x.shape
    assert B == 1 and T % SEL_TILE == 0 and T % PROJ_ROWS == 0 and T // SEL_LEN >= SEL_TOP
    assert attn_norm_w.shape[0] == 1, "the final norm is fused into the (single) layer's MoE kernel"
    xs = x.reshape(T, D)
    pos = positions.reshape(T)
    (cmpk, cmpv, sbq, sbk, sbv, qT, ksel, kwin, vselT, vwinT, gT) = _project(xs, pos, attn_norm_w[0], w_in[0])
    kc, vcT = _compress(cmpk, cmpv, cmp_pe_k[0], cmp_pe_v[0], cmp_k_w1[0], cmp_k_w2[0], cmp_v_w1[0], cmp_v_w2[0])
    o_nsa = _nsa(qT, kc, vcT, ksel, vselT, kwin, vwinT, gT)
    o_sb = _stick_breaking(sbq, sbk, sbv)
    x1, h2, logitsT = _mix(xs, o_nsa, o_sb, nsa_out_norm_w[0], sb_out_norm_w[0], w_out[0], ffn_norm_w[0],
                           router_group_w[0], router_group_b[0], router_expert_w[0], router_expert_b[0])
    out = _moe(h2, logitsT, x1, w_gate[0], w_up[0], w_down[0], final_norm_w)
    return out.reshape(B, T, D)
```

```python
import functools

import jax
import jax.numpy as jnp
from jax import lax
from jax.experimental import pallas as pl
from jax.experimental.pallas import tpu as pltpu

HEAD_DIM = 64
NSA_HEADS = 8
NSA_KV_HEADS = 2
NSA_GROUP = NSA_HEADS // NSA_KV_HEADS
SB_HEADS = 8
ROPE_THETA = 500000.0
ROPE_DIM = HEAD_DIM // 4
ROPE_HALF = ROPE_DIM // 2
CMP_LEN = 32
CMP_STRIDE = 16
CMP_HIDDEN = 256
SEL_LEN = 64
SEL_TOP = 16
SEL_BONUS = 1.0e4
WINDOW = 512
Q_BLOCK = 128
N_GROUPS = 4
EXPERTS_PER_GROUP = 4
N_EXPERTS = N_GROUPS * EXPERTS_PER_GROUP
EXPERT_FF = 512
EPS = 1e-6
NEG = -1e30
LOWEST = -3.0e38

N_Q = NSA_HEADS * HEAD_DIM
N_KVG = NSA_KV_HEADS * HEAD_DIM
N_GATE = NSA_HEADS * 3
N_GATE_PAD = 32
N_SBH = SB_HEADS * HEAD_DIM
SCALE = HEAD_DIM ** -0.5
LOG2E = 1.4426950408889634
SEL_BLOCKS_PER_TILE = 8
BIAS_ROWS = 16
CMP_CLASSES = 8
SEL_V_ROWS = HEAD_DIM + 16

PROJ_ROWS = 512
SEL_TILE = 512
WIN_TILES = WINDOW // Q_BLOCK + 1
SB_TILE = 128
MOE_ROWS = 1024
MOE_EXPERTS_PER_STEP = 4
MOE_SLOT_ROWS = 128
MOE_SUB_SHIFT = 8
MOE_SUB = 1 << MOE_SUB_SHIFT
MOE_SUB_CAP = 48
ROUTER_LANES = 128
ROUTER_ROWS = 32
SB_FIRST_TILES = 3
SB_SKIP_LOG = -104.0

VMEM_LIMIT = 56 * 1024 * 1024

BF16 = jnp.bfloat16
F32 = jnp.float32


def _rms(x, w):
    return x * lax.rsqrt(jnp.mean(x * x, axis=-1, keepdims=True) + EPS) * w


def _dot(a, b):
    return jnp.dot(a, b, preferred_element_type=F32)


def _dot_nt(a, b):
    return lax.dot_general(a, b, (((1,), (1,)), ((), ())), preferred_element_type=F32)


def _split3(x):
    hi = x.astype(BF16)
    r1 = x - hi.astype(F32)
    mid = r1.astype(BF16)
    lo = (r1 - mid.astype(F32)).astype(BF16)
    return hi, mid, lo


def _proj_kernel(x_ref, nw_ref, wn_ref, wt_ref, pos_ref, invf_ref,
                 cmpk_ref, cmpv_ref, sbqT_ref, sbk_ref, sbvT_ref,
                 qT_ref, ksel_ref, kwin_ref, vselT_ref, vwinT_ref, gT_ref):
    h = _rms(x_ref[...], nw_ref[...]).astype(BF16)
    p1 = _dot(h, wn_ref[...])
    cmpk_ref[...] = p1[:, 0:N_KVG]
    cmpv_ref[...] = p1[:, N_KVG:2 * N_KVG]
    sbk_ref[...] = p1[:, 2 * N_KVG:2 * N_KVG + N_SBH].astype(BF16)

    p2 = _dot_nt(wt_ref[...], h)
    o = N_Q + 4 * N_KVG + N_GATE_PAD
    sbqT_ref[...] = (p2[o:o + N_SBH] * SCALE).astype(BF16)
    sbv = p2[o + N_SBH:o + 2 * N_SBH].astype(BF16)
    for j in range(PROJ_ROWS // Q_BLOCK):
        sbvT_ref[j] = sbv[:, j * Q_BLOCK:(j + 1) * Q_BLOCK]
    ang = invf_ref[...] * pos_ref[...].astype(F32)
    cos, sin = jnp.cos(ang), jnp.sin(ang)
    n_rope_heads = NSA_HEADS + 2 * NSA_KV_HEADS
    roped = []
    for hd in range(n_rope_heads):
        blk = p2[hd * HEAD_DIM:(hd + 1) * HEAD_DIM]
        x1, x2 = blk[0:ROPE_HALF], blk[ROPE_HALF:ROPE_DIM]
        roped.append(jnp.concatenate(
            [x1 * cos - x2 * sin, x2 * cos + x1 * sin, blk[ROPE_DIM:]], axis=0))
    qT_ref[...] = (jnp.concatenate(roped[:NSA_HEADS], axis=0) * (SCALE * LOG2E)).astype(BF16)
    kT = jnp.concatenate(roped[NSA_HEADS:], axis=0)
    kn = kT.T.astype(BF16)
    r_blk = lax.broadcasted_iota(jnp.int32, (PROJ_ROWS, HEAD_DIM), 0) >> 6
    c_idx = lax.broadcasted_iota(jnp.int32, (PROJ_ROWS, HEAD_DIM), 1)
    onehot = jnp.where(r_blk == c_idx, 1.0, 0.0).astype(BF16)
    for gk in range(NSA_KV_HEADS):
        ksel_ref[gk] = jnp.concatenate([kn[:, gk * HEAD_DIM:(gk + 1) * HEAD_DIM], onehot], axis=1)
    kwin_ref[...] = kn[:, N_KVG:2 * N_KVG]
    o = N_Q + 2 * N_KVG
    for gk in range(NSA_KV_HEADS):
        vselT_ref[0, gk] = jnp.concatenate(
            [p2[o + gk * HEAD_DIM:o + (gk + 1) * HEAD_DIM],
             jnp.where(lax.broadcasted_iota(jnp.int32, (SEL_V_ROWS - HEAD_DIM, PROJ_ROWS), 0) == 0, 1.0, 0.0)],
            axis=0).astype(BF16)
    vw = p2[o + N_KVG:o + 2 * N_KVG].astype(BF16)
    for j in range(PROJ_ROWS // Q_BLOCK):
        vwinT_ref[j] = vw[:, j * Q_BLOCK:(j + 1) * Q_BLOCK]
    o = o + 2 * N_KVG
    gT_ref[...] = jax.nn.sigmoid(p2[o:o + N_GATE_PAD])


def _project(x, positions, attn_norm_w, w_in):
    T, D = x.shape
    R = PROJ_ROWS
    o1, o2, o3 = N_Q, N_Q + 6 * N_KVG, N_Q + 6 * N_KVG + N_GATE
    kv = lambda i: w_in[:, o1 + i * N_KVG:o1 + (i + 1) * N_KVG]
    sb = lambda i: w_in[:, o3 + i * N_SBH:o3 + (i + 1) * N_SBH]
    w_nat = jnp.concatenate([kv(0), kv(1), sb(1)], axis=1).astype(BF16)
    w_t = jnp.concatenate(
        [w_in[:, :o1], kv(2), kv(4), kv(3), kv(5), w_in[:, o2:o3],
         jnp.zeros((D, N_GATE_PAD - N_GATE), w_in.dtype), sb(0), sb(2)], axis=1).T.astype(BF16)
    inv_freq = ROPE_THETA ** (-jnp.arange(0, ROPE_DIM, 2, dtype=F32) / ROPE_DIM)
    n_nat, n_t = w_nat.shape[1], w_t.shape[0]
    full = lambda shape: pl.BlockSpec(shape, lambda i: (0,) * len(shape))
    rows = lambda n: pl.BlockSpec((R, n), lambda i: (i, 0))
    cols = lambda n: pl.BlockSpec((n, R), lambda i: (0, i))
    return pl.pallas_call(
        _proj_kernel,
        grid=(T // R,),
        in_specs=[rows(D), full((1, D)), full((D, n_nat)), full((n_t, D)), cols(1), full((ROPE_HALF, 1))],
        out_specs=[rows(N_KVG), rows(N_KVG), cols(N_SBH), rows(N_SBH),
                   pl.BlockSpec((R // Q_BLOCK, N_SBH, Q_BLOCK), lambda i: (i, 0, 0)),
                   cols(N_Q), pl.BlockSpec((NSA_KV_HEADS, R, N_KVG), lambda i: (0, i, 0)), rows(N_KVG),
                   pl.BlockSpec((1, NSA_KV_HEADS, SEL_V_ROWS, R), lambda i: (i, 0, 0, 0)),
                   pl.BlockSpec((R // Q_BLOCK, N_KVG, Q_BLOCK), lambda i: (i, 0, 0)),
                   cols(N_GATE_PAD)],
        out_shape=[jax.ShapeDtypeStruct((T, N_KVG), F32), jax.ShapeDtypeStruct((T, N_KVG), F32),
                   jax.ShapeDtypeStruct((N_SBH, T), BF16), jax.ShapeDtypeStruct((T, N_SBH), BF16),
                   jax.ShapeDtypeStruct((T // Q_BLOCK, N_SBH, Q_BLOCK), BF16),
                   jax.ShapeDtypeStruct((N_Q, T), BF16),
                   jax.ShapeDtypeStruct((NSA_KV_HEADS, T, N_KVG), BF16), jax.ShapeDtypeStruct((T, N_KVG), BF16),
                   jax.ShapeDtypeStruct((T // R, NSA_KV_HEADS, SEL_V_ROWS, R), BF16),
                   jax.ShapeDtypeStruct((T // Q_BLOCK, N_KVG, Q_BLOCK), BF16),
                   jax.ShapeDtypeStruct((N_GATE_PAD, T), F32)],
        compiler_params=pltpu.CompilerParams(dimension_semantics=("arbitrary",),
                                             vmem_limit_bytes=VMEM_LIMIT),
    )(x, attn_norm_w.reshape(1, D), w_nat, w_t, positions.reshape(1, T), inv_freq.reshape(ROPE_HALF, 1))


def _compress_kernel(x_ref, pea_ref, peb_ref, wa_ref, wb_ref, w2_ref, nat_ref, tr_ref):
    x = x_ref[0]
    nc = x.shape[0]
    ha = _dot((x + pea_ref[0]).astype(BF16), wa_ref[0])
    hb = _dot((x + peb_ref[0]).astype(BF16), wb_ref[0])
    hid = ha + pltpu.roll(hb, nc - 1, 0)
    act = (hid * jax.nn.sigmoid(hid)).astype(BF16)
    out = _dot(act, w2_ref[0])
    nat_ref[0] = out.astype(BF16)
    tr_ref[0] = out.T.astype(BF16)


def _compress(cmpk, cmpv, pe_k, pe_v, k_w1, k_w2, v_w1, v_w2):
    T = cmpk.shape[0]
    nc = T // CMP_STRIDE
    half = CMP_LEN // 2
    G = NSA_KV_HEADS
    width = half * N_KVG
    x = jnp.stack([cmpk.reshape(nc, width), cmpv.reshape(nc, width)])
    eye = jnp.eye(G, dtype=F32)

    def pe_rows(pe):
        return jnp.broadcast_to(pe[:, None, :], (half, G, HEAD_DIM)).reshape(1, width)

    def w1_block(w1):
        w = w1.reshape(half, HEAD_DIM, CMP_HIDDEN)
        return jnp.einsum('ldj,gh->lgdhj', w, eye).reshape(width, G * CMP_HIDDEN).astype(BF16)

    def w2_block(w2):
        return jnp.einsum('jd,gh->gjhd', w2, eye).reshape(G * CMP_HIDDEN, N_KVG).astype(BF16)

    hw = half * HEAD_DIM
    pea = jnp.stack([pe_rows(pe_k[:half]), pe_rows(pe_v[:half])])
    peb = jnp.stack([pe_rows(pe_k[half:]), pe_rows(pe_v[half:])])
    wa = jnp.stack([w1_block(k_w1[:hw]), w1_block(v_w1[:hw])])
    wb = jnp.stack([w1_block(k_w1[hw:]), w1_block(v_w1[hw:])])
    w2 = jnp.stack([w2_block(k_w2), w2_block(v_w2)])
    blk = lambda a, b: pl.BlockSpec((1, a, b), lambda i: (i, 0, 0))
    nat, tr = pl.pallas_call(
        _compress_kernel,
        grid=(2,),
        in_specs=[blk(nc, width), blk(1, width), blk(1, width), blk(width, G * CMP_HIDDEN),
                  blk(width, G * CMP_HIDDEN), blk(G * CMP_HIDDEN, N_KVG)],
        out_specs=[blk(nc, N_KVG), blk(N_KVG, nc)],
        out_shape=[jax.ShapeDtypeStruct((2, nc, N_KVG), BF16), jax.ShapeDtypeStruct((2, N_KVG, nc), BF16)],
        compiler_params=pltpu.CompilerParams(dimension_semantics=("arbitrary",),
                                             vmem_limit_bytes=VMEM_LIMIT),
    )(x, pea, peb, wa, wb, w2)
    return nat[0], tr[1]


def _nsa_kernel(q_ref, kc_ref, vcT_ref, ovT_ref, ksel_ref, vselT_ref, *rest):
    kwin_refs = rest[0:WIN_TILES]
    vwin_refs = rest[WIN_TILES:2 * WIN_TILES]
    gT_ref, o_ref = rest[2 * WIN_TILES:2 * WIN_TILES + 2]
    scratch = rest[2 * WIN_TILES + 2:]
    G = NSA_KV_HEADS
    per = len(scratch) // G
    bias_scr, s0_scr, s1_scr, t0_scr, t1_scr, m_scr, acc_scr, oc_scr = (
        [scratch[g * per + n] for g in range(G)] for n in range(per))
    groups = range(G)
    i = pl.program_id(0)
    R, Q = NSA_GROUP, Q_BLOCK
    L = R * Q
    nsel = bias_scr[0].shape[0]

    zero = jnp.zeros((HEAD_DIM, L), BF16)
    qg, qz = [], []
    for g in groups:
        q = jnp.concatenate([q_ref[(g * R + r) * HEAD_DIM:(g * R + r + 1) * HEAD_DIM, :] for r in range(R)], axis=1)
        qg.append(q)
        qz.append(jnp.concatenate([q if gg == g else zero for gg in groups], axis=0))

    def own_rows(x, g):
        return x[g * HEAD_DIM:(g + 1) * HEAD_DIM]

    t_lane = i * Q + (lax.broadcasted_iota(jnp.int32, (1, L), 1) & (Q - 1))

    nc = kc_ref.shape[0]
    q_pos = i * Q + lax.broadcasted_iota(jnp.int32, (1, Q), 1)
    bq = (q_pos >> 6).astype(F32)

    def compress_and_select(n_eff, first_class):
        m_eff = n_eff // (SEL_LEN // CMP_STRIDE)
        cmp_end = lax.broadcasted_iota(jnp.int32, (n_eff, 1), 0) * CMP_STRIDE + (CMP_LEN - 1)
        visible = cmp_end <= t_lane
        sc = [_dot(kc_ref[0:n_eff, :], qz[g]) for g in groups]
        pc = []
        for g in groups:
            s = jnp.where(visible, sc[g], NEG)
            mxc = jnp.max(s, axis=0, keepdims=True)
            mxc = jnp.where(mxc < 0.5 * NEG, 0.0, mxc)
            ec = jnp.exp2(s - mxc)
            pc.append(ec * (1.0 / jnp.maximum(jnp.sum(ec, axis=0, keepdims=True), 1e-30)))
        for g in groups:
            oc_scr[g][...] = own_rows(_dot(vcT_ref[:, 0:n_eff], pc[g].astype(BF16)), g)
        ov = ovT_ref[0:m_eff, 0:n_eff]
        imp = []
        for g in groups:
            psum = pc[g][:, 0:Q]
            for r in range(1, R):
                psum = psum + pc[g][:, r * Q:(r + 1) * Q]
            imp.append(sum(_dot(ov, part) for part in _split3(psum)))
        m_idx = lax.broadcasted_iota(jnp.int32, (m_eff, Q), 0).astype(F32)
        allowed = m_idx <= bq
        forced = (m_idx == 0.0) | (m_idx == bq) | (m_idx == bq - 1.0)
        if first_class:
            score = [jnp.where(allowed, imp[g] + jnp.where(forced, SEL_BONUS, 0.0), NEG) for g in groups]
            n_pick = min(SEL_TOP, m_eff)
        else:
            free = allowed & jnp.logical_not(forced)
            score = [jnp.where(free, imp[g], NEG) for g in groups]
            n_pick = SEL_TOP - 3
        for _ in range(n_pick):
            for g in groups:
                best = jnp.max(score[g], axis=0, keepdims=True)
                first = jnp.min(jnp.where(score[g] == best, m_idx, float(m_eff)), axis=0, keepdims=True)
                score[g] = jnp.where(m_idx == first, LOWEST, score[g])
        for g in groups:
            picked = score[g] < 0.5 * LOWEST
            if not first_class:
                picked = picked | forced
            bias = jnp.where(allowed & picked, 0.0, NEG)
            bias_scr[g][0:m_eff, :] = jnp.concatenate([bias] * R, axis=1)
            if m_eff < nsel:
                bias_scr[g][m_eff:nsel, :] = jnp.full((nsel - m_eff, L), NEG, F32)

    sizes = sorted({max(Q, (nc * k // CMP_CLASSES) // Q * Q) for k in range(1, CMP_CLASSES + 1)})
    lo = 0
    for n_eff in sizes:
        hi = n_eff // (Q // CMP_STRIDE)
        pl.when((i >= lo) & (i < hi))(functools.partial(compress_and_select, n_eff, lo == 0))
        lo = hi

    for g in groups:
        m_scr[g][...] = jnp.full(m_scr[g].shape, NEG, F32)
        acc_scr[g][...] = jnp.zeros(acc_scr[g].shape, F32)
    rhs_pad = jnp.zeros((N_KVG - HEAD_DIM - BIAS_ROWS, L), BF16)
    bias_pad = jnp.zeros((BIAS_ROWS - SEL_BLOCKS_PER_TILE, L), F32)
    buf0, buf1 = (s0_scr, t0_scr), (s1_scr, t1_scr)

    def scores(kt, dst):
        for g in groups:
            brows = bias_scr[g][pl.ds(pl.multiple_of(kt * SEL_BLOCKS_PER_TILE, SEL_BLOCKS_PER_TILE),
                                      SEL_BLOCKS_PER_TILE), :]
            rhs = jnp.concatenate([qg[g], jnp.concatenate([brows, bias_pad], axis=0).astype(BF16), rhs_pad], axis=0)
            s = _dot(ksel_ref[g, kt], rhs)
            dst[0][g][...] = s
            dst[1][g][...] = jnp.max(s, axis=0, keepdims=True)

    def absorb(kt, src):
        for g in groups:
            m_old = m_scr[g][...]
            m_new = jnp.maximum(m_old, src[1][g][...])
            alpha = jnp.exp2(m_old - m_new)
            p = jnp.exp2(src[0][g][...] - m_new)
            acc_scr[g][...] = alpha * acc_scr[g][...] + _dot(vselT_ref[kt, g], p.astype(BF16))
            m_scr[g][...] = m_new

    def step(kt, src, dst):
        scores(kt + 1, dst)
        absorb(kt, src)

    def two_steps(kt):
        step(kt, buf0, buf1)
        step(kt + 1, buf1, buf0)

    k_row = lax.broadcasted_iota(jnp.int32, (Q, 1), 0)
    q_lane = lax.broadcasted_iota(jnp.int32, (1, L), 1) & (Q - 1)
    n_full = (i * Q) // SEL_TILE

    def last_tile(src):
        r0 = pl.multiple_of(i * Q - n_full * SEL_TILE, Q)
        for g in groups:
            src[0][g][pl.ds(r0, Q), :] = jnp.where(k_row <= q_lane, src[0][g][pl.ds(r0, Q), :], NEG)
            src[1][g][...] = jnp.max(src[0][g][...], axis=0, keepdims=True)
        absorb(n_full, src)

    odd = n_full & 1

    @pl.when(odd == 0)
    def _():
        scores(0, buf0)

    @pl.when(odd == 1)
    def _():
        scores(0, buf1)
        step(0, buf1, buf0)

    n_pairs = n_full >> 1

    def four_steps(j, carry):
        two_steps(odd + 4 * j)
        two_steps(odd + 4 * j + 2)
        return carry

    lax.fori_loop(0, n_pairs >> 1, four_steps, 0)

    @pl.when((n_pairs & 1) == 1)
    def _():
        two_steps(odd + 4 * (n_pairs >> 1))

    last_tile(buf0)

    sw = [[] for _ in groups]
    for w in range(WIN_TILES):
        j = i - (WIN_TILES - 1) + w
        for g in groups:
            s = _dot(kwin_refs[w][0], qz[g])
            if w == 0:
                s = jnp.where(k_row > q_lane, s, NEG)
            if w == WIN_TILES - 1:
                s = jnp.where(k_row <= q_lane, s, NEG)
            else:
                s = jnp.where(j >= 0, s, NEG)
            sw[g].append(s)
    ew, denw = [], []
    for g in groups:
        mxw = functools.reduce(jnp.maximum, [jnp.max(s, axis=0, keepdims=True) for s in sw[g]])
        ew.append([jnp.exp2(s - mxw) for s in sw[g]])
        denw.append(jnp.maximum(sum(jnp.sum(e, axis=0, keepdims=True) for e in ew[g]), 1e-30))
    owT = []
    for g in groups:
        prod = sum(_dot(vwin_refs[w][0], ew[g][w].astype(BF16)) for w in range(WIN_TILES))
        owT.append(own_rows(prod, g) * (1.0 / denw[g]))

    def gate_row(g, j):
        return jnp.concatenate([gT_ref[(g * R + r) * 3 + j:(g * R + r) * 3 + j + 1, :] for r in range(R)], axis=1)

    o_rows = []
    for g in groups:
        osT = acc_scr[g][0:HEAD_DIM, :] * (1.0 / acc_scr[g][HEAD_DIM:HEAD_DIM + 1, :])
        oT = gate_row(g, 0) * oc_scr[g][...] + gate_row(g, 1) * osT + gate_row(g, 2) * owT[g]
        o_rows += [oT[:, r * Q:(r + 1) * Q] for r in range(R)]
    o_ref[...] = jnp.concatenate(o_rows, axis=0).T


def _nsa(qT, kc, vcT, ksel, vselT, kwin, vwinT, gT):
    T = qT.shape[1]
    nb = T // Q_BLOCK
    nc = kc.shape[0]
    nsel = T // SEL_LEN
    ntile = T // SEL_TILE
    n = jnp.arange(nc)[None, :] * CMP_STRIDE
    m = jnp.arange(nsel)[:, None] * SEL_LEN
    ovT = ((n < m + SEL_LEN) & (n + CMP_LEN > m)).astype(BF16)
    ksel4 = ksel.reshape(NSA_KV_HEADS, ntile, SEL_TILE, N_KVG)
    kwin3 = kwin.reshape(nb, Q_BLOCK, N_KVG)
    L = NSA_GROUP * Q_BLOCK
    const = lambda shape: pl.BlockSpec(shape, lambda i: (0,) * len(shape))
    win_tile = lambda i, w: (jnp.maximum(i - (WIN_TILES - 1) + w, 0), 0, 0)
    kwin_specs = [pl.BlockSpec((1, Q_BLOCK, N_KVG), functools.partial(win_tile, w=w)) for w in range(WIN_TILES)]
    vwin_specs = [pl.BlockSpec((1, N_KVG, Q_BLOCK), functools.partial(win_tile, w=w)) for w in range(WIN_TILES)]
    group_scratch = [pltpu.VMEM((nsel, L), F32), pltpu.VMEM((SEL_TILE, L), F32), pltpu.VMEM((SEL_TILE, L), F32),
                     pltpu.VMEM((1, L), F32), pltpu.VMEM((1, L), F32),
                     pltpu.VMEM((1, L), F32), pltpu.VMEM((SEL_V_ROWS, L), F32), pltpu.VMEM((HEAD_DIM, L), F32)]
    return pl.pallas_call(
        _nsa_kernel,
        grid=(nb,),
        in_specs=[pl.BlockSpec((N_Q, Q_BLOCK), lambda i: (0, i)),
                  const((nc, N_KVG)), const((N_KVG, nc)), const((nsel, nc)),
                  const((NSA_KV_HEADS, ntile, SEL_TILE, N_KVG)), const((ntile, NSA_KV_HEADS, SEL_V_ROWS, SEL_TILE))]
                 + kwin_specs + vwin_specs
                 + [pl.BlockSpec((N_GATE_PAD, Q_BLOCK), lambda i: (0, i))],
        out_specs=pl.BlockSpec((Q_BLOCK, N_Q), lambda i: (i, 0)),
        out_shape=jax.ShapeDtypeStruct((T, N_Q), F32),
        scratch_shapes=group_scratch * NSA_KV_HEADS,
        compiler_params=pltpu.CompilerParams(dimension_semantics=("arbitrary",),
                                             vmem_limit_bytes=VMEM_LIMIT),
    )(qT, kc, vcT, ovT, ksel4, vselT, *([kwin3] * WIN_TILES), *([vwinT] * WIN_TILES), gT)


def _sb_kernel(qT_ref, k_ref, vT_ref, o_ref, *scr):
    i = pl.program_id(0)
    Q, W = Q_BLOCK, 2 * HEAD_DIM
    pairs = SB_HEADS // 2
    blk_row = lax.broadcasted_iota(jnp.int32, (W, 2 * Q), 0) < HEAD_DIM
    blk_lane = lax.broadcasted_iota(jnp.int32, (W, 2 * Q), 1) < Q
    zero = jnp.zeros((), BF16)
    q_pairs = []
    for pr in range(pairs):
        x = qT_ref[pr * W:(pr + 1) * W, :]
        q_pairs.append(jnp.where(blk_row == blk_lane, jnp.concatenate([x, x], axis=1), zero))
    k_row = lax.broadcasted_iota(jnp.int32, (SB_TILE, 1), 0)
    q_lane = lax.broadcasted_iota(jnp.int32, (1, SB_HEADS * Q), 1) & (Q - 1)
    rr = lax.broadcasted_iota(jnp.int32, (SB_TILE, SB_TILE), 0)
    cc = lax.broadcasted_iota(jnp.int32, (SB_TILE, SB_TILE), 1)
    from_here = (cc >= rr).astype(BF16)
    acc_scrs, c_scrs = scr[0:pairs], scr[pairs:pairs + 2]
    for buf in scr:
        buf[...] = jnp.zeros(buf.shape, F32)

    halves = ((0, 1), (2, 3))
    causal = k_row < q_lane[:, 0:4 * Q]

    def walk(tiles):
        rows = [pl.ds(pl.multiple_of(kt * SB_TILE, SB_TILE), SB_TILE) for kt, _, _ in tiles]
        chains = [(t, hf) for t in range(len(tiles)) for hf in range(2)]
        z, d, suffix = {}, {}, {}
        for t, hf in chains:
            z[t, hf] = jnp.concatenate(
                [_dot(k_ref[rows[t], pr * W:(pr + 1) * W], q_pairs[pr]) for pr in halves[hf]], axis=1)
        for t, hf in chains:
            x = jnp.maximum(z[t, hf], 0.0) + jnp.log(1.0 + jnp.exp(-jnp.abs(z[t, hf])))
            if tiles[t][1]:
                x = jnp.where(causal, x, 0.0)
            if tiles[t][2] is not None:
                x = jnp.where(tiles[t][2], x, 0.0)
            d[t, hf] = x
            hi = x.astype(BF16)
            lo = (x - hi.astype(F32)).astype(BF16)
            suffix[t, hf] = _dot(from_here, hi) + _dot(from_here, lo)
        least = None
        for hf in range(2):
            c = c_scrs[hf][...]
            for t, (kt, own, exists) in enumerate(tiles):
                a = jnp.exp(z[t, hf] - suffix[t, hf] - c)
                if own:
                    a = jnp.where(causal, a, 0.0)
                if exists is not None:
                    a = jnp.where(exists, a, 0.0)
                a = a.astype(BF16)
                for n, pr in enumerate(halves[hf]):
                    acc_scrs[pr][...] = acc_scrs[pr][...] + _dot(vT_ref[kt, pr * W:(pr + 1) * W, :],
                                                                 a[:, n * 2 * Q:(n + 1) * 2 * Q])
                c = c + jnp.sum(d[t, hf], axis=0, keepdims=True)
            c_scrs[hf][...] = c
            least = c if least is None else jnp.minimum(least, c)
        return -jnp.min(least)

    first = [(i, True, None)] + [(jnp.maximum(i - n, 0), False, i - n >= 0) for n in range(1, SB_FIRST_TILES)]
    worst0 = walk(first)

    def cond(carry):
        kt, worst = carry
        return (kt >= 0) & (worst >= SB_SKIP_LOG)

    def body(carry):
        kt, _ = carry
        return kt - 1, walk([(kt, False, None)])

    lax.while_loop(cond, body, (i - SB_FIRST_TILES, worst0))
    for pr in range(pairs):
        acc = acc_scrs[pr][...]
        o_ref[:, pr * W:(pr + 1) * W] = jnp.where(blk_row[:, 0:Q], acc[:, 0:Q], acc[:, Q:2 * Q]).T


def _stick_breaking(sbqT, sbk, sbvT):
    T = sbk.shape[0]
    nb = T // Q_BLOCK
    W = 2 * HEAD_DIM
    return pl.pallas_call(
        _sb_kernel,
        grid=(nb,),
        in_specs=[pl.BlockSpec((N_SBH, Q_BLOCK), lambda i: (0, i)),
                  pl.BlockSpec((T, N_SBH), lambda i: (0, 0)),
                  pl.BlockSpec((nb, N_SBH, Q_BLOCK), lambda i: (0, 0, 0))],
        out_specs=pl.BlockSpec((Q_BLOCK, N_SBH), lambda i: (i, 0)),
        out_shape=jax.ShapeDtypeStruct((T, N_SBH), F32),
        scratch_shapes=[pltpu.VMEM((W, 2 * Q_BLOCK), F32)] * (SB_HEADS // 2) + [pltpu.VMEM((1, 4 * Q_BLOCK), F32)] * 2,
        compiler_params=pltpu.CompilerParams(dimension_semantics=("arbitrary",),
                                             vmem_limit_bytes=VMEM_LIMIT),
    )(sbqT, sbk, sbvT)


def _mix_kernel(x_ref, on_ref, os_ref, nwn_ref, nws_ref, wo_ref, fw_ref, rhi_ref, rlo_ref, rb_ref,
                x1_ref, h2_ref, lgT_ref):
    n1 = _rms(on_ref[...], nwn_ref[...]).astype(BF16)
    n2 = _rms(os_ref[...], nws_ref[...]).astype(BF16)
    x1 = x_ref[...] + _dot(n1, wo_ref[0:N_Q]) + _dot(n2, wo_ref[N_Q:N_Q + N_SBH])
    x1_ref[...] = x1
    h2 = _rms(x1, fw_ref[...])
    hi = h2.astype(BF16)
    lo = (h2 - hi.astype(F32)).astype(BF16)
    h2_ref[...] = hi
    lg = _dot(hi, rhi_ref[...]) + _dot(hi, rlo_ref[...]) + _dot(lo, rhi_ref[...]) + rb_ref[...]
    lgT_ref[...] = lg.T[0:ROUTER_ROWS]


def _mix(x, o_nsa, o_sb, nsa_norm_w, sb_norm_w, w_out, ffn_norm_w, rg_w, rg_b, re_w, re_b):
    T, D = x.shape
    R = PROJ_ROWS
    pad = ROUTER_LANES - N_GROUPS - N_EXPERTS
    wr = jnp.concatenate([rg_w, re_w, jnp.zeros((D, pad), F32)], axis=1)
    wr_hi = wr.astype(BF16)
    wr_lo = (wr - wr_hi.astype(F32)).astype(BF16)
    rb = jnp.concatenate([rg_b, re_b, jnp.zeros((pad,), F32)]).reshape(1, ROUTER_LANES)
    full = lambda shape: pl.BlockSpec(shape, lambda i: (0,) * len(shape))
    rows = lambda n: pl.BlockSpec((R, n), lambda i: (i, 0))
    return pl.pallas_call(
        _mix_kernel,
        grid=(T // R,),
        in_specs=[rows(D), rows(N_Q), rows(N_SBH), full((1, N_Q)), full((1, N_SBH)), full((N_Q + N_SBH, D)),
                  full((1, D)), full((D, ROUTER_LANES)), full((D, ROUTER_LANES)), full((1, ROUTER_LANES))],
        out_specs=[rows(D), rows(D), pl.BlockSpec((ROUTER_ROWS, R), lambda i: (0, i))],
        out_shape=[jax.ShapeDtypeStruct((T, D), F32), jax.ShapeDtypeStruct((T, D), BF16),
                   jax.ShapeDtypeStruct((ROUTER_ROWS, T), F32)],
        compiler_params=pltpu.CompilerParams(dimension_semantics=("arbitrary",),
                                             vmem_limit_bytes=VMEM_LIMIT),
    )(x, o_nsa, o_sb, nsa_norm_w.reshape(1, N_Q), sb_norm_w.reshape(1, N_SBH), w_out.astype(BF16),
      ffn_norm_w.reshape(1, D), wr_hi, wr_lo, rb)


def _routing(lg, axis):
    pos_i = lax.broadcasted_iota(jnp.int32, lg.shape, axis)
    pos = pos_i.astype(F32)
    first_max = lambda v, mx: jnp.min(jnp.where(v == mx, pos, float(ROUTER_LANES)), axis=axis, keepdims=True)
    gl = jnp.where(pos_i < N_GROUPS, lg, -jnp.inf)
    gmax = jnp.max(gl, axis=axis, keepdims=True)
    grp = first_max(gl, gmax)
    g_gate = 1.0 / jnp.sum(jnp.exp(gl - gmax), axis=axis, keepdims=True)
    e_idx = pos_i - N_GROUPS
    e_grp = (e_idx >> 2).astype(F32)
    in_grp = (e_idx >= 0) & (e_idx < N_EXPERTS) & (e_grp == grp)
    el = jnp.where(in_grp, lg, -jnp.inf)
    top1 = jnp.max(el, axis=axis, keepdims=True)
    i1 = first_max(el, top1)
    el2 = jnp.where(pos == i1, -jnp.inf, el)
    top2 = jnp.max(el2, axis=axis, keepdims=True)
    i2 = first_max(el2, top2)
    e2 = jnp.exp(top2 - top1)
    w1 = 1.0 / (1.0 + e2)
    w2 = e2 / (1.0 + e2)
    weight = g_gate * (jnp.where(pos == i1, w1, 0.0) + jnp.where(pos == i2, w2, 0.0))
    routed = jnp.where(pos == i1, 1.0, 0.0) + jnp.where(pos == i2, 1.0, 0.0)
    return weight, routed


def _moe_kernel(h_ref, lgT_ref, x1_ref, before_ref, wg_ref, wu_ref, wd_ref, fw_ref, o_ref,
                acc_scr, rankT_scr, wparts_scr):
    e = pl.program_id(1)
    rows, width = h_ref.shape
    n_sub = rows // MOE_SUB

    @pl.when(e == 0)
    def _():
        acc_scr[...] = jnp.zeros(acc_scr.shape, F32)
        weight, routed = _routing(lgT_ref[...], 0)
        rank = _dot(routed.astype(BF16), before_ref[...])
        rankT_scr[...] = jnp.where(routed > 0.0, rank, -1.0)
        parts = [p.astype(F32) for p in _split3(weight)]
        pad = jnp.zeros((MOE_SLOT_ROWS - 3 * ROUTER_ROWS, rows), F32)
        wparts_scr[...] = jnp.concatenate(parts + [pad], axis=0).T.astype(BF16)

    experts = range(MOE_EXPERTS_PER_STEP)
    pos = [e * MOE_EXPERTS_PER_STEP + x + N_GROUPS for x in experts]
    rank_row = [rankT_scr[pl.ds(pos[x], 1), :] for x in experts]
    n_routed = jnp.max(functools.reduce(jnp.maximum, rank_row)).astype(jnp.int32) + 1
    slot_col = lax.broadcasted_iota(jnp.int32, (MOE_SLOT_ROWS, 1), 0)
    lane = lax.broadcasted_iota(jnp.int32, (n_sub * MOE_SUB_CAP, ROUTER_LANES), 1)
    y_pad = jnp.zeros((MOE_SLOT_ROWS - MOE_SUB_CAP, width), BF16)

    def chunk(ch, carry):
        want = jnp.where(slot_col < MOE_SUB_CAP, slot_col + ch * MOE_SUB_CAP, -2).astype(F32)
        xg, w_slot, spread = [], [], []
        for x in experts:
            got, got_w, back = [], [], []
            for s in range(n_sub):
                sub = slice(s * MOE_SUB, (s + 1) * MOE_SUB)
                onehot = jnp.where(rank_row[x][:, sub] == want, 1.0, 0.0)
                pick = onehot[0:MOE_SUB_CAP].astype(BF16)
                got.append(_dot(pick, h_ref[sub, :]))
                got_w.append(_dot(pick, wparts_scr[sub, :]))
                back.append(onehot.T.astype(BF16))
            xg.append(jnp.concatenate(got, axis=0).astype(BF16))
            mine = (lane == pos[x]) | (lane == pos[x] + ROUTER_ROWS) | (lane == pos[x] + 2 * ROUTER_ROWS)
            w_slot.append(jnp.sum(jnp.where(mine, jnp.concatenate(got_w, axis=0), 0.0), axis=1, keepdims=True))
            spread.append(back)
        a = [_dot(xg[x], wg_ref[x]) for x in experts]
        b = [_dot(xg[x], wu_ref[x]) for x in experts]
        act = [(a[x] * jax.nn.sigmoid(a[x]) * b[x]).astype(BF16) for x in experts]
        y = [(_dot(act[x], wd_ref[x]) * w_slot[x]).astype(BF16) for x in experts]
        for s in range(n_sub):
            back = jnp.concatenate([spread[x][s] for x in experts], axis=1)
            vals = jnp.concatenate([jnp.concatenate([y[x][s * MOE_SUB_CAP:(s + 1) * MOE_SUB_CAP], y_pad], axis=0)
                                    for x in experts], axis=0)
            acc_scr[s * MOE_SUB:(s + 1) * MOE_SUB, :] += _dot(back, vals)
        return carry

    lax.fori_loop(0, (n_routed + (MOE_SUB_CAP - 1)) // MOE_SUB_CAP, chunk, 0)

    @pl.when(e == N_EXPERTS // MOE_EXPERTS_PER_STEP - 1)
    def _():
        o_ref[...] = _rms(x1_ref[...] + acc_scr[...], fw_ref[...])


def _moe(h2, logitsT, x1, w_gate, w_up, w_down, final_norm_w):
    T, D = x1.shape
    R = min(MOE_ROWS, T)
    tok = jnp.arange(R)
    same_sub = (tok[:, None] >> MOE_SUB_SHIFT) == (tok[None, :] >> MOE_SUB_SHIFT)
    before = ((tok[:, None] < tok[None, :]) & same_sub).astype(BF16)
    rows = lambda n: pl.BlockSpec((R, n), lambda i, e: (i, 0))
    const = lambda a, b: pl.BlockSpec((a, b), lambda i, e: (0, 0))
    per_expert = lambda a, b: pl.BlockSpec((MOE_EXPERTS_PER_STEP, a, b), lambda i, e: (e, 0, 0))
    return pl.pallas_call(
        _moe_kernel,
        grid=(T // R, N_EXPERTS // MOE_EXPERTS_PER_STEP),
        in_specs=[rows(D), pl.BlockSpec((ROUTER_ROWS, R), lambda i, e: (0, i)), rows(D), const(R, R),
                  per_expert(D, EXPERT_FF), per_expert(D, EXPERT_FF), per_expert(EXPERT_FF, D), const(1, D)],
        out_specs=rows(D),
        out_shape=jax.ShapeDtypeStruct((T, D), F32),
        scratch_shapes=[pltpu.VMEM((R, D), F32), pltpu.VMEM((ROUTER_ROWS, R), F32),
                        pltpu.VMEM((R, ROUTER_LANES), BF16)],
        compiler_params=pltpu.CompilerParams(dimension_semantics=("arbitrary", "arbitrary"),
                                             vmem_limit_bytes=VMEM_LIMIT),
    )(h2, logitsT, x1, before, w_gate.astype(BF16), w_up.astype(BF16), w_down.astype(BF16),
      final_norm_w.reshape(1, D))


def kernel(x, positions, attn_norm_w, w_in, cmp_pe_k, cmp_pe_v, cmp_k_w1, cmp_k_w2, cmp_v_w1, cmp_v_w2,
           nsa_out_norm_w, sb_out_norm_w, w_out, ffn_norm_w, router_group_w, router_group_b,
           router_expert_w, router_expert_b, w_gate, w_up, w_down, final_norm_w):
    B, T, D = x.shape
    assert B == 1 and T % SEL_TILE == 0 and T % PROJ_ROWS == 0 and T // SEL_LEN >= SEL_TOP
    assert attn_norm_w.shape[0] == 1, "the final norm is fused into the (single) layer's MoE kernel"
    xs = x.reshape(T, D)
    pos = positions.reshape(T)
    (cmpk, cmpv, sbq, sbk, sbv, qT, ksel, kwin, vselT, vwinT, gT) = _project(xs, pos, attn_norm_w[0], w_in[0])
    kc, vcT = _compress(cmpk, cmpv, cmp_pe_k[0], cmp_pe_v[0], cmp_k_w1[0], cmp_k_w2[0], cmp_v_w1[0], cmp_v_w2[0])
    o_nsa = _nsa(qT, kc, vcT, ksel, vselT, kwin, vwinT, gT)
    o_sb = _stick_breaking(sbq, sbk, sbv)
    x1, h2, logitsT = _mix(xs, o_nsa, o_sb, nsa_out_norm_w[0], sb_out_norm_w[0], w_out[0], ffn_norm_w[0],
                           router_group_w[0], router_group_b[0], router_expert_w[0], router_expert_b[0])
    out = _moe(h2, logitsT, x1, w_gate[0], w_up[0], w_down[0], final_norm_w)
    return out.reshape(B, T, D)
```

```python
import functools

import jax
import jax.numpy as jnp
from jax import lax
from jax.experimental import pallas as pl
from jax.experimental.pallas import tpu as pltpu

HEAD_DIM = 64
NSA_HEADS = 8
NSA_KV_HEADS = 2
NSA_GROUP = NSA_HEADS // NSA_KV_HEADS
SB_HEADS = 8
ROPE_THETA = 500000.0
ROPE_DIM = HEAD_DIM // 4
ROPE_HALF = ROPE_DIM // 2
CMP_LEN = 32
CMP_STRIDE = 16
CMP_HIDDEN = 256
SEL_LEN = 64
SEL_TOP = 16
SEL_BONUS = 1.0e4
WINDOW = 512
Q_BLOCK = 128
N_GROUPS = 4
EXPERTS_PER_GROUP = 4
N_EXPERTS = N_GROUPS * EXPERTS_PER_GROUP
EXPERT_FF = 512
EPS = 1e-6
NEG = -1e30
LOWEST = -3.0e38

N_Q = NSA_HEADS * HEAD_DIM
N_KVG = NSA_KV_HEADS * HEAD_DIM
N_GATE = NSA_HEADS * 3
N_GATE_PAD = 32
N_SBH = SB_HEADS * HEAD_DIM
SCALE = HEAD_DIM ** -0.5
LOG2E = 1.4426950408889634
SEL_BLOCKS_PER_TILE = 8
BIAS_ROWS = 16
CMP_CLASSES = 8
SEL_V_ROWS = HEAD_DIM + 16

PROJ_ROWS = 512
SEL_TILE = 512
WIN_TILES = WINDOW // Q_BLOCK + 1
SB_TILE = 128
MOE_ROWS = 1024
MOE_EXPERTS_PER_STEP = 2
MOE_SLOT_ROWS = 128
MOE_SUB_SHIFT = 8
MOE_SUB = 1 << MOE_SUB_SHIFT
MOE_SUB_CAP = 48
ROUTER_LANES = 128
ROUTER_ROWS = 32
SB_FIRST_TILES = 3
SB_SKIP_LOG = -104.0

VMEM_LIMIT = 56 * 1024 * 1024

BF16 = jnp.bfloat16
F32 = jnp.float32


def _rms(x, w):
    return x * lax.rsqrt(jnp.mean(x * x, axis=-1, keepdims=True) + EPS) * w


def _dot(a, b):
    return jnp.dot(a, b, preferred_element_type=F32)


def _dot_nt(a, b):
    return lax.dot_general(a, b, (((1,), (1,)), ((), ())), preferred_element_type=F32)


def _split3(x):
    hi = x.astype(BF16)
    r1 = x - hi.astype(F32)
    mid = r1.astype(BF16)
    lo = (r1 - mid.astype(F32)).astype(BF16)
    return hi, mid, lo


def _proj_kernel(x_ref, nw_ref, wn_ref, wt_ref, pos_ref, invf_ref,
                 cmpk_ref, cmpv_ref, sbqT_ref, sbk_ref, sbvT_ref,
                 qT_ref, ksel_ref, kwin_ref, vselT_ref, vwinT_ref, gT_ref):
    h = _rms(x_ref[...], nw_ref[...]).astype(BF16)
    p1 = _dot(h, wn_ref[...])
    cmpk_ref[...] = p1[:, 0:N_KVG]
    cmpv_ref[...] = p1[:, N_KVG:2 * N_KVG]
    sbk_ref[...] = p1[:, 2 * N_KVG:2 * N_KVG + N_SBH].astype(BF16)

    p2 = _dot_nt(wt_ref[...], h)
    o = N_Q + 4 * N_KVG + N_GATE_PAD
    sbqT_ref[...] = (p2[o:o + N_SBH] * SCALE).astype(BF16)
    sbv = p2[o + N_SBH:o + 2 * N_SBH].astype(BF16)
    for j in range(PROJ_ROWS // Q_BLOCK):
        sbvT_ref[j] = sbv[:, j * Q_BLOCK:(j + 1) * Q_BLOCK]
    ang = invf_ref[...] * pos_ref[...].astype(F32)
    cos, sin = jnp.cos(ang), jnp.sin(ang)
    n_rope_heads = NSA_HEADS + 2 * NSA_KV_HEADS
    roped = []
    for hd in range(n_rope_heads):
        blk = p2[hd * HEAD_DIM:(hd + 1) * HEAD_DIM]
        x1, x2 = blk[0:ROPE_HALF], blk[ROPE_HALF:ROPE_DIM]
        roped.append(jnp.concatenate(
            [x1 * cos - x2 * sin, x2 * cos + x1 * sin, blk[ROPE_DIM:]], axis=0))
    qT_ref[...] = (jnp.concatenate(roped[:NSA_HEADS], axis=0) * (SCALE * LOG2E)).astype(BF16)
    kT = jnp.concatenate(roped[NSA_HEADS:], axis=0)
    kn = kT.T.astype(BF16)
    r_blk = lax.broadcasted_iota(jnp.int32, (PROJ_ROWS, HEAD_DIM), 0) >> 6
    c_idx = lax.broadcasted_iota(jnp.int32, (PROJ_ROWS, HEAD_DIM), 1)
    onehot = jnp.where(r_blk == c_idx, 1.0, 0.0).astype(BF16)
    for gk in range(NSA_KV_HEADS):
        ksel_ref[gk] = jnp.concatenate([kn[:, gk * HEAD_DIM:(gk + 1) * HEAD_DIM], onehot], axis=1)
    kwin_ref[...] = kn[:, N_KVG:2 * N_KVG]
    o = N_Q + 2 * N_KVG
    for gk in range(NSA_KV_HEADS):
        vselT_ref[0, gk] = jnp.concatenate(
            [p2[o + gk * HEAD_DIM:o + (gk + 1) * HEAD_DIM],
             jnp.where(lax.broadcasted_iota(jnp.int32, (SEL_V_ROWS - HEAD_DIM, PROJ_ROWS), 0) == 0, 1.0, 0.0)],
            axis=0).astype(BF16)
    vw = p2[o + N_KVG:o + 2 * N_KVG].astype(BF16)
    for j in range(PROJ_ROWS // Q_BLOCK):
        vwinT_ref[j] = vw[:, j * Q_BLOCK:(j + 1) * Q_BLOCK]
    o = o + 2 * N_KVG
    gT_ref[...] = jax.nn.sigmoid(p2[o:o + N_GATE_PAD])


def _project(x, positions, attn_norm_w, w_in):
    T, D = x.shape
    R = PROJ_ROWS
    o1, o2, o3 = N_Q, N_Q + 6 * N_KVG, N_Q + 6 * N_KVG + N_GATE
    kv = lambda i: w_in[:, o1 + i * N_KVG:o1 + (i + 1) * N_KVG]
    sb = lambda i: w_in[:, o3 + i * N_SBH:o3 + (i + 1) * N_SBH]
    w_nat = jnp.concatenate([kv(0), kv(1), sb(1)], axis=1).astype(BF16)
    w_t = jnp.concatenate(
        [w_in[:, :o1], kv(2), kv(4), kv(3), kv(5), w_in[:, o2:o3],
         jnp.zeros((D, N_GATE_PAD - N_GATE), w_in.dtype), sb(0), sb(2)], axis=1).T.astype(BF16)
    inv_freq = ROPE_THETA ** (-jnp.arange(0, ROPE_DIM, 2, dtype=F32) / ROPE_DIM)
    n_nat, n_t = w_nat.shape[1], w_t.shape[0]
    full = lambda shape: pl.BlockSpec(shape, lambda i: (0,) * len(shape))
    rows = lambda n: pl.BlockSpec((R, n), lambda i: (i, 0))
    cols = lambda n: pl.BlockSpec((n, R), lambda i: (0, i))
    return pl.pallas_call(
        _proj_kernel,
        grid=(T // R,),
        in_specs=[rows(D), full((1, D)), full((D, n_nat)), full((n_t, D)), cols(1), full((ROPE_HALF, 1))],
        out_specs=[rows(N_KVG), rows(N_KVG), cols(N_SBH), rows(N_SBH),
                   pl.BlockSpec((R // Q_BLOCK, N_SBH, Q_BLOCK), lambda i: (i, 0, 0)),
                   cols(N_Q), pl.BlockSpec((NSA_KV_HEADS, R, N_KVG), lambda i: (0, i, 0)), rows(N_KVG),
                   pl.BlockSpec((1, NSA_KV_HEADS, SEL_V_ROWS, R), lambda i: (i, 0, 0, 0)),
                   pl.BlockSpec((R // Q_BLOCK, N_KVG, Q_BLOCK), lambda i: (i, 0, 0)),
                   cols(N_GATE_PAD)],
        out_shape=[jax.ShapeDtypeStruct((T, N_KVG), F32), jax.ShapeDtypeStruct((T, N_KVG), F32),
                   jax.ShapeDtypeStruct((N_SBH, T), BF16), jax.ShapeDtypeStruct((T, N_SBH), BF16),
                   jax.ShapeDtypeStruct((T // Q_BLOCK, N_SBH, Q_BLOCK), BF16),
                   jax.ShapeDtypeStruct((N_Q, T), BF16),
                   jax.ShapeDtypeStruct((NSA_KV_HEADS, T, N_KVG), BF16), jax.ShapeDtypeStruct((T, N_KVG), BF16),
                   jax.ShapeDtypeStruct((T // R, NSA_KV_HEADS, SEL_V_ROWS, R), BF16),
                   jax.ShapeDtypeStruct((T // Q_BLOCK, N_KVG, Q_BLOCK), BF16),
                   jax.ShapeDtypeStruct((N_GATE_PAD, T), F32)],
        compiler_params=pltpu.CompilerParams(dimension_semantics=("arbitrary",),
                                             vmem_limit_bytes=VMEM_LIMIT),
    )(x, attn_norm_w.reshape(1, D), w_nat, w_t, positions.reshape(1, T), inv_freq.reshape(ROPE_HALF, 1))


def _compress_kernel(x_ref, pea_ref, peb_ref, wa_ref, wb_ref, w2_ref, nat_ref, tr_ref):
    x = x_ref[0]
    nc = x.shape[0]
    ha = _dot((x + pea_ref[0]).astype(BF16), wa_ref[0])
    hb = _dot((x + peb_ref[0]).astype(BF16), wb_ref[0])
    hid = ha + pltpu.roll(hb, nc - 1, 0)
    act = (hid * jax.nn.sigmoid(hid)).astype(BF16)
    out = _dot(act, w2_ref[0])
    nat_ref[0] = out.astype(BF16)
    tr_ref[0] = out.T.astype(BF16)


def _compress(cmpk, cmpv, pe_k, pe_v, k_w1, k_w2, v_w1, v_w2):
    T = cmpk.shape[0]
    nc = T // CMP_STRIDE
    half = CMP_LEN // 2
    G = NSA_KV_HEADS
    width = half * N_KVG
    x = jnp.stack([cmpk.reshape(nc, width), cmpv.reshape(nc, width)])
    eye = jnp.eye(G, dtype=F32)

    def pe_rows(pe):
        return jnp.broadcast_to(pe[:, None, :], (half, G, HEAD_DIM)).reshape(1, width)

    def w1_block(w1):
        w = w1.reshape(half, HEAD_DIM, CMP_HIDDEN)
        return jnp.einsum('ldj,gh->lgdhj', w, eye).reshape(width, G * CMP_HIDDEN).astype(BF16)

    def w2_block(w2):
        return jnp.einsum('jd,gh->gjhd', w2, eye).reshape(G * CMP_HIDDEN, N_KVG).astype(BF16)

    hw = half * HEAD_DIM
    pea = jnp.stack([pe_rows(pe_k[:half]), pe_rows(pe_v[:half])])
    peb = jnp.stack([pe_rows(pe_k[half:]), pe_rows(pe_v[half:])])
    wa = jnp.stack([w1_block(k_w1[:hw]), w1_block(v_w1[:hw])])
    wb = jnp.stack([w1_block(k_w1[hw:]), w1_block(v_w1[hw:])])
    w2 = jnp.stack([w2_block(k_w2), w2_block(v_w2)])
    blk = lambda a, b: pl.BlockSpec((1, a, b), lambda i: (i, 0, 0))
    nat, tr = pl.pallas_call(
        _compress_kernel,
        grid=(2,),
        in_specs=[blk(nc, width), blk(1, width), blk(1, width), blk(width, G * CMP_HIDDEN),
                  blk(width, G * CMP_HIDDEN), blk(G * CMP_HIDDEN, N_KVG)],
        out_specs=[blk(nc, N_KVG), blk(N_KVG, nc)],
        out_shape=[jax.ShapeDtypeStruct((2, nc, N_KVG), BF16), jax.ShapeDtypeStruct((2, N_KVG, nc), BF16)],
        compiler_params=pltpu.CompilerParams(dimension_semantics=("arbitrary",),
                                             vmem_limit_bytes=VMEM_LIMIT),
    )(x, pea, peb, wa, wb, w2)
    return nat[0], tr[1]


def _nsa_kernel(q_ref, kc_ref, vcT_ref, ovT_ref, ksel_ref, vselT_ref, *rest):
    kwin_refs = rest[0:WIN_TILES]
    vwin_refs = rest[WIN_TILES:2 * WIN_TILES]
    gT_ref, o_ref = rest[2 * WIN_TILES:2 * WIN_TILES + 2]
    scratch = rest[2 * WIN_TILES + 2:]
    G = NSA_KV_HEADS
    per = len(scratch) // G
    bias_scr, s0_scr, s1_scr, t0_scr, t1_scr, m_scr, acc_scr, oc_scr = (
        [scratch[g * per + n] for g in range(G)] for n in range(per))
    groups = range(G)
    i = pl.program_id(0)
    R, Q = NSA_GROUP, Q_BLOCK
    L = R * Q
    nsel = bias_scr[0].shape[0]

    zero = jnp.zeros((HEAD_DIM, L), BF16)
    qg, qz = [], []
    for g in groups:
        q = jnp.concatenate([q_ref[(g * R + r) * HEAD_DIM:(g * R + r + 1) * HEAD_DIM, :] for r in range(R)], axis=1)
        qg.append(q)
        qz.append(jnp.concatenate([q if gg == g else zero for gg in groups], axis=0))

    def own_rows(x, g):
        return x[g * HEAD_DIM:(g + 1) * HEAD_DIM]

    t_lane = i * Q + (lax.broadcasted_iota(jnp.int32, (1, L), 1) & (Q - 1))

    nc = kc_ref.shape[0]
    q_pos = i * Q + lax.broadcasted_iota(jnp.int32, (1, Q), 1)
    bq = (q_pos >> 6).astype(F32)

    def compress_and_select(n_eff, first_class):
        m_eff = n_eff // (SEL_LEN // CMP_STRIDE)
        cmp_end = lax.broadcasted_iota(jnp.int32, (n_eff, 1), 0) * CMP_STRIDE + (CMP_LEN - 1)
        visible = cmp_end <= t_lane
        sc = [_dot(kc_ref[0:n_eff, :], qz[g]) for g in groups]
        pc = []
        for g in groups:
            s = jnp.where(visible, sc[g], NEG)
            mxc = jnp.max(s, axis=0, keepdims=True)
            mxc = jnp.where(mxc < 0.5 * NEG, 0.0, mxc)
            ec = jnp.exp2(s - mxc)
            pc.append(ec * (1.0 / jnp.maximum(jnp.sum(ec, axis=0, keepdims=True), 1e-30)))
        for g in groups:
            oc_scr[g][...] = own_rows(_dot(vcT_ref[:, 0:n_eff], pc[g].astype(BF16)), g)
        ov = ovT_ref[0:m_eff, 0:n_eff]
        imp = []
        for g in groups:
            psum = pc[g][:, 0:Q]
            for r in range(1, R):
                psum = psum + pc[g][:, r * Q:(r + 1) * Q]
            imp.append(sum(_dot(ov, part) for part in _split3(psum)))
        m_idx = lax.broadcasted_iota(jnp.int32, (m_eff, Q), 0).astype(F32)
        allowed = m_idx <= bq
        forced = (m_idx == 0.0) | (m_idx == bq) | (m_idx == bq - 1.0)
        if first_class:
            score = [jnp.where(allowed, imp[g] + jnp.where(forced, SEL_BONUS, 0.0), NEG) for g in groups]
            n_pick = min(SEL_TOP, m_eff)
        else:
            free = allowed & jnp.logical_not(forced)
            score = [jnp.where(free, imp[g], NEG) for g in groups]
            n_pick = SEL_TOP - 3
        for _ in range(n_pick):
            for g in groups:
                best = jnp.max(score[g], axis=0, keepdims=True)
                first = jnp.min(jnp.where(score[g] == best, m_idx, float(m_eff)), axis=0, keepdims=True)
                score[g] = jnp.where(m_idx == first, LOWEST, score[g])
        for g in groups:
            picked = score[g] < 0.5 * LOWEST
            if not first_class:
                picked = picked | forced
            bias = jnp.where(allowed & picked, 0.0, NEG)
            bias_scr[g][0:m_eff, :] = jnp.concatenate([bias] * R, axis=1)
            if m_eff < nsel:
                bias_scr[g][m_eff:nsel, :] = jnp.full((nsel - m_eff, L), NEG, F32)

    sizes = sorted({max(Q, (nc * k // CMP_CLASSES) // Q * Q) for k in range(1, CMP_CLASSES + 1)})
    lo = 0
    for n_eff in sizes:
        hi = n_eff // (Q // CMP_STRIDE)
        pl.when((i >= lo) & (i < hi))(functools.partial(compress_and_select, n_eff, lo == 0))
        lo = hi

    for g in groups:
        m_scr[g][...] = jnp.full(m_scr[g].shape, NEG, F32)
        acc_scr[g][...] = jnp.zeros(acc_scr[g].shape, F32)
    rhs_pad = jnp.zeros((N_KVG - HEAD_DIM - BIAS_ROWS, L), BF16)
    bias_pad = jnp.zeros((BIAS_ROWS - SEL_BLOCKS_PER_TILE, L), F32)
    buf0, buf1 = (s0_scr, t0_scr), (s1_scr, t1_scr)

    def scores(kt, dst):
        for g in groups:
            brows = bias_scr[g][pl.ds(pl.multiple_of(kt * SEL_BLOCKS_PER_TILE, SEL_BLOCKS_PER_TILE),
                                      SEL_BLOCKS_PER_TILE), :]
            rhs = jnp.concatenate([qg[g], jnp.concatenate([brows, bias_pad], axis=0).astype(BF16), rhs_pad], axis=0)
            s = _dot(ksel_ref[g, kt], rhs)
            dst[0][g][...] = s
            dst[1][g][...] = jnp.max(s, axis=0, keepdims=True)

    def absorb(kt, src):
        for g in groups:
            m_old = m_scr[g][...]
            m_new = jnp.maximum(m_old, src[1][g][...])
            alpha = jnp.exp2(m_old - m_new)
            p = jnp.exp2(src[0][g][...] - m_new)
            acc_scr[g][...] = alpha * acc_scr[g][...] + _dot(vselT_ref[kt, g], p.astype(BF16))
            m_scr[g][...] = m_new

    def step(kt, src, dst):
        scores(kt + 1, dst)
        absorb(kt, src)

    def two_steps(kt):
        step(kt, buf0, buf1)
        step(kt + 1, buf1, buf0)

    k_row = lax.broadcasted_iota(jnp.int32, (Q, 1), 0)
    q_lane = lax.broadcasted_iota(jnp.int32, (1, L), 1) & (Q - 1)
    n_full = (i * Q) // SEL_TILE

    def last_tile(src):
        r0 = pl.multiple_of(i * Q - n_full * SEL_TILE, Q)
        for g in groups:
            src[0][g][pl.ds(r0, Q), :] = jnp.where(k_row <= q_lane, src[0][g][pl.ds(r0, Q), :], NEG)
            src[1][g][...] = jnp.max(src[0][g][...], axis=0, keepdims=True)
        absorb(n_full, src)

    odd = n_full & 1

    @pl.when(odd == 0)
    def _():
        scores(0, buf0)

    @pl.when(odd == 1)
    def _():
        scores(0, buf1)
        step(0, buf1, buf0)

    n_pairs = n_full >> 1

    def four_steps(j, carry):
        two_steps(odd + 4 * j)
        two_steps(odd + 4 * j + 2)
        return carry

    lax.fori_loop(0, n_pairs >> 1, four_steps, 0)

    @pl.when((n_pairs & 1) == 1)
    def _():
        two_steps(odd + 4 * (n_pairs >> 1))

    last_tile(buf0)

    sw = [[] for _ in groups]
    for w in range(WIN_TILES):
        j = i - (WIN_TILES - 1) + w
        for g in groups:
            s = _dot(kwin_refs[w][0], qz[g])
            if w == 0:
                s = jnp.where(k_row > q_lane, s, NEG)
            if w == WIN_TILES - 1:
                s = jnp.where(k_row <= q_lane, s, NEG)
            else:
                s = jnp.where(j >= 0, s, NEG)
            sw[g].append(s)
    ew, denw = [], []
    for g in groups:
        mxw = functools.reduce(jnp.maximum, [jnp.max(s, axis=0, keepdims=True) for s in sw[g]])
        ew.append([jnp.exp2(s - mxw) for s in sw[g]])
        denw.append(jnp.maximum(sum(jnp.sum(e, axis=0, keepdims=True) for e in ew[g]), 1e-30))
    owT = []
    for g in groups:
        prod = sum(_dot(vwin_refs[w][0], ew[g][w].astype(BF16)) for w in range(WIN_TILES))
        owT.append(own_rows(prod, g) * (1.0 / denw[g]))

    def gate_row(g, j):
        return jnp.concatenate([gT_ref[(g * R + r) * 3 + j:(g * R + r) * 3 + j + 1, :] for r in range(R)], axis=1)

    o_rows = []
    for g in groups:
        osT = acc_scr[g][0:HEAD_DIM, :] * (1.0 / acc_scr[g][HEAD_DIM:HEAD_DIM + 1, :])
        oT = gate_row(g, 0) * oc_scr[g][...] + gate_row(g, 1) * osT + gate_row(g, 2) * owT[g]
        o_rows += [oT[:, r * Q:(r + 1) * Q] for r in range(R)]
    o_ref[...] = jnp.concatenate(o_rows, axis=0).T


def _nsa(qT, kc, vcT, ksel, vselT, kwin, vwinT, gT):
    T = qT.shape[1]
    nb = T // Q_BLOCK
    nc = kc.shape[0]
    nsel = T // SEL_LEN
    ntile = T // SEL_TILE
    n = jnp.arange(nc)[None, :] * CMP_STRIDE
    m = jnp.arange(nsel)[:, None] * SEL_LEN
    ovT = ((n < m + SEL_LEN) & (n + CMP_LEN > m)).astype(BF16)
    ksel4 = ksel.reshape(NSA_KV_HEADS, ntile, SEL_TILE, N_KVG)
    kwin3 = kwin.reshape(nb, Q_BLOCK, N_KVG)
    L = NSA_GROUP * Q_BLOCK
    const = lambda shape: pl.BlockSpec(shape, lambda i: (0,) * len(shape))
    win_tile = lambda i, w: (jnp.maximum(i - (WIN_TILES - 1) + w, 0), 0, 0)
    kwin_specs = [pl.BlockSpec((1, Q_BLOCK, N_KVG), functools.partial(win_tile, w=w)) for w in range(WIN_TILES)]
    vwin_specs = [pl.BlockSpec((1, N_KVG, Q_BLOCK), functools.partial(win_tile, w=w)) for w in range(WIN_TILES)]
    group_scratch = [pltpu.VMEM((nsel, L), F32), pltpu.VMEM((SEL_TILE, L), F32), pltpu.VMEM((SEL_TILE, L), F32),
                     pltpu.VMEM((1, L), F32), pltpu.VMEM((1, L), F32),
                     pltpu.VMEM((1, L), F32), pltpu.VMEM((SEL_V_ROWS, L), F32), pltpu.VMEM((HEAD_DIM, L), F32)]
    return pl.pallas_call(
        _nsa_kernel,
        grid=(nb,),
        in_specs=[pl.BlockSpec((N_Q, Q_BLOCK), lambda i: (0, i)),
                  const((nc, N_KVG)), const((N_KVG, nc)), const((nsel, nc)),
                  const((NSA_KV_HEADS, ntile, SEL_TILE, N_KVG)), const((ntile, NSA_KV_HEADS, SEL_V_ROWS, SEL_TILE))]
                 + kwin_specs + vwin_specs
                 + [pl.BlockSpec((N_GATE_PAD, Q_BLOCK), lambda i: (0, i))],
        out_specs=pl.BlockSpec((Q_BLOCK, N_Q), lambda i: (i, 0)),
        out_shape=jax.ShapeDtypeStruct((T, N_Q), F32),
        scratch_shapes=group_scratch * NSA_KV_HEADS,
        compiler_params=pltpu.CompilerParams(dimension_semantics=("arbitrary",),
                                             vmem_limit_bytes=VMEM_LIMIT),
    )(qT, kc, vcT, ovT, ksel4, vselT, *([kwin3] * WIN_TILES), *([vwinT] * WIN_TILES), gT)


def _sb_kernel(qT_ref, k_ref, vT_ref, o_ref, *scr):
    i = pl.program_id(0)
    Q, W = Q_BLOCK, 2 * HEAD_DIM
    pairs = SB_HEADS // 2
    blk_row = lax.broadcasted_iota(jnp.int32, (W, 2 * Q), 0) < HEAD_DIM
    blk_lane = lax.broadcasted_iota(jnp.int32, (W, 2 * Q), 1) < Q
    zero = jnp.zeros((), BF16)
    q_pairs = []
    for pr in range(pairs):
        x = qT_ref[pr * W:(pr + 1) * W, :]
        q_pairs.append(jnp.where(blk_row == blk_lane, jnp.concatenate([x, x], axis=1), zero))
    k_row = lax.broadcasted_iota(jnp.int32, (SB_TILE, 1), 0)
    q_lane = lax.broadcasted_iota(jnp.int32, (1, SB_HEADS * Q), 1) & (Q - 1)
    rr = lax.broadcasted_iota(jnp.int32, (SB_TILE, SB_TILE), 0)
    cc = lax.broadcasted_iota(jnp.int32, (SB_TILE, SB_TILE), 1)
    from_here = (cc >= rr).astype(BF16)
    acc_scrs, c_scrs = scr[0:pairs], scr[pairs:pairs + 2]
    for buf in scr:
        buf[...] = jnp.zeros(buf.shape, F32)

    halves = ((0, 1), (2, 3))
    causal = k_row < q_lane[:, 0:4 * Q]

    def walk(tiles):
        rows = [pl.ds(pl.multiple_of(kt * SB_TILE, SB_TILE), SB_TILE) for kt, _, _ in tiles]
        chains = [(t, hf) for t in range(len(tiles)) for hf in range(2)]
        z, d, suffix = {}, {}, {}
        for t, hf in chains:
            z[t, hf] = jnp.concatenate(
                [_dot(k_ref[rows[t], pr * W:(pr + 1) * W], q_pairs[pr]) for pr in halves[hf]], axis=1)
        for t, hf in chains:
            x = jnp.maximum(z[t, hf], 0.0) + jnp.log(1.0 + jnp.exp(-jnp.abs(z[t, hf])))
            if tiles[t][1]:
                x = jnp.where(causal, x, 0.0)
            if tiles[t][2] is not None:
                x = jnp.where(tiles[t][2], x, 0.0)
            d[t, hf] = x
            hi = x.astype(BF16)
            lo = (x - hi.astype(F32)).astype(BF16)
            suffix[t, hf] = _dot(from_here, hi) + _dot(from_here, lo)
        least = None
        for hf in range(2):
            c = c_scrs[hf][...]
            for t, (kt, own, exists) in enumerate(tiles):
                a = jnp.exp(z[t, hf] - suffix[t, hf] - c)
                if own:
                    a = jnp.where(causal, a, 0.0)
                if exists is not None:
                    a = jnp.where(exists, a, 0.0)
                a = a.astype(BF16)
                for n, pr in enumerate(halves[hf]):
                    acc_scrs[pr][...] = acc_scrs[pr][...] + _dot(vT_ref[kt, pr * W:(pr + 1) * W, :],
                                                                 a[:, n * 2 * Q:(n + 1) * 2 * Q])
                c = c + jnp.sum(d[t, hf], axis=0, keepdims=True)
            c_scrs[hf][...] = c
            least = c if least is None else jnp.minimum(least, c)
        return -jnp.min(least)

    first = [(i, True, None)] + [(jnp.maximum(i - n, 0), False, i - n >= 0) for n in range(1, SB_FIRST_TILES)]
    worst0 = walk(first)

    def cond(carry):
        kt, worst = carry
        return (kt >= 0) & (worst >= SB_SKIP_LOG)

    def body(carry):
        kt, _ = carry
        return kt - 1, walk([(kt, False, None)])

    lax.while_loop(cond, body, (i - SB_FIRST_TILES, worst0))
    for pr in range(pairs):
        acc = acc_scrs[pr][...]
        o_ref[:, pr * W:(pr + 1) * W] = jnp.where(blk_row[:, 0:Q], acc[:, 0:Q], acc[:, Q:2 * Q]).T


def _stick_breaking(sbqT, sbk, sbvT):
    T = sbk.shape[0]
    nb = T // Q_BLOCK
    W = 2 * HEAD_DIM
    return pl.pallas_call(
        _sb_kernel,
        grid=(nb,),
        in_specs=[pl.BlockSpec((N_SBH, Q_BLOCK), lambda i: (0, i)),
                  pl.BlockSpec((T, N_SBH), lambda i: (0, 0)),
                  pl.BlockSpec((nb, N_SBH, Q_BLOCK), lambda i: (0, 0, 0))],
        out_specs=pl.BlockSpec((Q_BLOCK, N_SBH), lambda i: (i, 0)),
        out_shape=jax.ShapeDtypeStruct((T, N_SBH), F32),
        scratch_shapes=[pltpu.VMEM((W, 2 * Q_BLOCK), F32)] * (SB_HEADS // 2) + [pltpu.VMEM((1, 4 * Q_BLOCK), F32)] * 2,
        compiler_params=pltpu.CompilerParams(dimension_semantics=("arbitrary",),
                                             vmem_limit_bytes=VMEM_LIMIT),
    )(sbqT, sbk, sbvT)


def _mix_kernel(x_ref, on_ref, os_ref, nwn_ref, nws_ref, wo_ref, fw_ref, rhi_ref, rlo_ref, rb_ref,
                x1_ref, h2_ref, lgT_ref):
    n1 = _rms(on_ref[...], nwn_ref[...]).astype(BF16)
    n2 = _rms(os_ref[...], nws_ref[...]).astype(BF16)
    x1 = x_ref[...] + _dot(n1, wo_ref[0:N_Q]) + _dot(n2, wo_ref[N_Q:N_Q + N_SBH])
    x1_ref[...] = x1
    h2 = _rms(x1, fw_ref[...])
    hi = h2.astype(BF16)
    lo = (h2 - hi.astype(F32)).astype(BF16)
    h2_ref[...] = hi
    lg = _dot(hi, rhi_ref[...]) + _dot(hi, rlo_ref[...]) + _dot(lo, rhi_ref[...]) + rb_ref[...]
    lgT_ref[...] = lg.T[0:ROUTER_ROWS]


def _mix(x, o_nsa, o_sb, nsa_norm_w, sb_norm_w, w_out, ffn_norm_w, rg_w, rg_b, re_w, re_b):
    T, D = x.shape
    R = PROJ_ROWS
    pad = ROUTER_LANES - N_GROUPS - N_EXPERTS
    wr = jnp.concatenate([rg_w, re_w, jnp.zeros((D, pad), F32)], axis=1)
    wr_hi = wr.astype(BF16)
    wr_lo = (wr - wr_hi.astype(F32)).astype(BF16)
    rb = jnp.concatenate([rg_b, re_b, jnp.zeros((pad,), F32)]).reshape(1, ROUTER_LANES)
    full = lambda shape: pl.BlockSpec(shape, lambda i: (0,) * len(shape))
    rows = lambda n: pl.BlockSpec((R, n), lambda i: (i, 0))
    return pl.pallas_call(
        _mix_kernel,
        grid=(T // R,),
        in_specs=[rows(D), rows(N_Q), rows(N_SBH), full((1, N_Q)), full((1, N_SBH)), full((N_Q + N_SBH, D)),
                  full((1, D)), full((D, ROUTER_LANES)), full((D, ROUTER_LANES)), full((1, ROUTER_LANES))],
        out_specs=[rows(D), rows(D), pl.BlockSpec((ROUTER_ROWS, R), lambda i: (0, i))],
        out_shape=[jax.ShapeDtypeStruct((T, D), F32), jax.ShapeDtypeStruct((T, D), BF16),
                   jax.ShapeDtypeStruct((ROUTER_ROWS, T), F32)],
        compiler_params=pltpu.CompilerParams(dimension_semantics=("arbitrary",),
                                             vmem_limit_bytes=VMEM_LIMIT),
    )(x, o_nsa, o_sb, nsa_norm_w.reshape(1, N_Q), sb_norm_w.reshape(1, N_SBH), w_out.astype(BF16),
      ffn_norm_w.reshape(1, D), wr_hi, wr_lo, rb)


def _routing(lg, axis):
    pos_i = lax.broadcasted_iota(jnp.int32, lg.shape, axis)
    pos = pos_i.astype(F32)
    first_max = lambda v, mx: jnp.min(jnp.where(v == mx, pos, float(ROUTER_LANES)), axis=axis, keepdims=True)
    gl = jnp.where(pos_i < N_GROUPS, lg, -jnp.inf)
    gmax = jnp.max(gl, axis=axis, keepdims=True)
    grp = first_max(gl, gmax)
    g_gate = 1.0 / jnp.sum(jnp.exp(gl - gmax), axis=axis, keepdims=True)
    e_idx = pos_i - N_GROUPS
    e_grp = (e_idx >> 2).astype(F32)
    in_grp = (e_idx >= 0) & (e_idx < N_EXPERTS) & (e_grp == grp)
    el = jnp.where(in_grp, lg, -jnp.inf)
    top1 = jnp.max(el, axis=axis, keepdims=True)
    i1 = first_max(el, top1)
    el2 = jnp.where(pos == i1, -jnp.inf, el)
    top2 = jnp.max(el2, axis=axis, keepdims=True)
    i2 = first_max(el2, top2)
    e2 = jnp.exp(top2 - top1)
    w1 = 1.0 / (1.0 + e2)
    w2 = e2 / (1.0 + e2)
    weight = g_gate * (jnp.where(pos == i1, w1, 0.0) + jnp.where(pos == i2, w2, 0.0))
    routed = jnp.where(pos == i1, 1.0, 0.0) + jnp.where(pos == i2, 1.0, 0.0)
    return weight, routed


def _moe_kernel(h_ref, lgT_ref, x1_ref, before_ref, wg_ref, wu_ref, wd_ref, fw_ref, o_ref,
                acc_scr, rankT_scr, wparts_scr):
    e = pl.program_id(1)
    rows, width = h_ref.shape
    n_sub = rows // MOE_SUB

    @pl.when(e == 0)
    def _():
        acc_scr[...] = jnp.zeros(acc_scr.shape, F32)
        weight, routed = _routing(lgT_ref[...], 0)
        rank = _dot(routed.astype(BF16), before_ref[...])
        rankT_scr[...] = jnp.where(routed > 0.0, rank, -1.0)
        parts = [p.astype(F32) for p in _split3(weight)]
        pad = jnp.zeros((MOE_SLOT_ROWS - 3 * ROUTER_ROWS, rows), F32)
        wparts_scr[...] = jnp.concatenate(parts + [pad], axis=0).T.astype(BF16)

    experts = range(MOE_EXPERTS_PER_STEP)
    pos = [e * MOE_EXPERTS_PER_STEP + x + N_GROUPS for x in experts]
    rank_row = [rankT_scr[pl.ds(pos[x], 1), :] for x in experts]
    n_routed = jnp.max(functools.reduce(jnp.maximum, rank_row)).astype(jnp.int32) + 1
    slot_col = lax.broadcasted_iota(jnp.int32, (MOE_SLOT_ROWS, 1), 0)
    lane = lax.broadcasted_iota(jnp.int32, (n_sub * MOE_SUB_CAP, ROUTER_LANES), 1)
    y_pad = jnp.zeros((MOE_SLOT_ROWS - MOE_SUB_CAP, width), BF16)

    def chunk(ch, carry):
        want = jnp.where(slot_col < MOE_SUB_CAP, slot_col + ch * MOE_SUB_CAP, -2).astype(F32)
        xg, w_slot = [], []
        got = [[] for _ in experts]
        got_w = [[] for _ in experts]
        spread = [[] for _ in experts]
        for s in range(n_sub):
            sub = slice(s * MOE_SUB, (s + 1) * MOE_SUB)
            onehot = [jnp.where(rank_row[x][:, sub] == want, 1.0, 0.0) for x in experts]
            picks = jnp.concatenate([onehot[x][0:MOE_SUB_CAP] for x in experts], axis=0).astype(BF16)
            rows_h = _dot(picks, h_ref[sub, :])
            rows_w = _dot(picks, wparts_scr[sub, :])
            for x in experts:
                got[x].append(rows_h[x * MOE_SUB_CAP:(x + 1) * MOE_SUB_CAP])
                got_w[x].append(rows_w[x * MOE_SUB_CAP:(x + 1) * MOE_SUB_CAP])
                spread[x].append(onehot[x].T.astype(BF16))
        for x in experts:
            xg.append(jnp.concatenate(got[x], axis=0).astype(BF16))
            mine = (lane == pos[x]) | (lane == pos[x] + ROUTER_ROWS) | (lane == pos[x] + 2 * ROUTER_ROWS)
            w_slot.append(jnp.sum(jnp.where(mine, jnp.concatenate(got_w[x], axis=0), 0.0), axis=1, keepdims=True))
        a = [_dot(xg[x], wg_ref[x]) for x in experts]
        b = [_dot(xg[x], wu_ref[x]) for x in experts]
        act = [(a[x] * jax.nn.sigmoid(a[x]) * b[x]).astype(BF16) for x in experts]
        y = [(_dot(act[x], wd_ref[x]) * w_slot[x]).astype(BF16) for x in experts]
        for s in range(n_sub):
            back = jnp.concatenate([spread[x][s] for x in experts], axis=1)
            vals = jnp.concatenate([jnp.concatenate([y[x][s * MOE_SUB_CAP:(s + 1) * MOE_SUB_CAP], y_pad], axis=0)
                                    for x in experts], axis=0)
            acc_scr[s * MOE_SUB:(s + 1) * MOE_SUB, :] += _dot(back, vals)
        return carry

    lax.fori_loop(0, (n_routed + (MOE_SUB_CAP - 1)) // MOE_SUB_CAP, chunk, 0)

    @pl.when(e == N_EXPERTS // MOE_EXPERTS_PER_STEP - 1)
    def _():
        o_ref[...] = _rms(x1_ref[...] + acc_scr[...], fw_ref[...])


def _moe(h2, logitsT, x1, w_gate, w_up, w_down, final_norm_w):
    T, D = x1.shape
    R = min(MOE_ROWS, T)
    tok = jnp.arange(R)
    same_sub = (tok[:, None] >> MOE_SUB_SHIFT) == (tok[None, :] >> MOE_SUB_SHIFT)
    before = ((tok[:, None] < tok[None, :]) & same_sub).astype(BF16)
    rows = lambda n: pl.BlockSpec((R, n), lambda i, e: (i, 0))
    const = lambda a, b: pl.BlockSpec((a, b), lambda i, e: (0, 0))
    per_expert = lambda a, b: pl.BlockSpec((MOE_EXPERTS_PER_STEP, a, b), lambda i, e: (e, 0, 0))
    return pl.pallas_call(
        _moe_kernel,
        grid=(T // R, N_EXPERTS // MOE_EXPERTS_PER_STEP),
        in_specs=[rows(D), pl.BlockSpec((ROUTER_ROWS, R), lambda i, e: (0, i)), rows(D), const(R, R),
                  per_expert(D, EXPERT_FF), per_expert(D, EXPERT_FF), per_expert(EXPERT_FF, D), const(1, D)],
        out_specs=rows(D),
        out_shape=jax.ShapeDtypeStruct((T, D), F32),
        scratch_shapes=[pltpu.VMEM((R, D), F32), pltpu.VMEM((ROUTER_ROWS, R), F32),
                        pltpu.VMEM((R, ROUTER_LANES), BF16)],
        compiler_params=pltpu.CompilerParams(dimension_semantics=("arbitrary", "arbitrary"),
                                             vmem_limit_bytes=VMEM_LIMIT),
    )(h2, logitsT, x1, before, w_gate.astype(BF16), w_up.astype(BF16), w_down.astype(BF16),
      final_norm_w.reshape(1, D))


def kernel(x, positions, attn_norm_w, w_in, cmp_pe_k, cmp_pe_v, cmp_k_w1, cmp_k_w2, cmp_v_w1, cmp_v_w2,
           nsa_out_norm_w, sb_out_norm_w, w_out, ffn_norm_w, router_group_w, router_group_b,
           router_expert_w, router_expert_b, w_gate, w_up, w_down, final_norm_w):
    B, T, D = x.shape
    assert B == 1 and T % SEL_TILE == 0 and T % PROJ_ROWS == 0 and T // SEL_LEN >= SEL_TOP
    assert attn_norm_w.shape[0] == 1, "the final norm is fused into the (single) layer's MoE kernel"
    xs = x.reshape(T, D)
    pos = positions.reshape(T)
    (cmpk, cmpv, sbq, sbk, sbv, qT, ksel, kwin, vselT, vwinT, gT) = _project(xs, pos, attn_norm_w[0], w_in[0])
    kc, vcT = _compress(cmpk, cmpv, cmp_pe_k[0], cmp_pe_v[0], cmp_k_w1[0], cmp_k_w2[0], cmp_v_w1[0], cmp_v_w2[0])
    o_nsa = _nsa(qT, kc, vcT, ksel, vselT, kwin, vwinT, gT)
    o_sb = _stick_breaking(sbq, sbk, sbv)
    x1, h2, logitsT = _mix(xs, o_nsa, o_sb, nsa_out_norm_w[0], sb_out_norm_w[0], w_out[0], ffn_norm_w[0],
                           router_group_w[0], router_group_b[0], router_expert_w[0], router_expert_b[0])
    out = _moe(h2, logitsT, x1, w_gate[0], w_up[0], w_down[0], final_norm_w)
    return out.reshape(B, T, D)
```

```python
import functools

import jax
import jax.numpy as jnp
from jax import lax
from jax.experimental import pallas as pl
from jax.experimental.pallas import tpu as pltpu

HEAD_DIM = 64
NSA_HEADS = 8
NSA_KV_HEADS = 2
NSA_GROUP = NSA_HEADS // NSA_KV_HEADS
SB_HEADS = 8
ROPE_THETA = 500000.0
ROPE_DIM = HEAD_DIM // 4
ROPE_HALF = ROPE_DIM // 2
CMP_LEN = 32
CMP_STRIDE = 16
CMP_HIDDEN = 256
SEL_LEN = 64
SEL_TOP = 16
SEL_BONUS = 1.0e4
WINDOW = 512
Q_BLOCK = 128
N_GROUPS = 4
EXPERTS_PER_GROUP = 4
N_EXPERTS = N_GROUPS * EXPERTS_PER_GROUP
EXPERT_FF = 512
EPS = 1e-6
NEG = -1e30
LOWEST = -3.0e38

N_Q = NSA_HEADS * HEAD_DIM
N_KVG = NSA_KV_HEADS * HEAD_DIM
N_GATE = NSA_HEADS * 3
N_GATE_PAD = 32
N_SBH = SB_HEADS * HEAD_DIM
SCALE = HEAD_DIM ** -0.5
LOG2E = 1.4426950408889634
SEL_BLOCKS_PER_TILE = 8
BIAS_ROWS = 16
CMP_CLASSES = 8
SEL_V_ROWS = HEAD_DIM + 16

PROJ_ROWS = 512
SEL_TILE = 512
WIN_TILES = WINDOW // Q_BLOCK + 1
SB_TILE = 128
MOE_ROWS = 1024
MOE_EXPERTS_PER_STEP = 2
MOE_SLOT_ROWS = 128
MOE_SUB_SHIFT = 8
MOE_SUB = 1 << MOE_SUB_SHIFT
MOE_SUB_CAP = 48
ROUTER_LANES = 128
ROUTER_ROWS = 32
SB_FIRST_TILES = 3
SB_SKIP_LOG = -104.0

VMEM_LIMIT = 56 * 1024 * 1024

BF16 = jnp.bfloat16
F32 = jnp.float32


def _rms(x, w):
    return x * lax.rsqrt(jnp.mean(x * x, axis=-1, keepdims=True) + EPS) * w


def _dot(a, b):
    return jnp.dot(a, b, preferred_element_type=F32)


def _dot_nt(a, b):
    return lax.dot_general(a, b, (((1,), (1,)), ((), ())), preferred_element_type=F32)


def _split3(x):
    hi = x.astype(BF16)
    r1 = x - hi.astype(F32)
    mid = r1.astype(BF16)
    lo = (r1 - mid.astype(F32)).astype(BF16)
    return hi, mid, lo


def _proj_kernel(x_ref, nw_ref, wn_ref, wt_ref, pos_ref, invf_ref,
                 cmpk_ref, cmpv_ref, sbqT_ref, sbk_ref, sbvT_ref,
                 qT_ref, ksel_ref, kwin_ref, vselT_ref, vwinT_ref, gT_ref):
    h = _rms(x_ref[...], nw_ref[...]).astype(BF16)
    p1 = _dot(h, wn_ref[...])
    cmpk_ref[...] = p1[:, 0:N_KVG]
    cmpv_ref[...] = p1[:, N_KVG:2 * N_KVG]
    sbk_ref[...] = p1[:, 2 * N_KVG:2 * N_KVG + N_SBH].astype(BF16)

    p2 = _dot_nt(wt_ref[...], h)
    o = N_Q + 4 * N_KVG + N_GATE_PAD
    sbqT_ref[...] = (p2[o:o + N_SBH] * SCALE).astype(BF16)
    sbv = p2[o + N_SBH:o + 2 * N_SBH].astype(BF16)
    for j in range(PROJ_ROWS // Q_BLOCK):
        sbvT_ref[j] = sbv[:, j * Q_BLOCK:(j + 1) * Q_BLOCK]
    ang = invf_ref[...] * pos_ref[...].astype(F32)
    cos, sin = jnp.cos(ang), jnp.sin(ang)
    n_rope_heads = NSA_HEADS + 2 * NSA_KV_HEADS
    roped = []
    for hd in range(n_rope_heads):
        blk = p2[hd * HEAD_DIM:(hd + 1) * HEAD_DIM]
        x1, x2 = blk[0:ROPE_HALF], blk[ROPE_HALF:ROPE_DIM]
        roped.append(jnp.concatenate(
            [x1 * cos - x2 * sin, x2 * cos + x1 * sin, blk[ROPE_DIM:]], axis=0))
    qT_ref[...] = (jnp.concatenate(roped[:NSA_HEADS], axis=0) * (SCALE * LOG2E)).astype(BF16)
    kT = jnp.concatenate(roped[NSA_HEADS:], axis=0)
    kn = kT.T.astype(BF16)
    r_blk = lax.broadcasted_iota(jnp.int32, (PROJ_ROWS, HEAD_DIM), 0) >> 6
    c_idx = lax.broadcasted_iota(jnp.int32, (PROJ_ROWS, HEAD_DIM), 1)
    onehot = jnp.where(r_blk == c_idx, 1.0, 0.0).astype(BF16)
    for gk in range(NSA_KV_HEADS):
        ksel_ref[gk] = jnp.concatenate([kn[:, gk * HEAD_DIM:(gk + 1) * HEAD_DIM], onehot], axis=1)
    kwin_ref[...] = kn[:, N_KVG:2 * N_KVG]
    o = N_Q + 2 * N_KVG
    for gk in range(NSA_KV_HEADS):
        vselT_ref[0, gk] = jnp.concatenate(
            [p2[o + gk * HEAD_DIM:o + (gk + 1) * HEAD_DIM],
             jnp.where(lax.broadcasted_iota(jnp.int32, (SEL_V_ROWS - HEAD_DIM, PROJ_ROWS), 0) == 0, 1.0, 0.0)],
            axis=0).astype(BF16)
    vw = p2[o + N_KVG:o + 2 * N_KVG].astype(BF16)
    for j in range(PROJ_ROWS // Q_BLOCK):
        vwinT_ref[j] = vw[:, j * Q_BLOCK:(j + 1) * Q_BLOCK]
    o = o + 2 * N_KVG
    gT_ref[...] = jax.nn.sigmoid(p2[o:o + N_GATE_PAD])


def _project(x, positions, attn_norm_w, w_in):
    T, D = x.shape
    R = PROJ_ROWS
    o1, o2, o3 = N_Q, N_Q + 6 * N_KVG, N_Q + 6 * N_KVG + N_GATE
    kv = lambda i: w_in[:, o1 + i * N_KVG:o1 + (i + 1) * N_KVG]
    sb = lambda i: w_in[:, o3 + i * N_SBH:o3 + (i + 1) * N_SBH]
    w_nat = jnp.concatenate([kv(0), kv(1), sb(1)], axis=1).astype(BF16)
    w_t = jnp.concatenate(
        [w_in[:, :o1], kv(2), kv(4), kv(3), kv(5), w_in[:, o2:o3],
         jnp.zeros((D, N_GATE_PAD - N_GATE), w_in.dtype), sb(0), sb(2)], axis=1).T.astype(BF16)
    inv_freq = ROPE_THETA ** (-jnp.arange(0, ROPE_DIM, 2, dtype=F32) / ROPE_DIM)
    n_nat, n_t = w_nat.shape[1], w_t.shape[0]
    full = lambda shape: pl.BlockSpec(shape, lambda i: (0,) * len(shape))
    rows = lambda n: pl.BlockSpec((R, n), lambda i: (i, 0))
    cols = lambda n: pl.BlockSpec((n, R), lambda i: (0, i))
    return pl.pallas_call(
        _proj_kernel,
        grid=(T // R,),
        in_specs=[rows(D), full((1, D)), full((D, n_nat)), full((n_t, D)), cols(1), full((ROPE_HALF, 1))],
        out_specs=[rows(N_KVG), rows(N_KVG), cols(N_SBH), rows(N_SBH),
                   pl.BlockSpec((R // Q_BLOCK, N_SBH, Q_BLOCK), lambda i: (i, 0, 0)),
                   cols(N_Q), pl.BlockSpec((NSA_KV_HEADS, R, N_KVG), lambda i: (0, i, 0)), rows(N_KVG),
                   pl.BlockSpec((1, NSA_KV_HEADS, SEL_V_ROWS, R), lambda i: (i, 0, 0, 0)),
                   pl.BlockSpec((R // Q_BLOCK, N_KVG, Q_BLOCK), lambda i: (i, 0, 0)),
                   cols(N_GATE_PAD)],
        out_shape=[jax.ShapeDtypeStruct((T, N_KVG), F32), jax.ShapeDtypeStruct((T, N_KVG), F32),
                   jax.ShapeDtypeStruct((N_SBH, T), BF16), jax.ShapeDtypeStruct((T, N_SBH), BF16),
                   jax.ShapeDtypeStruct((T // Q_BLOCK, N_SBH, Q_BLOCK), BF16),
                   jax.ShapeDtypeStruct((N_Q, T), BF16),
                   jax.ShapeDtypeStruct((NSA_KV_HEADS, T, N_KVG), BF16), jax.ShapeDtypeStruct((T, N_KVG), BF16),
                   jax.ShapeDtypeStruct((T // R, NSA_KV_HEADS, SEL_V_ROWS, R), BF16),
                   jax.ShapeDtypeStruct((T // Q_BLOCK, N_KVG, Q_BLOCK), BF16),
                   jax.ShapeDtypeStruct((N_GATE_PAD, T), F32)],
        compiler_params=pltpu.CompilerParams(dimension_semantics=("arbitrary",),
                                             vmem_limit_bytes=VMEM_LIMIT),
    )(x, attn_norm_w.reshape(1, D), w_nat, w_t, positions.reshape(1, T), inv_freq.reshape(ROPE_HALF, 1))


def _compress_kernel(x_ref, pea_ref, peb_ref, wa_ref, wb_ref, w2_ref, nat_ref, tr_ref):
    x = x_ref[0]
    nc = x.shape[0]
    ha = _dot((x + pea_ref[0]).astype(BF16), wa_ref[0])
    hb = _dot((x + peb_ref[0]).astype(BF16), wb_ref[0])
    hid = ha + pltpu.roll(hb, nc - 1, 0)
    act = (hid * jax.nn.sigmoid(hid)).astype(BF16)
    out = _dot(act, w2_ref[0])
    nat_ref[0] = out.astype(BF16)
    tr_ref[0] = out.T.astype(BF16)


def _compress(cmpk, cmpv, pe_k, pe_v, k_w1, k_w2, v_w1, v_w2):
    T = cmpk.shape[0]
    nc = T // CMP_STRIDE
    half = CMP_LEN // 2
    G = NSA_KV_HEADS
    width = half * N_KVG
    x = jnp.stack([cmpk.reshape(nc, width), cmpv.reshape(nc, width)])
    eye = jnp.eye(G, dtype=F32)

    def pe_rows(pe):
        return jnp.broadcast_to(pe[:, None, :], (half, G, HEAD_DIM)).reshape(1, width)

    def w1_block(w1):
        w = w1.reshape(half, HEAD_DIM, CMP_HIDDEN)
        return jnp.einsum('ldj,gh->lgdhj', w, eye).reshape(width, G * CMP_HIDDEN).astype(BF16)

    def w2_block(w2):
        return jnp.einsum('jd,gh->gjhd', w2, eye).reshape(G * CMP_HIDDEN, N_KVG).astype(BF16)

    hw = half * HEAD_DIM
    pea = jnp.stack([pe_rows(pe_k[:half]), pe_rows(pe_v[:half])])
    peb = jnp.stack([pe_rows(pe_k[half:]), pe_rows(pe_v[half:])])
    wa = jnp.stack([w1_block(k_w1[:hw]), w1_block(v_w1[:hw])])
    wb = jnp.stack([w1_block(k_w1[hw:]), w1_block(v_w1[hw:])])
    w2 = jnp.stack([w2_block(k_w2), w2_block(v_w2)])
    blk = lambda a, b: pl.BlockSpec((1, a, b), lambda i: (i, 0, 0))
    nat, tr = pl.pallas_call(
        _compress_kernel,
        grid=(2,),
        in_specs=[blk(nc, width), blk(1, width), blk(1, width), blk(width, G * CMP_HIDDEN),
                  blk(width, G * CMP_HIDDEN), blk(G * CMP_HIDDEN, N_KVG)],
        out_specs=[blk(nc, N_KVG), blk(N_KVG, nc)],
        out_shape=[jax.ShapeDtypeStruct((2, nc, N_KVG), BF16), jax.ShapeDtypeStruct((2, N_KVG, nc), BF16)],
        compiler_params=pltpu.CompilerParams(dimension_semantics=("arbitrary",),
                                             vmem_limit_bytes=VMEM_LIMIT),
    )(x, pea, peb, wa, wb, w2)
    return nat[0], tr[1]


def _nsa_kernel(q_ref, kc_ref, vcT_ref, ovT_ref, ksel_ref, vselT_ref, *rest):
    kwin_refs = rest[0:WIN_TILES]
    vwin_refs = rest[WIN_TILES:2 * WIN_TILES]
    gT_ref, o_ref = rest[2 * WIN_TILES:2 * WIN_TILES + 2]
    scratch = rest[2 * WIN_TILES + 2:]
    G = NSA_KV_HEADS
    per = len(scratch) // G
    bias_scr, s0_scr, s1_scr, t0_scr, t1_scr, m_scr, acc_scr, oc_scr = (
        [scratch[g * per + n] for g in range(G)] for n in range(per))
    groups = range(G)
    i = pl.program_id(0)
    R, Q = NSA_GROUP, Q_BLOCK
    L = R * Q
    nsel = bias_scr[0].shape[0]

    zero = jnp.zeros((HEAD_DIM, L), BF16)
    qg, qz = [], []
    for g in groups:
        q = jnp.concatenate([q_ref[(g * R + r) * HEAD_DIM:(g * R + r + 1) * HEAD_DIM, :] for r in range(R)], axis=1)
        qg.append(q)
        qz.append(jnp.concatenate([q if gg == g else zero for gg in groups], axis=0))

    def own_rows(x, g):
        return x[g * HEAD_DIM:(g + 1) * HEAD_DIM]

    t_lane = i * Q + (lax.broadcasted_iota(jnp.int32, (1, L), 1) & (Q - 1))

    nc = kc_ref.shape[0]
    q_pos = i * Q + lax.broadcasted_iota(jnp.int32, (1, Q), 1)
    bq = (q_pos >> 6).astype(F32)

    def compress_and_select(n_eff, first_class):
        m_eff = n_eff // (SEL_LEN // CMP_STRIDE)
        cmp_end = lax.broadcasted_iota(jnp.int32, (n_eff, 1), 0) * CMP_STRIDE + (CMP_LEN - 1)
        visible = cmp_end <= t_lane
        sc = [_dot(kc_ref[0:n_eff, :], qz[g]) for g in groups]
        pc = []
        for g in groups:
            s = jnp.where(visible, sc[g], NEG)
            mxc = jnp.max(s, axis=0, keepdims=True)
            mxc = jnp.where(mxc < 0.5 * NEG, 0.0, mxc)
            ec = jnp.exp2(s - mxc)
            pc.append(ec * (1.0 / jnp.maximum(jnp.sum(ec, axis=0, keepdims=True), 1e-30)))
        for g in groups:
            oc_scr[g][...] = own_rows(_dot(vcT_ref[:, 0:n_eff], pc[g].astype(BF16)), g)
        ov = ovT_ref[0:m_eff, 0:n_eff]
        imp = []
        for g in groups:
            psum = pc[g][:, 0:Q]
            for r in range(1, R):
                psum = psum + pc[g][:, r * Q:(r + 1) * Q]
            imp.append(sum(_dot(ov, part) for part in _split3(psum)))
        m_idx = lax.broadcasted_iota(jnp.int32, (m_eff, Q), 0).astype(F32)
        allowed = m_idx <= bq
        forced = (m_idx == 0.0) | (m_idx == bq) | (m_idx == bq - 1.0)
        if first_class:
            score = [jnp.where(allowed, imp[g] + jnp.where(forced, SEL_BONUS, 0.0), NEG) for g in groups]
            n_pick = min(SEL_TOP, m_eff)
        else:
            free = allowed & jnp.logical_not(forced)
            score = [jnp.where(free, imp[g], NEG) for g in groups]
            n_pick = SEL_TOP - 3
        for _ in range(n_pick):
            for g in groups:
                best = jnp.max(score[g], axis=0, keepdims=True)
                first = jnp.min(jnp.where(score[g] == best, m_idx, float(m_eff)), axis=0, keepdims=True)
                score[g] = jnp.where(m_idx == first, LOWEST, score[g])
        for g in groups:
            picked = score[g] < 0.5 * LOWEST
            if not first_class:
                picked = picked | forced
            bias = jnp.where(allowed & picked, 0.0, NEG)
            bias_scr[g][0:m_eff, :] = jnp.concatenate([bias] * R, axis=1)
            if m_eff < nsel:
                bias_scr[g][m_eff:nsel, :] = jnp.full((nsel - m_eff, L), NEG, F32)

    sizes = sorted({max(Q, (nc * k // CMP_CLASSES) // Q * Q) for k in range(1, CMP_CLASSES + 1)})
    lo = 0
    for n_eff in sizes:
        hi = n_eff // (Q // CMP_STRIDE)
        pl.when((i >= lo) & (i < hi))(functools.partial(compress_and_select, n_eff, lo == 0))
        lo = hi

    for g in groups:
        m_scr[g][...] = jnp.full(m_scr[g].shape, NEG, F32)
        acc_scr[g][...] = jnp.zeros(acc_scr[g].shape, F32)
    rhs_pad = jnp.zeros((N_KVG - HEAD_DIM - BIAS_ROWS, L), BF16)
    bias_pad = jnp.zeros((BIAS_ROWS - SEL_BLOCKS_PER_TILE, L), F32)
    buf0, buf1 = (s0_scr, t0_scr), (s1_scr, t1_scr)

    def scores(kt, dst):
        for g in groups:
            brows = bias_scr[g][pl.ds(pl.multiple_of(kt * SEL_BLOCKS_PER_TILE, SEL_BLOCKS_PER_TILE),
                                      SEL_BLOCKS_PER_TILE), :]
            rhs = jnp.concatenate([qg[g], jnp.concatenate([brows, bias_pad], axis=0).astype(BF16), rhs_pad], axis=0)
            s = _dot(ksel_ref[g, kt], rhs)
            dst[0][g][...] = s
            dst[1][g][...] = jnp.max(s, axis=0, keepdims=True)

    def absorb(kt, src):
        for g in groups:
            m_old = m_scr[g][...]
            m_new = jnp.maximum(m_old, src[1][g][...])
            alpha = jnp.exp2(m_old - m_new)
            p = jnp.exp2(src[0][g][...] - m_new)
            acc_scr[g][...] = alpha * acc_scr[g][...] + _dot(vselT_ref[kt, g], p.astype(BF16))
            m_scr[g][...] = m_new

    def step(kt, src, dst):
        scores(kt + 1, dst)
        absorb(kt, src)

    def two_steps(kt):
        step(kt, buf0, buf1)
        step(kt + 1, buf1, buf0)

    k_row = lax.broadcasted_iota(jnp.int32, (Q, 1), 0)
    q_lane = lax.broadcasted_iota(jnp.int32, (1, L), 1) & (Q - 1)
    n_full = (i * Q) // SEL_TILE

    def last_tile(src):
        r0 = pl.multiple_of(i * Q - n_full * SEL_TILE, Q)
        for g in groups:
            src[0][g][pl.ds(r0, Q), :] = jnp.where(k_row <= q_lane, src[0][g][pl.ds(r0, Q), :], NEG)
            src[1][g][...] = jnp.max(src[0][g][...], axis=0, keepdims=True)
        absorb(n_full, src)

    odd = n_full & 1

    @pl.when(odd == 0)
    def _():
        scores(0, buf0)

    @pl.when(odd == 1)
    def _():
        scores(0, buf1)
        step(0, buf1, buf0)

    n_pairs = n_full >> 1

    def four_steps(j, carry):
        two_steps(odd + 4 * j)
        two_steps(odd + 4 * j + 2)
        return carry

    lax.fori_loop(0, n_pairs >> 1, four_steps, 0)

    @pl.when((n_pairs & 1) == 1)
    def _():
        two_steps(odd + 4 * (n_pairs >> 1))

    last_tile(buf0)

    sw = [[] for _ in groups]
    for w in range(WIN_TILES):
        j = i - (WIN_TILES - 1) + w
        for g in groups:
            s = _dot(kwin_refs[w][0], qz[g])
            if w == 0:
                s = jnp.where(k_row > q_lane, s, NEG)
            if w == WIN_TILES - 1:
                s = jnp.where(k_row <= q_lane, s, NEG)
            else:
                s = jnp.where(j >= 0, s, NEG)
            sw[g].append(s)
    ew, denw = [], []
    for g in groups:
        mxw = functools.reduce(jnp.maximum, [jnp.max(s, axis=0, keepdims=True) for s in sw[g]])
        ew.append([jnp.exp2(s - mxw) for s in sw[g]])
        denw.append(jnp.maximum(sum(jnp.sum(e, axis=0, keepdims=True) for e in ew[g]), 1e-30))
    v_window = jnp.concatenate([vwin_refs[w][0] for w in range(WIN_TILES)], axis=1)
    owT = []
    for g in groups:
        prod = _dot(v_window, jnp.concatenate([e.astype(BF16) for e in ew[g]], axis=0))
        owT.append(own_rows(prod, g) * (1.0 / denw[g]))

    def gate_row(g, j):
        return jnp.concatenate([gT_ref[(g * R + r) * 3 + j:(g * R + r) * 3 + j + 1, :] for r in range(R)], axis=1)

    o_rows = []
    for g in groups:
        osT = acc_scr[g][0:HEAD_DIM, :] * (1.0 / acc_scr[g][HEAD_DIM:HEAD_DIM + 1, :])
        oT = gate_row(g, 0) * oc_scr[g][...] + gate_row(g, 1) * osT + gate_row(g, 2) * owT[g]
        o_rows += [oT[:, r * Q:(r + 1) * Q] for r in range(R)]
    o_ref[...] = jnp.concatenate(o_rows, axis=0).T


def _nsa(qT, kc, vcT, ksel, vselT, kwin, vwinT, gT):
    T = qT.shape[1]
    nb = T // Q_BLOCK
    nc = kc.shape[0]
    nsel = T // SEL_LEN
    ntile = T // SEL_TILE
    n = jnp.arange(nc)[None, :] * CMP_STRIDE
    m = jnp.arange(nsel)[:, None] * SEL_LEN
    ovT = ((n < m + SEL_LEN) & (n + CMP_LEN > m)).astype(BF16)
    ksel4 = ksel.reshape(NSA_KV_HEADS, ntile, SEL_TILE, N_KVG)
    kwin3 = kwin.reshape(nb, Q_BLOCK, N_KVG)
    L = NSA_GROUP * Q_BLOCK
    const = lambda shape: pl.BlockSpec(shape, lambda i: (0,) * len(shape))
    win_tile = lambda i, w: (jnp.maximum(i - (WIN_TILES - 1) + w, 0), 0, 0)
    kwin_specs = [pl.BlockSpec((1, Q_BLOCK, N_KVG), functools.partial(win_tile, w=w)) for w in range(WIN_TILES)]
    vwin_specs = [pl.BlockSpec((1, N_KVG, Q_BLOCK), functools.partial(win_tile, w=w)) for w in range(WIN_TILES)]
    group_scratch = [pltpu.VMEM((nsel, L), F32), pltpu.VMEM((SEL_TILE, L), F32), pltpu.VMEM((SEL_TILE, L), F32),
                     pltpu.VMEM((1, L), F32), pltpu.VMEM((1, L), F32),
                     pltpu.VMEM((1, L), F32), pltpu.VMEM((SEL_V_ROWS, L), F32), pltpu.VMEM((HEAD_DIM, L), F32)]
    return pl.pallas_call(
        _nsa_kernel,
        grid=(nb,),
        in_specs=[pl.BlockSpec((N_Q, Q_BLOCK), lambda i: (0, i)),
                  const((nc, N_KVG)), const((N_KVG, nc)), const((nsel, nc)),
                  const((NSA_KV_HEADS, ntile, SEL_TILE, N_KVG)), const((ntile, NSA_KV_HEADS, SEL_V_ROWS, SEL_TILE))]
                 + kwin_specs + vwin_specs
                 + [pl.BlockSpec((N_GATE_PAD, Q_BLOCK), lambda i: (0, i))],
        out_specs=pl.BlockSpec((Q_BLOCK, N_Q), lambda i: (i, 0)),
        out_shape=jax.ShapeDtypeStruct((T, N_Q), F32),
        scratch_shapes=group_scratch * NSA_KV_HEADS,
        compiler_params=pltpu.CompilerParams(dimension_semantics=("arbitrary",),
                                             vmem_limit_bytes=VMEM_LIMIT),
    )(qT, kc, vcT, ovT, ksel4, vselT, *([kwin3] * WIN_TILES), *([vwinT] * WIN_TILES), gT)


def _sb_kernel(qT_ref, k_ref, vT_ref, o_ref, *scr):
    i = pl.program_id(0)
    Q, W = Q_BLOCK, 2 * HEAD_DIM
    pairs = SB_HEADS // 2
    blk_row = lax.broadcasted_iota(jnp.int32, (W, 2 * Q), 0) < HEAD_DIM
    blk_lane = lax.broadcasted_iota(jnp.int32, (W, 2 * Q), 1) < Q
    zero = jnp.zeros((), BF16)
    q_pairs = []
    for pr in range(pairs):
        x = qT_ref[pr * W:(pr + 1) * W, :]
        q_pairs.append(jnp.where(blk_row == blk_lane, jnp.concatenate([x, x], axis=1), zero))
    k_row = lax.broadcasted_iota(jnp.int32, (SB_TILE, 1), 0)
    q_lane = lax.broadcasted_iota(jnp.int32, (1, SB_HEADS * Q), 1) & (Q - 1)
    rr = lax.broadcasted_iota(jnp.int32, (SB_TILE, SB_TILE), 0)
    cc = lax.broadcasted_iota(jnp.int32, (SB_TILE, SB_TILE), 1)
    from_here = (cc >= rr).astype(BF16)
    acc_scrs, c_scrs = scr[0:pairs], scr[pairs:pairs + 2]
    for buf in scr:
        buf[...] = jnp.zeros(buf.shape, F32)

    halves = ((0, 1), (2, 3))
    causal = k_row < q_lane[:, 0:4 * Q]

    def walk(tiles):
        rows = [pl.ds(pl.multiple_of(kt * SB_TILE, SB_TILE), SB_TILE) for kt, _, _ in tiles]
        chains = [(t, hf) for t in range(len(tiles)) for hf in range(2)]
        z, d, suffix = {}, {}, {}
        for t, hf in chains:
            z[t, hf] = jnp.concatenate(
                [_dot(k_ref[rows[t], pr * W:(pr + 1) * W], q_pairs[pr]) for pr in halves[hf]], axis=1)
        for t, hf in chains:
            x = jnp.maximum(z[t, hf], 0.0) + jnp.log(1.0 + jnp.exp(-jnp.abs(z[t, hf])))
            if tiles[t][1]:
                x = jnp.where(causal, x, 0.0)
            if tiles[t][2] is not None:
                x = jnp.where(tiles[t][2], x, 0.0)
            d[t, hf] = x
            hi = x.astype(BF16)
            lo = (x - hi.astype(F32)).astype(BF16)
            suffix[t, hf] = _dot(from_here, hi) + _dot(from_here, lo)
        least = None
        for hf in range(2):
            c = c_scrs[hf][...]
            for t, (kt, own, exists) in enumerate(tiles):
                a = jnp.exp(z[t, hf] - suffix[t, hf] - c)
                if own:
                    a = jnp.where(causal, a, 0.0)
                if exists is not None:
                    a = jnp.where(exists, a, 0.0)
                a = a.astype(BF16)
                for n, pr in enumerate(halves[hf]):
                    acc_scrs[pr][...] = acc_scrs[pr][...] + _dot(vT_ref[kt, pr * W:(pr + 1) * W, :],
                                                                 a[:, n * 2 * Q:(n + 1) * 2 * Q])
                c = c + jnp.sum(d[t, hf], axis=0, keepdims=True)
            c_scrs[hf][...] = c
            least = c if least is None else jnp.minimum(least, c)
        return -jnp.min(least)

    first = [(i, True, None)] + [(jnp.maximum(i - n, 0), False, i - n >= 0) for n in range(1, SB_FIRST_TILES)]
    worst0 = walk(first)

    def cond(carry):
        kt, worst = carry
        return (kt >= 0) & (worst >= SB_SKIP_LOG)

    def body(carry):
        kt, _ = carry
        return kt - 1, walk([(kt, False, None)])

    lax.while_loop(cond, body, (i - SB_FIRST_TILES, worst0))
    for pr in range(pairs):
        acc = acc_scrs[pr][...]
        o_ref[:, pr * W:(pr + 1) * W] = jnp.where(blk_row[:, 0:Q], acc[:, 0:Q], acc[:, Q:2 * Q]).T


def _stick_breaking(sbqT, sbk, sbvT):
    T = sbk.shape[0]
    nb = T // Q_BLOCK
    W = 2 * HEAD_DIM
    return pl.pallas_call(
        _sb_kernel,
        grid=(nb,),
        in_specs=[pl.BlockSpec((N_SBH, Q_BLOCK), lambda i: (0, i)),
                  pl.BlockSpec((T, N_SBH), lambda i: (0, 0)),
                  pl.BlockSpec((nb, N_SBH, Q_BLOCK), lambda i: (0, 0, 0))],
        out_specs=pl.BlockSpec((Q_BLOCK, N_SBH), lambda i: (i, 0)),
        out_shape=jax.ShapeDtypeStruct((T, N_SBH), F32),
        scratch_shapes=[pltpu.VMEM((W, 2 * Q_BLOCK), F32)] * (SB_HEADS // 2) + [pltpu.VMEM((1, 4 * Q_BLOCK), F32)] * 2,
        compiler_params=pltpu.CompilerParams(dimension_semantics=("arbitrary",),
                                             vmem_limit_bytes=VMEM_LIMIT),
    )(sbqT, sbk, sbvT)


def _mix_kernel(x_ref, on_ref, os_ref, nwn_ref, nws_ref, wo_ref, fw_ref, rhi_ref, rlo_ref, rb_ref,
                x1_ref, h2_ref, lgT_ref):
    n1 = _rms(on_ref[...], nwn_ref[...]).astype(BF16)
    n2 = _rms(os_ref[...], nws_ref[...]).astype(BF16)
    x1 = x_ref[...] + _dot(n1, wo_ref[0:N_Q]) + _dot(n2, wo_ref[N_Q:N_Q + N_SBH])
    x1_ref[...] = x1
    h2 = _rms(x1, fw_ref[...])
    hi = h2.astype(BF16)
    lo = (h2 - hi.astype(F32)).astype(BF16)
    h2_ref[...] = hi
    lg = _dot(hi, rhi_ref[...]) + _dot(hi, rlo_ref[...]) + _dot(lo, rhi_ref[...]) + rb_ref[...]
    lgT_ref[...] = lg.T[0:ROUTER_ROWS]


def _mix(x, o_nsa, o_sb, nsa_norm_w, sb_norm_w, w_out, ffn_norm_w, rg_w, rg_b, re_w, re_b):
    T, D = x.shape
    R = PROJ_ROWS
    pad = ROUTER_LANES - N_GROUPS - N_EXPERTS
    wr = jnp.concatenate([rg_w, re_w, jnp.zeros((D, pad), F32)], axis=1)
    wr_hi = wr.astype(BF16)
    wr_lo = (wr - wr_hi.astype(F32)).astype(BF16)
    rb = jnp.concatenate([rg_b, re_b, jnp.zeros((pad,), F32)]).reshape(1, ROUTER_LANES)
    full = lambda shape: pl.BlockSpec(shape, lambda i: (0,) * len(shape))
    rows = lambda n: pl.BlockSpec((R, n), lambda i: (i, 0))
    return pl.pallas_call(
        _mix_kernel,
        grid=(T // R,),
        in_specs=[rows(D), rows(N_Q), rows(N_SBH), full((1, N_Q)), full((1, N_SBH)), full((N_Q + N_SBH, D)),
                  full((1, D)), full((D, ROUTER_LANES)), full((D, ROUTER_LANES)), full((1, ROUTER_LANES))],
        out_specs=[rows(D), rows(D), pl.BlockSpec((ROUTER_ROWS, R), lambda i: (0, i))],
        out_shape=[jax.ShapeDtypeStruct((T, D), F32), jax.ShapeDtypeStruct((T, D), BF16),
                   jax.ShapeDtypeStruct((ROUTER_ROWS, T), F32)],
        compiler_params=pltpu.CompilerParams(dimension_semantics=("arbitrary",),
                                             vmem_limit_bytes=VMEM_LIMIT),
    )(x, o_nsa, o_sb, nsa_norm_w.reshape(1, N_Q), sb_norm_w.reshape(1, N_SBH), w_out.astype(BF16),
      ffn_norm_w.reshape(1, D), wr_hi, wr_lo, rb)


def _routing(lg, axis):
    pos_i = lax.broadcasted_iota(jnp.int32, lg.shape, axis)
    pos = pos_i.astype(F32)
    first_max = lambda v, mx: jnp.min(jnp.where(v == mx, pos, float(ROUTER_LANES)), axis=axis, keepdims=True)
    gl = jnp.where(pos_i < N_GROUPS, lg, -jnp.inf)
    gmax = jnp.max(gl, axis=axis, keepdims=True)
    grp = first_max(gl, gmax)
    g_gate = 1.0 / jnp.sum(jnp.exp(gl - gmax), axis=axis, keepdims=True)
    e_idx = pos_i - N_GROUPS
    e_grp = (e_idx >> 2).astype(F32)
    in_grp = (e_idx >= 0) & (e_idx < N_EXPERTS) & (e_grp == grp)
    el = jnp.where(in_grp, lg, -jnp.inf)
    top1 = jnp.max(el, axis=axis, keepdims=True)
    i1 = first_max(el, top1)
    el2 = jnp.where(pos == i1, -jnp.inf, el)
    top2 = jnp.max(el2, axis=axis, keepdims=True)
    i2 = first_max(el2, top2)
    e2 = jnp.exp(top2 - top1)
    w1 = 1.0 / (1.0 + e2)
    w2 = e2 / (1.0 + e2)
    weight = g_gate * (jnp.where(pos == i1, w1, 0.0) + jnp.where(pos == i2, w2, 0.0))
    routed = jnp.where(pos == i1, 1.0, 0.0) + jnp.where(pos == i2, 1.0, 0.0)
    return weight, routed


def _moe_kernel(h_ref, lgT_ref, x1_ref, before_ref, wg_ref, wu_ref, wd_ref, fw_ref, o_ref,
                acc_scr, rankT_scr, wparts_scr):
    e = pl.program_id(1)
    rows, width = h_ref.shape
    n_sub = rows // MOE_SUB

    @pl.when(e == 0)
    def _():
        acc_scr[...] = jnp.zeros(acc_scr.shape, F32)
        weight, routed = _routing(lgT_ref[...], 0)
        rank = _dot(routed.astype(BF16), before_ref[...])
        rankT_scr[...] = jnp.where(routed > 0.0, rank, -1.0)
        parts = [p.astype(F32) for p in _split3(weight)]
        pad = jnp.zeros((MOE_SLOT_ROWS - 3 * ROUTER_ROWS, rows), F32)
        wparts_scr[...] = jnp.concatenate(parts + [pad], axis=0).T.astype(BF16)

    experts = range(MOE_EXPERTS_PER_STEP)
    pos = [e * MOE_EXPERTS_PER_STEP + x + N_GROUPS for x in experts]
    rank_row = [rankT_scr[pl.ds(pos[x], 1), :] for x in experts]
    n_routed = jnp.max(functools.reduce(jnp.maximum, rank_row)).astype(jnp.int32) + 1
    slot_col = lax.broadcasted_iota(jnp.int32, (MOE_SLOT_ROWS, 1), 0)
    lane = lax.broadcasted_iota(jnp.int32, (n_sub * MOE_SUB_CAP, ROUTER_LANES), 1)
    y_pad = jnp.zeros((MOE_SLOT_ROWS - MOE_SUB_CAP, width), BF16)

    def chunk(ch, carry):
        want = jnp.where(slot_col < MOE_SUB_CAP, slot_col + ch * MOE_SUB_CAP, -2).astype(F32)
        xg, w_slot = [], []
        got = [[] for _ in experts]
        got_w = [[] for _ in experts]
        spread = [[] for _ in experts]
        for s in range(n_sub):
            sub = slice(s * MOE_SUB, (s + 1) * MOE_SUB)
            onehot = [jnp.where(rank_row[x][:, sub] == want, 1.0, 0.0) for x in experts]
            picks = jnp.concatenate([onehot[x][0:MOE_SUB_CAP] for x in experts], axis=0).astype(BF16)
            rows_h = _dot(picks, h_ref[sub, :])
            rows_w = _dot(picks, wparts_scr[sub, :])
            for x in experts:
                got[x].append(rows_h[x * MOE_SUB_CAP:(x + 1) * MOE_SUB_CAP])
                got_w[x].append(rows_w[x * MOE_SUB_CAP:(x + 1) * MOE_SUB_CAP])
                spread[x].append(onehot[x].T.astype(BF16))
        for x in experts:
            xg.append(jnp.concatenate(got[x], axis=0).astype(BF16))
            mine = (lane == pos[x]) | (lane == pos[x] + ROUTER_ROWS) | (lane == pos[x] + 2 * ROUTER_ROWS)
            w_slot.append(jnp.sum(jnp.where(mine, jnp.concatenate(got_w[x], axis=0), 0.0), axis=1, keepdims=True))
        a = [_dot(xg[x], wg_ref[x]) for x in experts]
        b = [_dot(xg[x], wu_ref[x]) for x in experts]
        act = [(a[x] * jax.nn.sigmoid(a[x]) * b[x]).astype(BF16) for x in experts]
        y = [(_dot(act[x], wd_ref[x]) * w_slot[x]).astype(BF16) for x in experts]
        for s in range(n_sub):
            back = jnp.concatenate([spread[x][s] for x in experts], axis=1)
            vals = jnp.concatenate([jnp.concatenate([y[x][s * MOE_SUB_CAP:(s + 1) * MOE_SUB_CAP], y_pad], axis=0)
                                    for x in experts], axis=0)
            acc_scr[s * MOE_SUB:(s + 1) * MOE_SUB, :] += _dot(back, vals)
        return carry

    lax.fori_loop(0, (n_routed + (MOE_SUB_CAP - 1)) // MOE_SUB_CAP, chunk, 0)

    @pl.when(e == N_EXPERTS // MOE_EXPERTS_PER_STEP - 1)
    def _():
        o_ref[...] = _rms(x1_ref[...] + acc_scr[...], fw_ref[...])


def _moe(h2, logitsT, x1, w_gate, w_up, w_down, final_norm_w):
    T, D = x1.shape
    R = min(MOE_ROWS, T)
    tok = jnp.arange(R)
    same_sub = (tok[:, None] >> MOE_SUB_SHIFT) == (tok[None, :] >> MOE_SUB_SHIFT)
    before = ((tok[:, None] < tok[None, :]) & same_sub).astype(BF16)
    rows = lambda n: pl.BlockSpec((R, n), lambda i, e: (i, 0))
    const = lambda a, b: pl.BlockSpec((a, b), lambda i, e: (0, 0))
    per_expert = lambda a, b: pl.BlockSpec((MOE_EXPERTS_PER_STEP, a, b), lambda i, e: (e, 0, 0))
    return pl.pallas_call(
        _moe_kernel,
        grid=(T // R, N_EXPERTS // MOE_EXPERTS_PER_STEP),
        in_specs=[rows(D), pl.BlockSpec((ROUTER_ROWS, R), lambda i, e: (0, i)), rows(D), const(R, R),
                  per_expert(D, EXPERT_FF), per_expert(D, EXPERT_FF), per_expert(EXPERT_FF, D), const(1, D)],
        out_specs=rows(D),
        out_shape=jax.ShapeDtypeStruct((T, D), F32),
        scratch_shapes=[pltpu.VMEM((R, D), F32), pltpu.VMEM((ROUTER_ROWS, R), F32),
                        pltpu.VMEM((R, ROUTER_LANES), BF16)],
        compiler_params=pltpu.CompilerParams(dimension_semantics=("arbitrary", "arbitrary"),
                                             vmem_limit_bytes=VMEM_LIMIT),
    )(h2, logitsT, x1, before, w_gate.astype(BF16), w_up.astype(BF16), w_down.astype(BF16),
      final_norm_w.reshape(1, D))


def kernel(x, positions, attn_norm_w, w_in, cmp_pe_k, cmp_pe_v, cmp_k_w1, cmp_k_w2, cmp_v_w1, cmp_v_w2,
           nsa_out_norm_w, sb_out_norm_w, w_out, ffn_norm_w, router_group_w, router_group_b,
           router_expert_w, router_expert_b, w_gate, w_up, w_down, final_norm_w):
    B, T, D = x.shape
    assert B == 1 and T % SEL_TILE == 0 and T % PROJ_ROWS == 0 and T // SEL_LEN >= SEL_TOP
    assert attn_norm_w.shape[0] == 1, "the final norm is fused into the (single) layer's MoE kernel"
    xs = x.reshape(T, D)
    pos = positions.reshape(T)
    (cmpk, cmpv, sbq, sbk, sbv, qT, ksel, kwin, vselT, vwinT, gT) = _project(xs, pos, attn_norm_w[0], w_in[0])
    kc, vcT = _compress(cmpk, cmpv, cmp_pe_k[0], cmp_pe_v[0], cmp_k_w1[0], cmp_k_w2[0], cmp_v_w1[0], cmp_v_w2[0])
    o_nsa = _nsa(qT, kc, vcT, ksel, vselT, kwin, vwinT, gT)
    o_sb = _stick_breaking(sbq, sbk, sbv)
    x1, h2, logitsT = _mix(xs, o_nsa, o_sb, nsa_out_norm_w[0], sb_out_norm_w[0], w_out[0], ffn_norm_w[0],
                           router_group_w[0], router_group_b[0], router_expert_w[0], router_expert_b[0])
    out = _moe(h2, logitsT, x1, w_gate[0], w_up[0], w_down[0], final_norm_w)
    return out.reshape(B, T, D)
```

```python
import functools

import jax
import jax.numpy as jnp
from jax import lax
from jax.experimental import pallas as pl
from jax.experimental.pallas import tpu as pltpu

HEAD_DIM = 64
NSA_HEADS = 8
NSA_KV_HEADS = 2
NSA_GROUP = NSA_HEADS // NSA_KV_HEADS
SB_HEADS = 8
ROPE_THETA = 500000.0
ROPE_DIM = HEAD_DIM // 4
ROPE_HALF = ROPE_DIM // 2
CMP_LEN = 32
CMP_STRIDE = 16
CMP_HIDDEN = 256
SEL_LEN = 64
SEL_TOP = 16
SEL_BONUS = 1.0e4
WINDOW = 512
Q_BLOCK = 128
N_GROUPS = 4
EXPERTS_PER_GROUP = 4
N_EXPERTS = N_GROUPS * EXPERTS_PER_GROUP
EXPERT_FF = 512
EPS = 1e-6
NEG = -1e30
LOWEST = -3.0e38

N_Q = NSA_HEADS * HEAD_DIM
N_KVG = NSA_KV_HEADS * HEAD_DIM
N_GATE = NSA_HEADS * 3
N_GATE_PAD = 32
N_SBH = SB_HEADS * HEAD_DIM
SCALE = HEAD_DIM ** -0.5
LOG2E = 1.4426950408889634
SEL_BLOCKS_PER_TILE = 8
BIAS_ROWS = 16
CMP_CLASSES = 8
SEL_V_ROWS = HEAD_DIM + 16

PROJ_ROWS = 512
SEL_TILE = 512
WIN_TILES = WINDOW // Q_BLOCK + 1
SB_TILE = 128
MOE_ROWS = 1024
MOE_EXPERTS_PER_STEP = 2
MOE_SLOT_ROWS = 128
MOE_SUB_SHIFT = 8
MOE_SUB = 1 << MOE_SUB_SHIFT
MOE_SUB_CAP = 48
ROUTER_LANES = 128
ROUTER_ROWS = 32
SB_FIRST_TILES = 3
SB_SKIP_LOG = -104.0

VMEM_LIMIT = 56 * 1024 * 1024

BF16 = jnp.bfloat16
F32 = jnp.float32


def _rms(x, w):
    return x * lax.rsqrt(jnp.mean(x * x, axis=-1, keepdims=True) + EPS) * w


def _dot(a, b):
    return jnp.dot(a, b, preferred_element_type=F32)


def _dot_nt(a, b):
    return lax.dot_general(a, b, (((1,), (1,)), ((), ())), preferred_element_type=F32)


def _split3(x):
    hi = x.astype(BF16)
    r1 = x - hi.astype(F32)
    mid = r1.astype(BF16)
    lo = (r1 - mid.astype(F32)).astype(BF16)
    return hi, mid, lo


def _proj_kernel(x_ref, nw_ref, wn_ref, wt_ref, pos_ref, invf_ref,
                 cmpk_ref, cmpv_ref, sbqT_ref, sbk_ref, sbvT_ref,
                 qT_ref, ksel_ref, kwin_ref, vselT_ref, vwinT_ref, gT_ref):
    h = _rms(x_ref[...], nw_ref[...]).astype(BF16)
    p1 = _dot(h, wn_ref[...])
    cmpk_ref[...] = p1[:, 0:N_KVG]
    cmpv_ref[...] = p1[:, N_KVG:2 * N_KVG]
    sbk_ref[...] = p1[:, 2 * N_KVG:2 * N_KVG + N_SBH].astype(BF16)

    p2 = _dot_nt(wt_ref[...], h)
    o = N_Q + 4 * N_KVG + N_GATE_PAD
    sbqT_ref[...] = (p2[o:o + N_SBH] * SCALE).astype(BF16)
    sbv = p2[o + N_SBH:o + 2 * N_SBH].astype(BF16)
    for j in range(PROJ_ROWS // Q_BLOCK):
        sbvT_ref[j] = sbv[:, j * Q_BLOCK:(j + 1) * Q_BLOCK]
    ang = invf_ref[...] * pos_ref[...].astype(F32)
    cos, sin = jnp.cos(ang), jnp.sin(ang)
    n_rope_heads = NSA_HEADS + 2 * NSA_KV_HEADS
    roped = []
    for hd in range(n_rope_heads):
        blk = p2[hd * HEAD_DIM:(hd + 1) * HEAD_DIM]
        x1, x2 = blk[0:ROPE_HALF], blk[ROPE_HALF:ROPE_DIM]
        roped.append(jnp.concatenate(
            [x1 * cos - x2 * sin, x2 * cos + x1 * sin, blk[ROPE_DIM:]], axis=0))
    qT_ref[...] = (jnp.concatenate(roped[:NSA_HEADS], axis=0) * (SCALE * LOG2E)).astype(BF16)
    kT = jnp.concatenate(roped[NSA_HEADS:], axis=0)
    kn = kT.T.astype(BF16)
    r_blk = lax.broadcasted_iota(jnp.int32, (PROJ_ROWS, HEAD_DIM), 0) >> 6
    c_idx = lax.broadcasted_iota(jnp.int32, (PROJ_ROWS, HEAD_DIM), 1)
    onehot = jnp.where(r_blk == c_idx, 1.0, 0.0).astype(BF16)
    for gk in range(NSA_KV_HEADS):
        ksel_ref[gk] = jnp.concatenate([kn[:, gk * HEAD_DIM:(gk + 1) * HEAD_DIM], onehot], axis=1)
    kwin_ref[...] = kn[:, N_KVG:2 * N_KVG]
    o = N_Q + 2 * N_KVG
    for gk in range(NSA_KV_HEADS):
        vselT_ref[0, gk] = jnp.concatenate(
            [p2[o + gk * HEAD_DIM:o + (gk + 1) * HEAD_DIM],
             jnp.where(lax.broadcasted_iota(jnp.int32, (SEL_V_ROWS - HEAD_DIM, PROJ_ROWS), 0) == 0, 1.0, 0.0)],
            axis=0).astype(BF16)
    vw = p2[o + N_KVG:o + 2 * N_KVG].astype(BF16)
    for j in range(PROJ_ROWS // Q_BLOCK):
        vwinT_ref[j] = vw[:, j * Q_BLOCK:(j + 1) * Q_BLOCK]
    o = o + 2 * N_KVG
    gT_ref[...] = jax.nn.sigmoid(p2[o:o + N_GATE_PAD])


def _project(x, positions, attn_norm_w, w_in):
    T, D = x.shape
    R = PROJ_ROWS
    o1, o2, o3 = N_Q, N_Q + 6 * N_KVG, N_Q + 6 * N_KVG + N_GATE
    kv = lambda i: w_in[:, o1 + i * N_KVG:o1 + (i + 1) * N_KVG]
    sb = lambda i: w_in[:, o3 + i * N_SBH:o3 + (i + 1) * N_SBH]
    w_nat = jnp.concatenate([kv(0), kv(1), sb(1)], axis=1).astype(BF16)
    w_t = jnp.concatenate(
        [w_in[:, :o1], kv(2), kv(4), kv(3), kv(5), w_in[:, o2:o3],
         jnp.zeros((D, N_GATE_PAD - N_GATE), w_in.dtype), sb(0), sb(2)], axis=1).T.astype(BF16)
    inv_freq = ROPE_THETA ** (-jnp.arange(0, ROPE_DIM, 2, dtype=F32) / ROPE_DIM)
    n_nat, n_t = w_nat.shape[1], w_t.shape[0]
    full = lambda shape: pl.BlockSpec(shape, lambda i: (0,) * len(shape))
    rows = lambda n: pl.BlockSpec((R, n), lambda i: (i, 0))
    cols = lambda n: pl.BlockSpec((n, R), lambda i: (0, i))
    return pl.pallas_call(
        _proj_kernel,
        grid=(T // R,),
        in_specs=[rows(D), full((1, D)), full((D, n_nat)), full((n_t, D)), cols(1), full((ROPE_HALF, 1))],
        out_specs=[rows(N_KVG), rows(N_KVG), cols(N_SBH), rows(N_SBH),
                   pl.BlockSpec((R // Q_BLOCK, N_SBH, Q_BLOCK), lambda i: (i, 0, 0)),
                   cols(N_Q), pl.BlockSpec((NSA_KV_HEADS, R, N_KVG), lambda i: (0, i, 0)), rows(N_KVG),
                   pl.BlockSpec((1, NSA_KV_HEADS, SEL_V_ROWS, R), lambda i: (i, 0, 0, 0)),
                   pl.BlockSpec((R // Q_BLOCK, N_KVG, Q_BLOCK), lambda i: (i, 0, 0)),
                   cols(N_GATE_PAD)],
        out_shape=[jax.ShapeDtypeStruct((T, N_KVG), F32), jax.ShapeDtypeStruct((T, N_KVG), F32),
                   jax.ShapeDtypeStruct((N_SBH, T), BF16), jax.ShapeDtypeStruct((T, N_SBH), BF16),
                   jax.ShapeDtypeStruct((T // Q_BLOCK, N_SBH, Q_BLOCK), BF16),
                   jax.ShapeDtypeStruct((N_Q, T), BF16),
                   jax.ShapeDtypeStruct((NSA_KV_HEADS, T, N_KVG), BF16), jax.ShapeDtypeStruct((T, N_KVG), BF16),
                   jax.ShapeDtypeStruct((T // R, NSA_KV_HEADS, SEL_V_ROWS, R), BF16),
                   jax.ShapeDtypeStruct((T // Q_BLOCK, N_KVG, Q_BLOCK), BF16),
                   jax.ShapeDtypeStruct((N_GATE_PAD, T), F32)],
        compiler_params=pltpu.CompilerParams(dimension_semantics=("arbitrary",),
                                             vmem_limit_bytes=VMEM_LIMIT),
    )(x, attn_norm_w.reshape(1, D), w_nat, w_t, positions.reshape(1, T), inv_freq.reshape(ROPE_HALF, 1))


def _compress_kernel(x_ref, pea_ref, peb_ref, wa_ref, wb_ref, w2_ref, nat_ref, tr_ref):
    x = x_ref[0]
    nc = x.shape[0]
    ha = _dot((x + pea_ref[0]).astype(BF16), wa_ref[0])
    hb = _dot((x + peb_ref[0]).astype(BF16), wb_ref[0])
    hid = ha + pltpu.roll(hb, nc - 1, 0)
    act = (hid * jax.nn.sigmoid(hid)).astype(BF16)
    out = _dot(act, w2_ref[0])
    nat_ref[0] = out.astype(BF16)
    tr_ref[0] = out.T.astype(BF16)


def _compress(cmpk, cmpv, pe_k, pe_v, k_w1, k_w2, v_w1, v_w2):
    T = cmpk.shape[0]
    nc = T // CMP_STRIDE
    half = CMP_LEN // 2
    G = NSA_KV_HEADS
    width = half * N_KVG
    x = jnp.stack([cmpk.reshape(nc, width), cmpv.reshape(nc, width)])
    eye = jnp.eye(G, dtype=F32)

    def pe_rows(pe):
        return jnp.broadcast_to(pe[:, None, :], (half, G, HEAD_DIM)).reshape(1, width)

    def w1_block(w1):
        w = w1.reshape(half, HEAD_DIM, CMP_HIDDEN)
        return jnp.einsum('ldj,gh->lgdhj', w, eye).reshape(width, G * CMP_HIDDEN).astype(BF16)

    def w2_block(w2):
        return jnp.einsum('jd,gh->gjhd', w2, eye).reshape(G * CMP_HIDDEN, N_KVG).astype(BF16)

    hw = half * HEAD_DIM
    pea = jnp.stack([pe_rows(pe_k[:half]), pe_rows(pe_v[:half])])
    peb = jnp.stack([pe_rows(pe_k[half:]), pe_rows(pe_v[half:])])
    wa = jnp.stack([w1_block(k_w1[:hw]), w1_block(v_w1[:hw])])
    wb = jnp.stack([w1_block(k_w1[hw:]), w1_block(v_w1[hw:])])
    w2 = jnp.stack([w2_block(k_w2), w2_block(v_w2)])
    blk = lambda a, b: pl.BlockSpec((1, a, b), lambda i: (i, 0, 0))
    nat, tr = pl.pallas_call(
        _compress_kernel,
        grid=(2,),
        in_specs=[blk(nc, width), blk(1, width), blk(1, width), blk(width, G * CMP_HIDDEN),
                  blk(width, G * CMP_HIDDEN), blk(G * CMP_HIDDEN, N_KVG)],
        out_specs=[blk(nc, N_KVG), blk(N_KVG, nc)],
        out_shape=[jax.ShapeDtypeStruct((2, nc, N_KVG), BF16), jax.ShapeDtypeStruct((2, N_KVG, nc), BF16)],
        compiler_params=pltpu.CompilerParams(dimension_semantics=("arbitrary",),
                                             vmem_limit_bytes=VMEM_LIMIT),
    )(x, pea, peb, wa, wb, w2)
    return nat[0], tr[1]


def _nsa_kernel(q_ref, kc_ref, vcT_ref, ovT_ref, ksel_ref, vselT_ref, *rest):
    kwin_refs = rest[0:WIN_TILES]
    vwin_refs = rest[WIN_TILES:2 * WIN_TILES]
    gT_ref, o_ref = rest[2 * WIN_TILES:2 * WIN_TILES + 2]
    scratch = rest[2 * WIN_TILES + 2:]
    G = NSA_KV_HEADS
    per = len(scratch) // G
    bias_scr, s0_scr, s1_scr, t0_scr, t1_scr, m_scr, acc_scr, oc_scr = (
        [scratch[g * per + n] for g in range(G)] for n in range(per))
    groups = range(G)
    i = pl.program_id(0)
    R, Q = NSA_GROUP, Q_BLOCK
    L = R * Q
    nsel = bias_scr[0].shape[0]

    zero = jnp.zeros((HEAD_DIM, L), BF16)
    qg, qz = [], []
    for g in groups:
        q = jnp.concatenate([q_ref[(g * R + r) * HEAD_DIM:(g * R + r + 1) * HEAD_DIM, :] for r in range(R)], axis=1)
        qg.append(q)
        qz.append(jnp.concatenate([q if gg == g else zero for gg in groups], axis=0))

    def own_rows(x, g):
        return x[g * HEAD_DIM:(g + 1) * HEAD_DIM]

    t_lane = i * Q + (lax.broadcasted_iota(jnp.int32, (1, L), 1) & (Q - 1))

    nc = kc_ref.shape[0]
    q_pos = i * Q + lax.broadcasted_iota(jnp.int32, (1, Q), 1)
    bq = (q_pos >> 6).astype(F32)

    def compress_and_select(n_eff, first_class):
        m_eff = n_eff // (SEL_LEN // CMP_STRIDE)
        cmp_end = lax.broadcasted_iota(jnp.int32, (n_eff, 1), 0) * CMP_STRIDE + (CMP_LEN - 1)
        visible = cmp_end <= t_lane
        sc = [_dot(kc_ref[0:n_eff, :], qz[g]) for g in groups]
        pc = []
        for g in groups:
            s = jnp.where(visible, sc[g], NEG)
            mxc = jnp.max(s, axis=0, keepdims=True)
            mxc = jnp.where(mxc < 0.5 * NEG, 0.0, mxc)
            ec = jnp.exp2(s - mxc)
            pc.append(ec * (1.0 / jnp.maximum(jnp.sum(ec, axis=0, keepdims=True), 1e-30)))
        for g in groups:
            oc_scr[g][...] = own_rows(_dot(vcT_ref[:, 0:n_eff], pc[g].astype(BF16)), g)
        ov = ovT_ref[0:m_eff, 0:n_eff]
        imp = []
        for g in groups:
            psum = pc[g][:, 0:Q]
            for r in range(1, R):
                psum = psum + pc[g][:, r * Q:(r + 1) * Q]
            imp.append(sum(_dot(ov, part) for part in _split3(psum)))
        m_idx = lax.broadcasted_iota(jnp.int32, (m_eff, Q), 0).astype(F32)
        allowed = m_idx <= bq
        forced = (m_idx == 0.0) | (m_idx == bq) | (m_idx == bq - 1.0)
        if first_class:
            score = [jnp.where(allowed, imp[g] + jnp.where(forced, SEL_BONUS, 0.0), NEG) for g in groups]
            n_pick = min(SEL_TOP, m_eff)
        else:
            free = allowed & jnp.logical_not(forced)
            score = [jnp.where(free, imp[g], NEG) for g in groups]
            n_pick = SEL_TOP - 3
        for _ in range(n_pick):
            for g in groups:
                best = jnp.max(score[g], axis=0, keepdims=True)
                first = jnp.min(jnp.where(score[g] == best, m_idx, float(m_eff)), axis=0, keepdims=True)
                score[g] = jnp.where(m_idx == first, LOWEST, score[g])
        for g in groups:
            picked = score[g] < 0.5 * LOWEST
            if not first_class:
                picked = picked | forced
            bias = jnp.where(allowed & picked, 0.0, NEG)
            bias_scr[g][0:m_eff, :] = jnp.concatenate([bias] * R, axis=1)
            if m_eff < nsel:
                bias_scr[g][m_eff:nsel, :] = jnp.full((nsel - m_eff, L), NEG, F32)

    sizes = sorted({max(Q, (nc * k // CMP_CLASSES) // Q * Q) for k in range(1, CMP_CLASSES + 1)})
    lo = 0
    for n_eff in sizes:
        hi = n_eff // (Q // CMP_STRIDE)
        pl.when((i >= lo) & (i < hi))(functools.partial(compress_and_select, n_eff, lo == 0))
        lo = hi

    for g in groups:
        m_scr[g][...] = jnp.full(m_scr[g].shape, NEG, F32)
        acc_scr[g][...] = jnp.zeros(acc_scr[g].shape, F32)
    rhs_pad = jnp.zeros((N_KVG - HEAD_DIM - BIAS_ROWS, L), BF16)
    bias_pad = jnp.zeros((BIAS_ROWS - SEL_BLOCKS_PER_TILE, L), F32)
    buf0, buf1 = (s0_scr, t0_scr), (s1_scr, t1_scr)

    def scores(kt, dst):
        for g in groups:
            brows = bias_scr[g][pl.ds(pl.multiple_of(kt * SEL_BLOCKS_PER_TILE, SEL_BLOCKS_PER_TILE),
                                      SEL_BLOCKS_PER_TILE), :]
            rhs = jnp.concatenate([qg[g], jnp.concatenate([brows, bias_pad], axis=0).astype(BF16), rhs_pad], axis=0)
            s = _dot(ksel_ref[g, kt], rhs)
            dst[0][g][...] = s
            dst[1][g][...] = jnp.max(s, axis=0, keepdims=True)

    def absorb(kt, src):
        for g in groups:
            m_old = m_scr[g][...]
            m_new = jnp.maximum(m_old, src[1][g][...])
            alpha = jnp.exp2(m_old - m_new)
            p = jnp.exp2(src[0][g][...] - m_new)
            acc_scr[g][...] = alpha * acc_scr[g][...] + _dot(vselT_ref[kt, g], p.astype(BF16))
            m_scr[g][...] = m_new

    def step(kt, src, dst):
        scores(kt + 1, dst)
        absorb(kt, src)

    def two_steps(kt):
        step(kt, buf0, buf1)
        step(kt + 1, buf1, buf0)

    k_row = lax.broadcasted_iota(jnp.int32, (Q, 1), 0)
    q_lane = lax.broadcasted_iota(jnp.int32, (1, L), 1) & (Q - 1)
    n_full = (i * Q) // SEL_TILE

    def last_tile(src):
        r0 = pl.multiple_of(i * Q - n_full * SEL_TILE, Q)
        for g in groups:
            src[0][g][pl.ds(r0, Q), :] = jnp.where(k_row <= q_lane, src[0][g][pl.ds(r0, Q), :], NEG)
            src[1][g][...] = jnp.max(src[0][g][...], axis=0, keepdims=True)
        absorb(n_full, src)

    odd = n_full & 1

    @pl.when(odd == 0)
    def _():
        scores(0, buf0)

    @pl.when(odd == 1)
    def _():
        scores(0, buf1)
        step(0, buf1, buf0)

    n_pairs = n_full >> 1

    def four_steps(j, carry):
        two_steps(odd + 4 * j)
        two_steps(odd + 4 * j + 2)
        return carry

    lax.fori_loop(0, n_pairs >> 1, four_steps, 0)

    @pl.when((n_pairs & 1) == 1)
    def _():
        two_steps(odd + 4 * (n_pairs >> 1))

    last_tile(buf0)

    k_window = jnp.concatenate([kwin_refs[w][0] for w in range(WIN_TILES)], axis=0)
    s_window = _dot(k_window, jnp.concatenate(qz, axis=1))
    sw = [[] for _ in groups]
    for w in range(WIN_TILES):
        j = i - (WIN_TILES - 1) + w
        for g in groups:
            s = s_window[w * Q:(w + 1) * Q, g * L:(g + 1) * L]
            if w == 0:
                s = jnp.where(k_row > q_lane, s, NEG)
            if w == WIN_TILES - 1:
                s = jnp.where(k_row <= q_lane, s, NEG)
            else:
                s = jnp.where(j >= 0, s, NEG)
            sw[g].append(s)
    ew, denw = [], []
    for g in groups:
        mxw = functools.reduce(jnp.maximum, [jnp.max(s, axis=0, keepdims=True) for s in sw[g]])
        ew.append([jnp.exp2(s - mxw) for s in sw[g]])
        denw.append(jnp.maximum(sum(jnp.sum(e, axis=0, keepdims=True) for e in ew[g]), 1e-30))
    v_window = jnp.concatenate([vwin_refs[w][0] for w in range(WIN_TILES)], axis=1)
    owT = []
    for g in groups:
        prod = _dot(v_window, jnp.concatenate([e.astype(BF16) for e in ew[g]], axis=0))
        owT.append(own_rows(prod, g) * (1.0 / denw[g]))

    def gate_row(g, j):
        return jnp.concatenate([gT_ref[(g * R + r) * 3 + j:(g * R + r) * 3 + j + 1, :] for r in range(R)], axis=1)

    o_rows = []
    for g in groups:
        osT = acc_scr[g][0:HEAD_DIM, :] * (1.0 / acc_scr[g][HEAD_DIM:HEAD_DIM + 1, :])
        oT = gate_row(g, 0) * oc_scr[g][...] + gate_row(g, 1) * osT + gate_row(g, 2) * owT[g]
        o_rows += [oT[:, r * Q:(r + 1) * Q] for r in range(R)]
    o_ref[...] = jnp.concatenate(o_rows, axis=0).T


def _nsa(qT, kc, vcT, ksel, vselT, kwin, vwinT, gT):
    T = qT.shape[1]
    nb = T // Q_BLOCK
    nc = kc.shape[0]
    nsel = T // SEL_LEN
    ntile = T // SEL_TILE
    n = jnp.arange(nc)[None, :] * CMP_STRIDE
    m = jnp.arange(nsel)[:, None] * SEL_LEN
    ovT = ((n < m + SEL_LEN) & (n + CMP_LEN > m)).astype(BF16)
    ksel4 = ksel.reshape(NSA_KV_HEADS, ntile, SEL_TILE, N_KVG)
    kwin3 = kwin.reshape(nb, Q_BLOCK, N_KVG)
    L = NSA_GROUP * Q_BLOCK
    const = lambda shape: pl.BlockSpec(shape, lambda i: (0,) * len(shape))
    win_tile = lambda i, w: (jnp.maximum(i - (WIN_TILES - 1) + w, 0), 0, 0)
    kwin_specs = [pl.BlockSpec((1, Q_BLOCK, N_KVG), functools.partial(win_tile, w=w)) for w in range(WIN_TILES)]
    vwin_specs = [pl.BlockSpec((1, N_KVG, Q_BLOCK), functools.partial(win_tile, w=w)) for w in range(WIN_TILES)]
    group_scratch = [pltpu.VMEM((nsel, L), F32), pltpu.VMEM((SEL_TILE, L), F32), pltpu.VMEM((SEL_TILE, L), F32),
                     pltpu.VMEM((1, L), F32), pltpu.VMEM((1, L), F32),
                     pltpu.VMEM((1, L), F32), pltpu.VMEM((SEL_V_ROWS, L), F32), pltpu.VMEM((HEAD_DIM, L), F32)]
    return pl.pallas_call(
        _nsa_kernel,
        grid=(nb,),
        in_specs=[pl.BlockSpec((N_Q, Q_BLOCK), lambda i: (0, i)),
                  const((nc, N_KVG)), const((N_KVG, nc)), const((nsel, nc)),
                  const((NSA_KV_HEADS, ntile, SEL_TILE, N_KVG)), const((ntile, NSA_KV_HEADS, SEL_V_ROWS, SEL_TILE))]
                 + kwin_specs + vwin_specs
                 + [pl.BlockSpec((N_GATE_PAD, Q_BLOCK), lambda i: (0, i))],
        out_specs=pl.BlockSpec((Q_BLOCK, N_Q), lambda i: (i, 0)),
        out_shape=jax.ShapeDtypeStruct((T, N_Q), F32),
        scratch_shapes=group_scratch * NSA_KV_HEADS,
        compiler_params=pltpu.CompilerParams(dimension_semantics=("arbitrary",),
                                             vmem_limit_bytes=VMEM_LIMIT),
    )(qT, kc, vcT, ovT, ksel4, vselT, *([kwin3] * WIN_TILES), *([vwinT] * WIN_TILES), gT)


def _sb_kernel(qT_ref, k_ref, vT_ref, o_ref, *scr):
    i = pl.program_id(0)
    Q, W = Q_BLOCK, 2 * HEAD_DIM
    pairs = SB_HEADS // 2
    blk_row = lax.broadcasted_iota(jnp.int32, (W, 2 * Q), 0) < HEAD_DIM
    blk_lane = lax.broadcasted_iota(jnp.int32, (W, 2 * Q), 1) < Q
    zero = jnp.zeros((), BF16)
    q_pairs = []
    for pr in range(pairs):
        x = qT_ref[pr * W:(pr + 1) * W, :]
        q_pairs.append(jnp.where(blk_row == blk_lane, jnp.concatenate([x, x], axis=1), zero))
    k_row = lax.broadcasted_iota(jnp.int32, (SB_TILE, 1), 0)
    q_lane = lax.broadcasted_iota(jnp.int32, (1, SB_HEADS * Q), 1) & (Q - 1)
    rr = lax.broadcasted_iota(jnp.int32, (SB_TILE, SB_TILE), 0)
    cc = lax.broadcasted_iota(jnp.int32, (SB_TILE, SB_TILE), 1)
    from_here = (cc >= rr).astype(BF16)
    acc_scrs, c_scrs = scr[0:pairs], scr[pairs:pairs + 2]
    for buf in scr:
        buf[...] = jnp.zeros(buf.shape, F32)

    halves = ((0, 1), (2, 3))
    causal = k_row < q_lane[:, 0:4 * Q]

    def walk(tiles):
        rows = [pl.ds(pl.multiple_of(kt * SB_TILE, SB_TILE), SB_TILE) for kt, _, _ in tiles]
        chains = [(t, hf) for t in range(len(tiles)) for hf in range(2)]
        z, d, suffix = {}, {}, {}
        for t, hf in chains:
            z[t, hf] = jnp.concatenate(
                [_dot(k_ref[rows[t], pr * W:(pr + 1) * W], q_pairs[pr]) for pr in halves[hf]], axis=1)
        for t, hf in chains:
            x = jnp.maximum(z[t, hf], 0.0) + jnp.log(1.0 + jnp.exp(-jnp.abs(z[t, hf])))
            if tiles[t][1]:
                x = jnp.where(causal, x, 0.0)
            if tiles[t][2] is not None:
                x = jnp.where(tiles[t][2], x, 0.0)
            d[t, hf] = x
            hi = x.astype(BF16)
            lo = (x - hi.astype(F32)).astype(BF16)
            suffix[t, hf] = _dot(from_here, hi) + _dot(from_here, lo)
        least = None
        for hf in range(2):
            c = c_scrs[hf][...]
            for t, (kt, own, exists) in enumerate(tiles):
                a = jnp.exp(z[t, hf] - suffix[t, hf] - c)
                if own:
                    a = jnp.where(causal, a, 0.0)
                if exists is not None:
                    a = jnp.where(exists, a, 0.0)
                a = a.astype(BF16)
                for n, pr in enumerate(halves[hf]):
                    acc_scrs[pr][...] = acc_scrs[pr][...] + _dot(vT_ref[kt, pr * W:(pr + 1) * W, :],
                                                                 a[:, n * 2 * Q:(n + 1) * 2 * Q])
                c = c + jnp.sum(d[t, hf], axis=0, keepdims=True)
            c_scrs[hf][...] = c
            least = c if least is None else jnp.minimum(least, c)
        return -jnp.min(least)

    first = [(i, True, None)] + [(jnp.maximum(i - n, 0), False, i - n >= 0) for n in range(1, SB_FIRST_TILES)]
    worst0 = walk(first)

    def cond(carry):
        kt, worst = carry
        return (kt >= 0) & (worst >= SB_SKIP_LOG)

    def body(carry):
        kt, _ = carry
        return kt - 1, walk([(kt, False, None)])

    lax.while_loop(cond, body, (i - SB_FIRST_TILES, worst0))
    for pr in range(pairs):
        acc = acc_scrs[pr][...]
        o_ref[:, pr * W:(pr + 1) * W] = jnp.where(blk_row[:, 0:Q], acc[:, 0:Q], acc[:, Q:2 * Q]).T


def _stick_breaking(sbqT, sbk, sbvT):
    T = sbk.shape[0]
    nb = T // Q_BLOCK
    W = 2 * HEAD_DIM
    return pl.pallas_call(
        _sb_kernel,
        grid=(nb,),
        in_specs=[pl.BlockSpec((N_SBH, Q_BLOCK), lambda i: (0, i)),
                  pl.BlockSpec((T, N_SBH), lambda i: (0, 0)),
                  pl.BlockSpec((nb, N_SBH, Q_BLOCK), lambda i: (0, 0, 0))],
        out_specs=pl.BlockSpec((Q_BLOCK, N_SBH), lambda i: (i, 0)),
        out_shape=jax.ShapeDtypeStruct((T, N_SBH), F32),
        scratch_shapes=[pltpu.VMEM((W, 2 * Q_BLOCK), F32)] * (SB_HEADS // 2) + [pltpu.VMEM((1, 4 * Q_BLOCK), F32)] * 2,
        compiler_params=pltpu.CompilerParams(dimension_semantics=("arbitrary",),
                                             vmem_limit_bytes=VMEM_LIMIT),
    )(sbqT, sbk, sbvT)


def _mix_kernel(x_ref, on_ref, os_ref, nwn_ref, nws_ref, wo_ref, fw_ref, rhi_ref, rlo_ref, rb_ref,
                x1_ref, h2_ref, lgT_ref):
    n1 = _rms(on_ref[...], nwn_ref[...]).astype(BF16)
    n2 = _rms(os_ref[...], nws_ref[...]).astype(BF16)
    x1 = x_ref[...] + _dot(n1, wo_ref[0:N_Q]) + _dot(n2, wo_ref[N_Q:N_Q + N_SBH])
    x1_ref[...] = x1
    h2 = _rms(x1, fw_ref[...])
    hi = h2.astype(BF16)
    lo = (h2 - hi.astype(F32)).astype(BF16)
    h2_ref[...] = hi
    lg = _dot(hi, rhi_ref[...]) + _dot(hi, rlo_ref[...]) + _dot(lo, rhi_ref[...]) + rb_ref[...]
    lgT_ref[...] = lg.T[0:ROUTER_ROWS]


def _mix(x, o_nsa, o_sb, nsa_norm_w, sb_norm_w, w_out, ffn_norm_w, rg_w, rg_b, re_w, re_b):
    T, D = x.shape
    R = PROJ_ROWS
    pad = ROUTER_LANES - N_GROUPS - N_EXPERTS
    wr = jnp.concatenate([rg_w, re_w, jnp.zeros((D, pad), F32)], axis=1)
    wr_hi = wr.astype(BF16)
    wr_lo = (wr - wr_hi.astype(F32)).astype(BF16)
    rb = jnp.concatenate([rg_b, re_b, jnp.zeros((pad,), F32)]).reshape(1, ROUTER_LANES)
    full = lambda shape: pl.BlockSpec(shape, lambda i: (0,) * len(shape))
    rows = lambda n: pl.BlockSpec((R, n), lambda i: (i, 0))
    return pl.pallas_call(
        _mix_kernel,
        grid=(T // R,),
        in_specs=[rows(D), rows(N_Q), rows(N_SBH), full((1, N_Q)), full((1, N_SBH)), full((N_Q + N_SBH, D)),
                  full((1, D)), full((D, ROUTER_LANES)), full((D, ROUTER_LANES)), full((1, ROUTER_LANES))],
        out_specs=[rows(D), rows(D), pl.BlockSpec((ROUTER_ROWS, R), lambda i: (0, i))],
        out_shape=[jax.ShapeDtypeStruct((T, D), F32), jax.ShapeDtypeStruct((T, D), BF16),
                   jax.ShapeDtypeStruct((ROUTER_ROWS, T), F32)],
        compiler_params=pltpu.CompilerParams(dimension_semantics=("arbitrary",),
                                             vmem_limit_bytes=VMEM_LIMIT),
    )(x, o_nsa, o_sb, nsa_norm_w.reshape(1, N_Q), sb_norm_w.reshape(1, N_SBH), w_out.astype(BF16),
      ffn_norm_w.reshape(1, D), wr_hi, wr_lo, rb)


def _routing(lg, axis):
    pos_i = lax.broadcasted_iota(jnp.int32, lg.shape, axis)
    pos = pos_i.astype(F32)
    first_max = lambda v, mx: jnp.min(jnp.where(v == mx, pos, float(ROUTER_LANES)), axis=axis, keepdims=True)
    gl = jnp.where(pos_i < N_GROUPS, lg, -jnp.inf)
    gmax = jnp.max(gl, axis=axis, keepdims=True)
    grp = first_max(gl, gmax)
    g_gate = 1.0 / jnp.sum(jnp.exp(gl - gmax), axis=axis, keepdims=True)
    e_idx = pos_i - N_GROUPS
    e_grp = (e_idx >> 2).astype(F32)
    in_grp = (e_idx >= 0) & (e_idx < N_EXPERTS) & (e_grp == grp)
    el = jnp.where(in_grp, lg, -jnp.inf)
    top1 = jnp.max(el, axis=axis, keepdims=True)
    i1 = first_max(el, top1)
    el2 = jnp.where(pos == i1, -jnp.inf, el)
    top2 = jnp.max(el2, axis=axis, keepdims=True)
    i2 = first_max(el2, top2)
    e2 = jnp.exp(top2 - top1)
    w1 = 1.0 / (1.0 + e2)
    w2 = e2 / (1.0 + e2)
    weight = g_gate * (jnp.where(pos == i1, w1, 0.0) + jnp.where(pos == i2, w2, 0.0))
    routed = jnp.where(pos == i1, 1.0, 0.0) + jnp.where(pos == i2, 1.0, 0.0)
    return weight, routed


def _moe_kernel(h_ref, lgT_ref, x1_ref, before_ref, wg_ref, wu_ref, wd_ref, fw_ref, o_ref,
                acc_scr, rankT_scr, wparts_scr):
    e = pl.program_id(1)
    rows, width = h_ref.shape
    n_sub = rows // MOE_SUB

    @pl.when(e == 0)
    def _():
        acc_scr[...] = jnp.zeros(acc_scr.shape, F32)
        weight, routed = _routing(lgT_ref[...], 0)
        rank = _dot(routed.astype(BF16), before_ref[...])
        rankT_scr[...] = jnp.where(routed > 0.0, rank, -1.0)
        parts = [p.astype(F32) for p in _split3(weight)]
        pad = jnp.zeros((MOE_SLOT_ROWS - 3 * ROUTER_ROWS, rows), F32)
        wparts_scr[...] = jnp.concatenate(parts + [pad], axis=0).T.astype(BF16)

    experts = range(MOE_EXPERTS_PER_STEP)
    pos = [e * MOE_EXPERTS_PER_STEP + x + N_GROUPS for x in experts]
    rank_row = [rankT_scr[pl.ds(pos[x], 1), :] for x in experts]
    n_routed = jnp.max(functools.reduce(jnp.maximum, rank_row)).astype(jnp.int32) + 1
    slot_col = lax.broadcasted_iota(jnp.int32, (MOE_SLOT_ROWS, 1), 0)
    lane = lax.broadcasted_iota(jnp.int32, (n_sub * MOE_SUB_CAP, ROUTER_LANES), 1)
    y_pad = jnp.zeros((MOE_SLOT_ROWS - MOE_SUB_CAP, width), BF16)

    def chunk(ch, carry):
        want = jnp.where(slot_col < MOE_SUB_CAP, slot_col + ch * MOE_SUB_CAP, -2).astype(F32)
        xg, w_slot = [], []
        got = [[] for _ in experts]
        got_w = [[] for _ in experts]
        spread = [[] for _ in experts]
        for s in range(n_sub):
            sub = slice(s * MOE_SUB, (s + 1) * MOE_SUB)
            onehot = [jnp.where(rank_row[x][:, sub] == want, 1.0, 0.0) for x in experts]
            picks = jnp.concatenate([onehot[x][0:MOE_SUB_CAP] for x in experts], axis=0).astype(BF16)
            rows_h = _dot(picks, h_ref[sub, :])
            rows_w = _dot(picks, wparts_scr[sub, :])
            for x in experts:
                got[x].append(rows_h[x * MOE_SUB_CAP:(x + 1) * MOE_SUB_CAP])
                got_w[x].append(rows_w[x * MOE_SUB_CAP:(x + 1) * MOE_SUB_CAP])
                spread[x].append(onehot[x].T.astype(BF16))
        for x in experts:
            xg.append(jnp.concatenate(got[x], axis=0).astype(BF16))
            mine = (lane == pos[x]) | (lane == pos[x] + ROUTER_ROWS) | (lane == pos[x] + 2 * ROUTER_ROWS)
            w_slot.append(jnp.sum(jnp.where(mine, jnp.concatenate(got_w[x], axis=0), 0.0), axis=1, keepdims=True))
        a = [_dot(xg[x], wg_ref[x]) for x in experts]
        b = [_dot(xg[x], wu_ref[x]) for x in experts]
        act = [(a[x] * jax.nn.sigmoid(a[x]) * b[x]).astype(BF16) for x in experts]
        y = [(_dot(act[x], wd_ref[x]) * w_slot[x]).astype(BF16) for x in experts]
        for s in range(n_sub):
            back = jnp.concatenate([spread[x][s] for x in experts], axis=1)
            vals = jnp.concatenate([jnp.concatenate([y[x][s * MOE_SUB_CAP:(s + 1) * MOE_SUB_CAP], y_pad], axis=0)
                                    for x in experts], axis=0)
            acc_scr[s * MOE_SUB:(s + 1) * MOE_SUB, :] += _dot(back, vals)
        return carry

    lax.fori_loop(0, (n_routed + (MOE_SUB_CAP - 1)) // MOE_SUB_CAP, chunk, 0)

    @pl.when(e == N_EXPERTS // MOE_EXPERTS_PER_STEP - 1)
    def _():
        o_ref[...] = _rms(x1_ref[...] + acc_scr[...], fw_ref[...])


def _moe(h2, logitsT, x1, w_gate, w_up, w_down, final_norm_w):
    T, D = x1.shape
    R = min(MOE_ROWS, T)
    tok = jnp.arange(R)
    same_sub = (tok[:, None] >> MOE_SUB_SHIFT) == (tok[None, :] >> MOE_SUB_SHIFT)
    before = ((tok[:, None] < tok[None, :]) & same_sub).astype(BF16)
    rows = lambda n: pl.BlockSpec((R, n), lambda i, e: (i, 0))
    const = lambda a, b: pl.BlockSpec((a, b), lambda i, e: (0, 0))
    per_expert = lambda a, b: pl.BlockSpec((MOE_EXPERTS_PER_STEP, a, b), lambda i, e: (e, 0, 0))
    return pl.pallas_call(
        _moe_kernel,
        grid=(T // R, N_EXPERTS // MOE_EXPERTS_PER_STEP),
        in_specs=[rows(D), pl.BlockSpec((ROUTER_ROWS, R), lambda i, e: (0, i)), rows(D), const(R, R),
                  per_expert(D, EXPERT_FF), per_expert(D, EXPERT_FF), per_expert(EXPERT_FF, D), const(1, D)],
        out_specs=rows(D),
        out_shape=jax.ShapeDtypeStruct((T, D), F32),
        scratch_shapes=[pltpu.VMEM((R, D), F32), pltpu.VMEM((ROUTER_ROWS, R), F32),
                        pltpu.VMEM((R, ROUTER_LANES), BF16)],
        compiler_params=pltpu.CompilerParams(dimension_semantics=("arbitrary", "arbitrary"),
                                             vmem_limit_bytes=VMEM_LIMIT),
    )(h2, logitsT, x1, before, w_gate.astype(BF16), w_up.astype(BF16), w_down.astype(BF16),
      final_norm_w.reshape(1, D))


def kernel(x, positions, attn_norm_w, w_in, cmp_pe_k, cmp_pe_v, cmp_k_w1, cmp_k_w2, cmp_v_w1, cmp_v_w2,
           nsa_out_norm_w, sb_out_norm_w, w_out, ffn_norm_w, router_group_w, router_group_b,
           router_expert_w, router_expert_b, w_gate, w_up, w_down, final_norm_w):
    B, T, D = x.shape
    assert B == 1 and T % SEL_TILE == 0 and T % PROJ_ROWS == 0 and T // SEL_LEN >= SEL_TOP
    assert attn_norm_w.shape[0] == 1, "the final norm is fused into the (single) layer's MoE kernel"
    xs = x.reshape(T, D)
    pos = positions.reshape(T)
    (cmpk, cmpv, sbq, sbk, sbv, qT, ksel, kwin, vselT, vwinT, gT) = _project(xs, pos, attn_norm_w[0], w_in[0])
    kc, vcT = _compress(cmpk, cmpv, cmp_pe_k[0], cmp_pe_v[0], cmp_k_w1[0], cmp_k_w2[0], cmp_v_w1[0], cmp_v_w2[0])
    o_nsa = _nsa(qT, kc, vcT, ksel, vselT, kwin, vwinT, gT)
    o_sb = _stick_breaking(sbq, sbk, sbv)
    x1, h2, logitsT = _mix(xs, o_nsa, o_sb, nsa_out_norm_w[0], sb_out_norm_w[0], w_out[0], ffn_norm_w[0],
                           router_group_w[0], router_group_b[0], router_expert_w[0], router_expert_b[0])
    out = _moe(h2, logitsT, x1, w_gate[0], w_up[0], w_down[0], final_norm_w)
    return out.reshape(B, T, D)
```
